```python
import math
import jax
import jax.numpy as jnp
from jax import lax
import numpy as np

D_MODEL = 1024
BATCH = 32
SEQ = 2048
DEPTH = 2
DEC_BATCH = 8
DEC_SEQ = 16
PAST_LEN = 1024

CHUNK = 64
Q_BLOCK = 128
N_EVEN = (DEPTH + 1) // 2
N_ODD = DEPTH // 2

A_HEADS = 4
A_QK_DIM = 64
A_V_DIM = 128
B_HEADS = 4
B_QK_DIM = 128
B_V_DIM = 128
C_HEADS = 4
C_QK_DIM = 128
C_V_DIM = 256
C_GATE_RANK = 16
C_GATE_TAU = 16.0
MEM_LEN = 256
M_HEADS = 4
M_HEAD_DIM = 256
REL_BUCKETS = 32
REL_MAX_DIST = 128
D_FF = 2816
CONV_W = 3
ROPE_BASE = 10000.0
EPS = 1e-6
NEG_INF = -1e30

A_Q = A_HEADS * 2 * A_QK_DIM
A_V = A_HEADS * A_V_DIM
B_QK = B_HEADS * B_QK_DIM
B_V = B_HEADS * B_V_DIM
EVEN_IN = 2 * A_Q + A_V + 2 * B_QK + 2 * B_V
EVEN_MIX = A_V + B_V
EVEN_SPLITS = (A_Q, 2 * A_Q, 2 * A_Q + A_V, 2 * A_Q + A_V + B_QK,
               2 * A_Q + A_V + 2 * B_QK, 2 * A_Q + A_V + 2 * B_QK + B_V)
C_QK = C_HEADS * C_QK_DIM
C_V = C_HEADS * C_V_DIM
ODD_IN = 2 * C_QK + 2 * C_V + C_GATE_RANK
ODD_SPLITS = (C_QK, 2 * C_QK, 2 * C_QK + C_V, 2 * C_QK + 2 * C_V)
M_WIDTH = M_HEADS * M_HEAD_DIM

kernel_name = "hybrid_streaming_encoder_step"

F32 = jnp.float32


def rmsnorm(x, g):
    xf = x.astype(F32)
    y = xf * lax.rsqrt(jnp.mean(xf * xf, axis=-1, keepdims=True) + EPS)
    return (y * g.astype(F32)).astype(x.dtype)


def head_rms(x):
    xf = x.astype(F32)
    return xf * lax.rsqrt(jnp.mean(xf * xf, axis=-1, keepdims=True) + EPS)


def rel_bucket(rel):
    nb = REL_BUCKETS // 2
    max_exact = nb // 2
    n = jnp.abs(rel)
    nf = jnp.maximum(n, 1).astype(F32)
    large = max_exact + (jnp.log(nf / max_exact) / math.log(REL_MAX_DIST / max_exact)
                         * (nb - max_exact)).astype(jnp.int32)
    large = jnp.minimum(large, nb - 1)
    return jnp.where(rel > 0, nb, 0) + jnp.where(n < max_exact, n, large)


def rotary(x, pos):
    half = x.shape[-1] // 2
    inv = ROPE_BASE ** (-jnp.arange(half, dtype=F32) / half)
    ang = pos.astype(F32)[:, None] * inv[None, :]
    cos = jnp.cos(ang)[None, :, None, :]
    sin = jnp.sin(ang)[None, :, None, :]
    xf = x.astype(F32)
    x1, x2 = xf[..., :half], xf[..., half:]
    return jnp.concatenate([x1 * cos - x2 * sin, x1 * sin + x2 * cos], axis=-1)


def diff_attn_block(q, k, v, q_pos, k_pos, lam, rel_bias):
    s = jnp.einsum("bqhcd,bkhcd->bchqk", q, k).astype(F32) * (A_QK_DIM ** -0.5)
    bias = jnp.transpose(rel_bias[rel_bucket(k_pos[None, :] - q_pos[:, None])], (2, 0, 1))
    allowed = (k_pos[None, :] // CHUNK) <= (q_pos[:, None] // CHUNK)
    s = jnp.where(allowed, s + bias.astype(F32), NEG_INF)
    p = jax.nn.softmax(s, axis=-1)
    w = p[:, 0] - lam * p[:, 1]
    return jnp.einsum("bhqk,bkhd->bqhd", w.astype(v.dtype), v)


def diff_attention(q, k, v, q_pos, k_pos, lam, rel_bias):
    bsz, tq = q.shape[0], q.shape[1]
    if tq > Q_BLOCK and tq % Q_BLOCK == 0:
        nblk = tq // Q_BLOCK
        qb = jnp.moveaxis(q.reshape((bsz, nblk, Q_BLOCK) + q.shape[2:]), 1, 0)
        pb = q_pos.reshape(nblk, Q_BLOCK)
        o = lax.map(lambda a: diff_attn_block(a[0], k, v, a[1], k_pos, lam, rel_bias), (qb, pb))
        return jnp.moveaxis(o, 0, 1).reshape(bsz, tq, A_HEADS, A_V_DIM)
    return diff_attn_block(q, k, v, q_pos, k_pos, lam, rel_bias)


def chunk_len(t):
    return CHUNK if t % CHUNK == 0 else t


def to_chunks(a, c):
    b, t = a.shape[0], a.shape[1]
    return jnp.moveaxis(a.reshape((b, t // c, c) + a.shape[2:]), 1, 0)


def from_chunks(a):
    n, b, c = a.shape[0], a.shape[1], a.shape[2]
    return jnp.moveaxis(a, 0, 1).reshape((b, n * c) + a.shape[3:])


def retention(q, k, v, s0):
    c = chunk_len(q.shape[1])
    log_g = jnp.log1p(-jnp.exp2(-5.0 - jnp.arange(B_HEADS, dtype=F32)))
    idx = jnp.arange(c, dtype=F32)
    dist = idx[:, None] - idx[None, :]
    decay = jnp.where(dist >= 0, jnp.exp(jnp.maximum(dist, 0.0)[None] * log_g[:, None, None]), 0.0)
    q_dec = jnp.exp((idx[:, None] + 1.0) * log_g[None, :])[None, :, :, None]
    k_dec = jnp.exp((c - 1.0 - idx)[:, None] * log_g[None, :])[None, :, :, None]
    c_dec = jnp.exp(c * log_g)[None, :, None, None]

    def step(s, inp):
        qc, kc, vc = inp
        att = jnp.einsum("bihd,bjhd->bhij", qc, kc) * decay[None]
        o = (jnp.einsum("bhij,bjhe->bihe", att, vc)
             + jnp.einsum("bihd,bhde->bihe", qc * q_dec, s))
        s = s * c_dec + jnp.einsum("bjhd,bjhe->bhde", kc * k_dec, vc)
        return s, o

    s, o = lax.scan(step, s0.astype(F32),
                    (to_chunks(q.astype(F32), c), to_chunks(k.astype(F32), c), to_chunks(v.astype(F32), c)))
    return from_chunks(o), s


def gla(q, k, v, log_a, s0):
    c = chunk_len(q.shape[1])
    causal = jnp.tril(jnp.ones((c, c), dtype=bool))[None, :, :, None, None]

    def step(s, inp):
        qc, kc, vc, gc = inp
        b = jnp.cumsum(gc, axis=1)
        rel = jnp.exp(jnp.where(causal, b[:, :, None] - b[:, None, :], -jnp.inf))
        att = jnp.einsum("bihd,bjhd,bijhd->bhij", qc, kc, rel)
        o = (jnp.einsum("bhij,bjhe->bihe", att, vc)
             + jnp.einsum("bihd,bhde->bihe", qc * jnp.exp(b), s))
        b_last = b[:, -1]
        s = (s * jnp.exp(b_last)[..., None]
             + jnp.einsum("bjhd,bjhe->bhde", kc * jnp.exp(b_last[:, None] - b), vc))
        return s, o

    def chunk_f32(a):
        return to_chunks(a.astype(F32), c)

    s, o = lax.scan(step, s0.astype(F32), (chunk_f32(q), chunk_f32(k), chunk_f32(v), chunk_f32(log_a)))
    return from_chunks(o), s


def memory_kv(mem, ln_mem, w_ck, w_cv):
    bsz, m, _ = mem.shape
    ks, vs = [], []
    for l in range(DEPTH):
        mn = rmsnorm(mem, ln_mem[l])
        ks.append((mn @ w_ck[l]).reshape(bsz, m, M_HEADS, M_HEAD_DIM))
        vs.append((mn @ w_cv[l]).reshape(bsz, m, M_HEADS, M_HEAD_DIM))
    return jnp.stack(ks), jnp.stack(vs)


def mem_attend(h, mk, mv, wq, wo):
    bsz, t, _ = h.shape
    q = (h @ wq).reshape(bsz, t, M_HEADS, M_HEAD_DIM)
    s = jnp.einsum("bthd,bmhd->bhtm", q, mk.astype(h.dtype)).astype(F32) * (M_HEAD_DIM ** -0.5)
    p = jax.nn.softmax(s, axis=-1)
    o = jnp.einsum("bhtm,bmhd->bthd", p.astype(h.dtype), mv.astype(h.dtype))
    return o.reshape(bsz, t, M_WIDTH) @ wo


def conv_ffn(h, prev, w_gate, w_up, conv_w, conv_b, w_down):
    t = h.shape[1]
    g = h @ w_gate
    gp = jnp.concatenate([prev.astype(g.dtype), g], axis=1)
    c = conv_b
    for j in range(CONV_W):
        c = c + conv_w[j] * gp[:, j:j + t]
    y = (jax.nn.gelu(c) * (h @ w_up)) @ w_down
    return y, gp[:, t:]


def forward(x, past_k, past_v, ret_state, gla_state, conv_prev, mem_k, mem_v, p):
    bsz, t, _ = x.shape
    past = past_k.shape[2]
    q_pos = past + jnp.arange(t, dtype=jnp.int32)
    k_pos = jnp.arange(past + t, dtype=jnp.int32)
    new_k, new_v, new_ret, new_gla, new_conv = [], [], [], [], []
    for l in range(DEPTH):
        h = rmsnorm(x, p["ln_mix"][l])
        if l % 2 == 0:
            e = l // 2
            z = h @ p["w_in_even"][e]
            aq, ak, av, bq, bk, bv, bg = jnp.split(z, EVEN_SPLITS, axis=-1)
            ak = ak.reshape(bsz, t, A_HEADS, A_V_DIM)
            av = av.reshape(bsz, t, A_HEADS, A_V_DIM)
            new_k.append(ak)
            new_v.append(av)
            k_all = jnp.concatenate([past_k[e].astype(x.dtype), ak], axis=1)
            v_all = jnp.concatenate([past_v[e].astype(x.dtype), av], axis=1)
            lam_init = 0.8 - 0.6 * math.exp(-0.3 * l)
            lam = (jnp.exp(jnp.sum(p["diff_lq1"][e].astype(F32) * p["diff_lk1"][e].astype(F32)))
                   - jnp.exp(jnp.sum(p["diff_lq2"][e].astype(F32) * p["diff_lk2"][e].astype(F32)))
                   + lam_init)
            o_a = diff_attention(aq.reshape(bsz, t, A_HEADS, 2, A_QK_DIM),
                                 k_all.reshape(bsz, past + t, A_HEADS, 2, A_QK_DIM),
                                 v_all, q_pos, k_pos, lam, p["rel_bias"])
            o_a = head_rms(o_a) * p["diff_subln"][e].astype(F32) * (1.0 - lam_init)
            qb = rotary(bq.reshape(bsz, t, B_HEADS, B_QK_DIM), q_pos) * (B_QK_DIM ** -0.5)
            kb = rotary(bk.reshape(bsz, t, B_HEADS, B_QK_DIM), q_pos)
            o_b, s_b = retention(qb, kb, bv.reshape(bsz, t, B_HEADS, B_V_DIM), ret_state[e])
            new_ret.append(s_b.astype(x.dtype))
            o_b = head_rms(o_b) * jax.nn.silu(bg.astype(F32)).reshape(bsz, t, B_HEADS, B_V_DIM)
            mixed = jnp.concatenate([o_a.reshape(bsz, t, A_V), o_b.reshape(bsz, t, B_V)], axis=-1)
            x = x + mixed.astype(x.dtype) @ p["w_out_even"][e]
        else:
            oi = l // 2
            z = h @ p["w_in_odd"][oi]
            cq, ck, cv, cr, ca = jnp.split(z, ODD_SPLITS, axis=-1)
            log_a = jax.nn.log_sigmoid((ca @ p["w_gate_lr"][oi] + p["b_gate"][oi]).astype(F32)) / C_GATE_TAU
            o_c, s_c = gla(cq.reshape(bsz, t, C_HEADS, C_QK_DIM).astype(F32) * (C_QK_DIM ** -0.5),
                           ck.reshape(bsz, t, C_HEADS, C_QK_DIM),
                           cv.reshape(bsz, t, C_HEADS, C_V_DIM),
                           log_a.reshape(bsz, t, C_HEADS, C_QK_DIM), gla_state[oi])
            new_gla.append(s_c.astype(x.dtype))
            o_c = (head_rms(o_c) * p["gla_norm"][oi].astype(F32)
                   * jax.nn.silu(cr.astype(F32)).reshape(bsz, t, C_HEADS, C_V_DIM))
            x = x + o_c.reshape(bsz, t, C_V).astype(x.dtype) @ p["w_out_odd"][oi]
        h = rmsnorm(x, p["ln_cross"][l])
        x = x + mem_attend(h, mem_k[l], mem_v[l], p["w_cq"][l], p["w_co"][l])
        h = rmsnorm(x, p["ln_ffn"][l])
        f, tail = conv_ffn(h, conv_prev[l], p["w_ffn_gate"][l], p["w_ffn_up"][l],
                           p["ffn_conv_w"][l], p["ffn_conv_b"][l], p["w_ffn_down"][l])
        new_conv.append(tail)
        x = x + f
    y = rmsnorm(x, p["ln_final"])
    return y, jnp.stack(new_k), jnp.stack(new_v), jnp.stack(new_ret), jnp.stack(new_gla), jnp.stack(new_conv)


def setup_inputs(seed: int = 0) -> dict:
    key = jax.random.key(seed)
    ks = iter(jax.random.split(key, 48))

    def nrm(shape, scale):
        return jax.random.normal(next(ks), shape, F32) * scale

    d = D_MODEL
    return {
        "x_prompt": nrm((BATCH, SEQ, d), 1.0),
        "x_sample": nrm((DEC_BATCH, DEC_SEQ, d), 1.0),
        "cache_diff_k": nrm((N_EVEN, DEC_BATCH, PAST_LEN, A_HEADS, A_V_DIM), 1.0),
        "cache_diff_v": nrm((N_EVEN, DEC_BATCH, PAST_LEN, A_HEADS, A_V_DIM), 1.0),
        "state_retention": nrm((N_EVEN, DEC_BATCH, B_HEADS, B_QK_DIM, B_V_DIM), 0.3),
        "state_gla": nrm((N_ODD, DEC_BATCH, C_HEADS, C_QK_DIM, C_V_DIM), 0.3),
        "cache_ffn_conv": nrm((DEPTH, DEC_BATCH, CONV_W - 1, D_FF), 1.0),
        "cache_mem_k": nrm((DEPTH, DEC_BATCH, MEM_LEN, M_HEADS, M_HEAD_DIM), 1.0),
        "cache_mem_v": nrm((DEPTH, DEC_BATCH, MEM_LEN, M_HEADS, M_HEAD_DIM), 1.0),
        "mem_prompt": nrm((BATCH, MEM_LEN, d), 1.0),
        "ln_mix": 1.0 + nrm((DEPTH, d), 0.05),
        "ln_cross": 1.0 + nrm((DEPTH, d), 0.05),
        "ln_ffn": 1.0 + nrm((DEPTH, d), 0.05),
        "ln_mem": 1.0 + nrm((DEPTH, d), 0.05),
        "ln_final": 1.0 + nrm((d,), 0.05),
        "w_in_even": nrm((N_EVEN, d, EVEN_IN), d ** -0.5),
        "w_out_even": nrm((N_EVEN, EVEN_MIX, d), EVEN_MIX ** -0.5),
        "diff_lq1": nrm((N_EVEN, A_QK_DIM), 0.1),
        "diff_lk1": nrm((N_EVEN, A_QK_DIM), 0.1),
        "diff_lq2": nrm((N_EVEN, A_QK_DIM), 0.1),
        "diff_lk2": nrm((N_EVEN, A_QK_DIM), 0.1),
        "diff_subln": 1.0 + nrm((N_EVEN, A_V_DIM), 0.05),
        "rel_bias": nrm((REL_BUCKETS, A_HEADS), 0.5),
        "w_in_odd": nrm((N_ODD, d, ODD_IN), d ** -0.5),
        "w_gate_lr": nrm((N_ODD, C_GATE_RANK, C_QK), C_GATE_RANK ** -0.5),
        "b_gate": nrm((N_ODD, C_QK), 0.1),
        "gla_norm": 1.0 + nrm((N_ODD, C_V_DIM), 0.05),
        "w_out_odd": nrm((N_ODD, C_V, d), C_V ** -0.5),
        "w_cq": nrm((DEPTH, d, M_WIDTH), d ** -0.5),
        "w_ck": nrm((DEPTH, d, M_WIDTH), d ** -0.5),
        "w_cv": nrm((DEPTH, d, M_WIDTH), d ** -0.5),
        "w_co": nrm((DEPTH, M_WIDTH, d), M_WIDTH ** -0.5),
        "w_ffn_gate": nrm((DEPTH, d, D_FF), d ** -0.5),
        "w_ffn_up": nrm((DEPTH, d, D_FF), d ** -0.5),
        "ffn_conv_w": nrm((DEPTH, CONV_W, D_FF), CONV_W ** -0.5),
        "ffn_conv_b": nrm((DEPTH, D_FF), 0.02),
        "w_ffn_down": nrm((DEPTH, D_FF, d), D_FF ** -0.5),
    }


def reference(x_prompt, x_sample, cache_diff_k, cache_diff_v, state_retention, state_gla,
              cache_ffn_conv, cache_mem_k, cache_mem_v, mem_prompt,
              ln_mix, ln_cross, ln_ffn, ln_mem, ln_final,
              w_in_even, w_out_even, diff_lq1, diff_lk1, diff_lq2, diff_lk2, diff_subln, rel_bias,
              w_in_odd, w_gate_lr, b_gate, gla_norm, w_out_odd,
              w_cq, w_ck, w_cv, w_co,
              w_ffn_gate, w_ffn_up, ffn_conv_w, ffn_conv_b, w_ffn_down):
    p = dict(ln_mix=ln_mix, ln_cross=ln_cross, ln_ffn=ln_ffn, ln_final=ln_final,
             w_in_even=w_in_even, w_out_even=w_out_even,
             diff_lq1=diff_lq1, diff_lk1=diff_lk1, diff_lq2=diff_lq2, diff_lk2=diff_lk2,
             diff_subln=diff_subln, rel_bias=rel_bias,
             w_in_odd=w_in_odd, w_gate_lr=w_gate_lr, b_gate=b_gate, gla_norm=gla_norm,
             w_out_odd=w_out_odd, w_cq=w_cq, w_co=w_co,
             w_ffn_gate=w_ffn_gate, w_ffn_up=w_ffn_up, ffn_conv_w=ffn_conv_w,
             ffn_conv_b=ffn_conv_b, w_ffn_down=w_ffn_down)

    bp = x_prompt.shape[0]
    dt = x_prompt.dtype
    mem_k_p, mem_v_p = memory_kv(mem_prompt, ln_mem, w_ck, w_cv)
    zero_k = jnp.zeros((N_EVEN, bp, 0, A_HEADS, A_V_DIM), dt)
    zero_ret = jnp.zeros((N_EVEN, bp, B_HEADS, B_QK_DIM, B_V_DIM), dt)
    zero_gla = jnp.zeros((N_ODD, bp, C_HEADS, C_QK_DIM, C_V_DIM), dt)
    zero_conv = jnp.zeros((DEPTH, bp, CONV_W - 1, D_FF), dt)
    y_prompt, dk_p, dv_p, ret_p, gla_p, conv_p = forward(
        x_prompt, zero_k, zero_k, zero_ret, zero_gla, zero_conv, mem_k_p, mem_v_p, p)

    y_sample, dk_s, dv_s, ret_s, gla_s, conv_s = forward(
        x_sample, cache_diff_k, cache_diff_v, state_retention, state_gla, cache_ffn_conv,
        cache_mem_k, cache_mem_v, p)

    return (y_prompt, y_sample, dk_p, dv_p, ret_p, gla_p, conv_p, mem_k_p, mem_v_p,
            dk_s, dv_s, ret_s, gla_s, conv_s)
```

```python
import functools
import math

import numpy as np
import jax
import jax.numpy as jnp
from jax import lax
from jax.experimental import pallas as pl
from jax.experimental.pallas import tpu as pltpu

F32 = jnp.float32
BF16 = jnp.bfloat16

D_MODEL = 1024
CHUNK = 64
A_HEADS = 4
A_QK_DIM = 64
A_V_DIM = 128
B_HEADS = 4
B_QK_DIM = 128
C_HEADS = 4
C_QK_DIM = 128
C_V_DIM = 256
C_GATE_RANK = 16
C_GATE_TAU = 16.0
M_HEADS = 4
M_HEAD_DIM = 256
REL_BUCKETS = 32
REL_MAX_DIST = 128
D_FF = 2816
ROPE_BASE = 10000.0
EPS = 1e-6
NEG_INF = -1e30

EVEN_IN = 3584
ODD_Z = 3072
LANES = 128
SUBLANES = 8
ODD_IN_PAD = ODD_Z + LANES
FF_CHUNK = 256
V7X_VMEM_LIMIT_BYTES = 56 * 1024 * 1024


def _params(n_axes):
    return pltpu.CompilerParams(dimension_semantics=("arbitrary",) * n_axes,
                                vmem_limit_bytes=V7X_VMEM_LIMIT_BYTES)


def _dot(a, b):
    return jnp.dot(a, b, preferred_element_type=F32)


def _dot_nt(a, b):
    return lax.dot_general(a, b, (((1,), (1,)), ((), ())), preferred_element_type=F32)


def _dot_tn(a, b):
    return lax.dot_general(a, b, (((0,), (0,)), ((), ())), preferred_element_type=F32)


def _rms(x, g):
    return x * lax.rsqrt(jnp.mean(x * x, axis=-1, keepdims=True) + EPS) * g


def _head_rms(x):
    return x * lax.rsqrt(jnp.mean(x * x, axis=-1, keepdims=True) + EPS)


def _silu(x):
    return x * (1.0 / (1.0 + jnp.exp(-x)))


def _gelu_tanh(x):
    return 0.5 * x * (1.0 + jnp.tanh(math.sqrt(2.0 / math.pi) * (x + 0.044715 * (x * x * x))))


def _log_sigmoid(x):
    return jnp.minimum(x, 0.0) - jnp.log1p(jnp.exp(-jnp.abs(x)))


def _in_even_kernel(x_ref, g_ref, w_ref, z_ref, k_ref, v_ref):
    h = _rms(x_ref[...], g_ref[...]).astype(BF16)
    width = 4 * A_V_DIM
    for c in range(EVEN_IN // width):
        zc = _dot(h, w_ref[:, c * width:(c + 1) * width])
        z_ref[:, c * width:(c + 1) * width] = zc.astype(BF16)
        if c == 1:
            k_ref[...] = zc
        if c == 2:
            v_ref[...] = zc


def _in_even(x2d, gamma, w, tm):
    n = x2d.shape[0]
    width = 4 * A_V_DIM
    return pl.pallas_call(
        _in_even_kernel,
        grid=(n // tm,),
        in_specs=[pl.BlockSpec((tm, D_MODEL), lambda i: (i, 0)),
                  pl.BlockSpec((1, D_MODEL), lambda i: (0, 0)),
                  pl.BlockSpec((D_MODEL, EVEN_IN), lambda i: (0, 0))],
        out_specs=[pl.BlockSpec((tm, EVEN_IN), lambda i: (i, 0)),
                   pl.BlockSpec((tm, width), lambda i: (i, 0)),
                   pl.BlockSpec((tm, width), lambda i: (i, 0))],
        out_shape=[jax.ShapeDtypeStruct((n, EVEN_IN), BF16),
                   jax.ShapeDtypeStruct((n, width), F32),
                   jax.ShapeDtypeStruct((n, width), F32)],
        compiler_params=_params(1),
    )(x2d, gamma.reshape(1, D_MODEL), w)


def _in_odd_kernel(x_ref, g_ref, w_ref, wlr_ref, bg_ref, z_ref, la_ref):
    h = _rms(x_ref[...], g_ref[...]).astype(BF16)
    width = 512
    for c in range(ODD_Z // width):
        z_ref[:, c * width:(c + 1) * width] = _dot(h, w_ref[:, c * width:(c + 1) * width]).astype(BF16)
    ca = _dot(h, w_ref[:, ODD_Z:ODD_IN_PAD]).astype(BF16)
    pre = _dot(ca, wlr_ref[...]) + bg_ref[...]
    la_ref[...] = _log_sigmoid(pre) / C_GATE_TAU


def _in_odd(x2d, gamma, w_pad, wlr_pad, b_gate, tm):
    n = x2d.shape[0]
    qk = C_HEADS * C_QK_DIM
    return pl.pallas_call(
        _in_odd_kernel,
        grid=(n // tm,),
        in_specs=[pl.BlockSpec((tm, D_MODEL), lambda i: (i, 0)),
                  pl.BlockSpec((1, D_MODEL), lambda i: (0, 0)),
                  pl.BlockSpec((D_MODEL, ODD_IN_PAD), lambda i: (0, 0)),
                  pl.BlockSpec((LANES, qk), lambda i: (0, 0)),
                  pl.BlockSpec((1, qk), lambda i: (0, 0))],
        out_specs=[pl.BlockSpec((tm, ODD_Z), lambda i: (i, 0)),
                   pl.BlockSpec((tm, qk), lambda i: (i, 0))],
        out_shape=[jax.ShapeDtypeStruct((n, ODD_Z), BF16),
                   jax.ShapeDtypeStruct((n, qk), F32)],
        compiler_params=_params(1),
    )(x2d, gamma.reshape(1, D_MODEL), w_pad, wlr_pad, b_gate.reshape(1, qk))


def _norm_mm_kernel(x_ref, g_ref, w_ref, o_ref):
    h = _rms(x_ref[...], g_ref[...]).astype(BF16)
    o_ref[...] = _dot(h, w_ref[...])


def _norm_mm(x2d, gamma, w, tm):
    n, nout = x2d.shape[0], w.shape[1]
    return pl.pallas_call(
        _norm_mm_kernel,
        grid=(n // tm,),
        in_specs=[pl.BlockSpec((tm, D_MODEL), lambda i: (i, 0)),
                  pl.BlockSpec((1, D_MODEL), lambda i: (0, 0)),
                  pl.BlockSpec((D_MODEL, nout), lambda i: (0, 0))],
        out_specs=pl.BlockSpec((tm, nout), lambda i: (i, 0)),
        out_shape=jax.ShapeDtypeStruct((n, nout), F32),
        compiler_params=_params(1),
    )(x2d, gamma.reshape(1, D_MODEL), w)


def _mm_res_kernel(*refs, n_in, final_norm):
    res_ref = refs[0]
    a_refs = refs[1:1 + n_in]
    w_refs = refs[1 + n_in:1 + 2 * n_in]
    rest = refs[1 + 2 * n_in:]
    acc = res_ref[...]
    for a_ref, w_ref in zip(a_refs, w_refs):
        acc = acc + _dot(a_ref[...], w_ref[...])
    if final_norm:
        gf_ref, o_ref, y_ref = rest
        o_ref[...] = acc
        y_ref[...] = _rms(acc, gf_ref[...])
    else:
        (o_ref,) = rest
        o_ref[...] = acc


def _mm_res(res, a_list, w_list, tm, final_gamma=None):
    n = res.shape[0]
    n_in = len(a_list)
    in_specs = [pl.BlockSpec((tm, D_MODEL), lambda i: (i, 0))]
    in_specs += [pl.BlockSpec((tm, a.shape[1]), lambda i: (i, 0)) for a in a_list]
    in_specs += [pl.BlockSpec(w.shape, lambda i: (0, 0)) for w in w_list]
    args = [res] + list(a_list) + list(w_list)
    out_spec = pl.BlockSpec((tm, D_MODEL), lambda i: (i, 0))
    out_sds = jax.ShapeDtypeStruct((n, D_MODEL), F32)
    if final_gamma is not None:
        in_specs.append(pl.BlockSpec((1, D_MODEL), lambda i: (0, 0)))
        args.append(final_gamma.reshape(1, D_MODEL))
        out_specs, out_shape = [out_spec, out_spec], [out_sds, out_sds]
    else:
        out_specs, out_shape = out_spec, out_sds
    return pl.pallas_call(
        functools.partial(_mm_res_kernel, n_in=n_in, final_norm=final_gamma is not None),
        grid=(n // tm,),
        in_specs=in_specs,
        out_specs=out_specs,
        out_shape=out_shape,
        compiler_params=_params(1),
    )(*args)


def _diff_attn_kernel(lam_ref, q_ref, k_ref, v_ref, bias_ref, subln_ref, o_ref,
                      m_sc, l_sc, acc_sc, *, tq, nd, out_scale):
    i = pl.program_id(2)
    q = q_ref[0] * (A_QK_DIM ** -0.5)
    lane = lax.broadcasted_iota(jnp.int32, (tq, A_V_DIM), 1)
    zero = jnp.zeros_like(q)
    qp = jnp.concatenate([jnp.where(lane < A_QK_DIM, q, zero),
                          jnp.where(lane >= A_QK_DIM, q, zero)], axis=0)
    m_sc[...] = jnp.full(m_sc.shape, NEG_INF, F32)
    l_sc[...] = jnp.zeros(l_sc.shape, F32)
    acc_sc[...] = jnp.zeros(acc_sc.shape, F32)

    def step(j, masked):
        start = pl.multiple_of(j * tq, tq)
        ks = k_ref[0, pl.ds(start, tq), :]
        vs = v_ref[0, pl.ds(start, tq), :]
        b = bias_ref[0, j - i + (nd - 1)]
        s = _dot_nt(qp, ks) + jnp.concatenate([b, b], axis=0)
        if masked:
            r = lax.broadcasted_iota(jnp.int32, (2 * tq, tq), 0) % tq
            c = lax.broadcasted_iota(jnp.int32, (2 * tq, tq), 1)
            s = jnp.where((c // CHUNK) <= (r // CHUNK), s, NEG_INF)
        m_old = m_sc[...]
        m_new = jnp.maximum(m_old, jnp.max(s, axis=-1, keepdims=True))
        p = jnp.exp(s - m_new)
        alpha = jnp.exp(m_old - m_new)
        l_sc[...] = alpha * l_sc[...] + jnp.sum(p, axis=-1, keepdims=True)
        acc_sc[...] = alpha * acc_sc[...] + _dot(p.astype(BF16), vs)
        m_sc[...] = m_new

    def body(j, carry):
        step(j, False)
        return carry

    lax.fori_loop(0, i, body, 0)
    step(i, True)
    out = acc_sc[...] / l_sc[...]
    o = out[:tq] - lam_ref[0, 0] * out[tq:]
    o_ref[...] = (_head_rms(o) * subln_ref[...] * out_scale).astype(BF16)


def _diff_attn(lam, z3, bias_tiles, subln, tq, out_scale):
    bsz, t, _ = z3.shape
    nd = t // tq
    return pl.pallas_call(
        functools.partial(_diff_attn_kernel, tq=tq, nd=nd, out_scale=out_scale),
        grid=(bsz, A_HEADS, nd),
        in_specs=[pl.BlockSpec(memory_space=pltpu.SMEM),
                  pl.BlockSpec((1, tq, A_V_DIM), lambda b, h, i: (b, i, h)),
                  pl.BlockSpec((1, t, A_V_DIM), lambda b, h, i: (b, 0, A_HEADS + h)),
                  pl.BlockSpec((1, t, A_V_DIM), lambda b, h, i: (b, 0, 2 * A_HEADS + h)),
                  pl.BlockSpec((1, nd, tq, tq), lambda b, h, i: (h, 0, 0, 0)),
                  pl.BlockSpec((1, A_V_DIM), lambda b, h, i: (0, 0))],
        out_specs=pl.BlockSpec((tq, A_V_DIM), lambda b, h, i: (b * nd + i, h)),
        out_shape=jax.ShapeDtypeStruct((bsz * t, A_HEADS * A_V_DIM), BF16),
        scratch_shapes=[pltpu.VMEM((2 * tq, 1), F32), pltpu.VMEM((2 * tq, 1), F32),
                        pltpu.VMEM((2 * tq, A_V_DIM), F32)],
        compiler_params=_params(3),
    )(lam, z3, z3, z3, bias_tiles, subln.reshape(1, A_V_DIM))


def _diff_attn_cached_kernel(lam_ref, q_ref, kn_ref, vn_ref, kp_ref, vp_ref, bp_ref, bn_ref,
                             subln_ref, o_ref, *, t, past, out_scale):
    lane = lax.broadcasted_iota(jnp.int32, (t, A_V_DIM), 1)
    qpos_p = past + lax.broadcasted_iota(jnp.int32, (2 * t, past), 0) % t
    kpos_p = lax.broadcasted_iota(jnp.int32, (2 * t, past), 1)
    ok_p = (kpos_p // CHUNK) <= (qpos_p // CHUNK)
    qpos_n = past + lax.broadcasted_iota(jnp.int32, (2 * t, t), 0) % t
    kpos_n = past + lax.broadcasted_iota(jnp.int32, (2 * t, t), 1)
    ok_n = (kpos_n // CHUNK) <= (qpos_n // CHUNK)
    for h in range(A_HEADS):
        cs = slice(h * A_V_DIM, (h + 1) * A_V_DIM)
        q = q_ref[0, :, cs] * (A_QK_DIM ** -0.5)
        zero = jnp.zeros_like(q)
        qp = jnp.concatenate([jnp.where(lane < A_QK_DIM, q, zero),
                              jnp.where(lane >= A_QK_DIM, q, zero)], axis=0)
        kp = kp_ref[0, :, cs].astype(BF16)
        vp = vp_ref[0, :, cs].astype(BF16)
        kn = kn_ref[0, :, cs]
        vn = vn_ref[0, :, cs]
        bp = bp_ref[h]
        bn = bn_ref[h]
        sp = jnp.where(ok_p, _dot_nt(qp, kp) + jnp.concatenate([bp, bp], axis=0), NEG_INF)
        sn = jnp.where(ok_n, _dot_nt(qp, kn) + jnp.concatenate([bn, bn], axis=0), NEG_INF)
        m = jnp.maximum(jnp.max(sp, axis=-1, keepdims=True), jnp.max(sn, axis=-1, keepdims=True))
        pp = jnp.exp(sp - m)
        pn = jnp.exp(sn - m)
        l = jnp.sum(pp, axis=-1, keepdims=True) + jnp.sum(pn, axis=-1, keepdims=True)
        out = (_dot(pp.astype(BF16), vp) + _dot(pn.astype(BF16), vn)) / l
        o = out[:t] - lam_ref[0, 0] * out[t:]
        o_ref[:, cs] = (_head_rms(o) * subln_ref[...] * out_scale).astype(BF16)


def _diff_attn_cached(lam, z3, past_k, past_v, bias_past, bias_new, subln, out_scale):
    bsz, t, _ = z3.shape
    past = past_k.shape[1]
    width = A_HEADS * A_V_DIM
    return pl.pallas_call(
        functools.partial(_diff_attn_cached_kernel, t=t, past=past, out_scale=out_scale),
        grid=(bsz,),
        in_specs=[pl.BlockSpec(memory_space=pltpu.SMEM),
                  pl.BlockSpec((1, t, width), lambda b: (b, 0, 0)),
                  pl.BlockSpec((1, t, width), lambda b: (b, 0, 1)),
                  pl.BlockSpec((1, t, width), lambda b: (b, 0, 2)),
                  pl.BlockSpec((1, past, width), lambda b: (b, 0, 0)),
                  pl.BlockSpec((1, past, width), lambda b: (b, 0, 0)),
                  pl.BlockSpec((A_HEADS, t, past), lambda b: (0, 0, 0)),
                  pl.BlockSpec((A_HEADS, t, t), lambda b: (0, 0, 0)),
                  pl.BlockSpec((1, A_V_DIM), lambda b: (0, 0))],
        out_specs=pl.BlockSpec((t, width), lambda b: (b, 0)),
        out_shape=jax.ShapeDtypeStruct((bsz * t, width), BF16),
        compiler_params=_params(1),
    )(lam, z3, z3, z3, past_k, past_v, bias_past, bias_new, subln.reshape(1, A_V_DIM))


def _retention_kernel(q_ref, k_ref, v_ref, gt_ref, cos_ref, sin_ref, dec_ref, qd_ref, kd_ref,
                      cd_ref, s0_ref, o_ref, s_out_ref, s_sc):
    t = pl.program_id(2)

    @pl.when(t == 0)
    def _():
        s_sc[...] = s0_ref[0, 0]

    half = B_QK_DIM // 2
    q = q_ref[0].astype(F32)
    k = k_ref[0].astype(F32)
    cos = cos_ref[...]
    sin = sin_ref[...]
    qr = (q * cos + pltpu.roll(q, half, 1) * sin) * (B_QK_DIM ** -0.5)
    kr = k * cos + pltpu.roll(k, half, 1) * sin
    v = v_ref[0]
    att = _dot_nt(qr.astype(BF16), kr.astype(BF16)) * dec_ref[0]
    s = s_sc[...]
    o = _dot(att.astype(BF16), v) + _dot((qr * qd_ref[0]).astype(BF16), s.astype(BF16))
    s_sc[...] = s * cd_ref[0] + _dot_tn((kr * kd_ref[0]).astype(BF16), v)
    gt = gt_ref[0].astype(F32)
    o_ref[...] = (_head_rms(o) * _silu(gt)).astype(BF16)

    @pl.when(t == pl.num_programs(2) - 1)
    def _():
        s_out_ref[0, 0] = s_sc[...]


def _retention(z3, cos, sin, consts, s0, c):
    bsz, t, _ = z3.shape
    nt = t // c
    dec, qd, kd, cd = consts
    d = B_QK_DIM
    base = 3 * A_HEADS
    return pl.pallas_call(
        _retention_kernel,
        grid=(bsz, B_HEADS, nt),
        in_specs=[pl.BlockSpec((1, c, d), lambda b, h, i: (b, i, base + h)),
                  pl.BlockSpec((1, c, d), lambda b, h, i: (b, i, base + B_HEADS + h)),
                  pl.BlockSpec((1, c, d), lambda b, h, i: (b, i, base + 2 * B_HEADS + h)),
                  pl.BlockSpec((1, c, d), lambda b, h, i: (b, i, base + 3 * B_HEADS + h)),
                  pl.BlockSpec((c, d), lambda b, h, i: (i, 0)),
                  pl.BlockSpec((c, d), lambda b, h, i: (i, 0)),
                  pl.BlockSpec((1, c, c), lambda b, h, i: (h, 0, 0)),
                  pl.BlockSpec((1, c, d), lambda b, h, i: (h, 0, 0)),
                  pl.BlockSpec((1, c, d), lambda b, h, i: (h, 0, 0)),
                  pl.BlockSpec((1, 1, d), lambda b, h, i: (h, 0, 0)),
                  pl.BlockSpec((1, 1, d, d), lambda b, h, i: (b, h, 0, 0))],
        out_specs=[pl.BlockSpec((c, d), lambda b, h, i: (b * nt + i, h)),
                   pl.BlockSpec((1, 1, d, d), lambda b, h, i: (b, h, 0, 0))],
        out_shape=[jax.ShapeDtypeStruct((bsz * t, B_HEADS * d), BF16),
                   jax.ShapeDtypeStruct((bsz, B_HEADS, d, d), F32)],
        scratch_shapes=[pltpu.VMEM((d, d), F32)],
        compiler_params=_params(3),
    )(z3, z3, z3, z3, cos, sin, dec, qd, kd, cd, s0)


def _retention_consts(c):
    log_g = jnp.log1p(-jnp.exp2(-5.0 - jnp.arange(B_HEADS, dtype=F32)))
    idx = jnp.arange(c, dtype=F32)
    dist = idx[:, None] - idx[None, :]
    dec = jnp.where(dist >= 0, jnp.exp(jnp.maximum(dist, 0.0)[None] * log_g[:, None, None]), 0.0)
    qd = jnp.exp((idx[None, :] + 1.0) * log_g[:, None])
    kd = jnp.exp((c - 1.0 - idx)[None, :] * log_g[:, None])
    cd = jnp.exp(c * log_g)
    bc = lambda a: jnp.broadcast_to(a[..., None], a.shape + (B_QK_DIM,))
    return dec, bc(qd), bc(kd), bc(cd[:, None])


def _rotary_tables(pos):
    half = B_QK_DIM // 2
    inv = ROPE_BASE ** (-jnp.arange(half, dtype=F32) / half)
    ang = pos.astype(F32)[:, None] * inv[None, :]
    cos, sin = jnp.cos(ang), jnp.sin(ang)
    return jnp.concatenate([cos, cos], axis=-1), jnp.concatenate([-sin, sin], axis=-1)


def _gla_levels(c):
    return [c >> (l + 1) for l in range(int(math.log2(c)))]


def _gla_consts(c):
    levels = _gla_levels(c)
    rows = np.arange(c)
    mats = []
    for s in levels:
        ref = (rows // (2 * s)) * 2 * s + s - 1
        a = np.zeros((c, c), np.float32)
        for i in range(c):
            if i & s:
                a[i, ref[i] + 1:i + 1] = 1.0
            else:
                a[i, i + 1:ref[i] + 1] = 1.0
        mats.append(a)
    mats.append(np.tril(np.ones((c, c), np.float32)))
    mats.append(np.triu(np.ones((c, c), np.float32), 1))
    lv = np.full((c, c), -1, np.int32)
    for i in range(c):
        lv[i, i] = len(levels)
        for j in range(i):
            lv[i, j] = levels.index(1 << int(math.floor(math.log2(i ^ j))))
    return jnp.asarray(np.concatenate(mats, axis=0), BF16), jnp.asarray(lv)


def _gla_kernel(q_ref, k_ref, v_ref, r_ref, g_ref, s0_ref, a_ref, lv_ref, nw_ref,
                o_ref, s_out_ref, st_sc, *, c, n_chunks):
    t = pl.program_id(2)

    @pl.when(t == 0)
    def _():
        st_sc[...] = s0_ref[0, 0].T

    levels = _gla_levels(c)
    n_lv = len(levels)
    a = a_ref[...]
    lv = lv_ref[...]
    row = lax.broadcasted_iota(jnp.int32, (c, C_QK_DIM), 0)
    for ci in range(n_chunks):
        rs = slice(ci * c, (ci + 1) * c)
        q = q_ref[0, rs, :].astype(F32) * (C_QK_DIM ** -0.5)
        k = k_ref[0, rs, :].astype(F32)
        v = v_ref[0, rs, :]
        g = g_ref[0, rs, :]
        g_hi = g.astype(BF16)
        g_lo = (g - g_hi.astype(F32)).astype(BF16)
        x2 = _dot(a, jnp.concatenate([g_hi, g_lo], axis=1))
        x = x2[:, :C_QK_DIM] + x2[:, C_QK_DIM:]
        att = jnp.zeros((c, c), F32)
        for l, s in enumerate(levels):
            e = jnp.exp(x[l * c:(l + 1) * c])
            up = (row & s) != 0
            mix = jnp.where(up, q, k) * e
            qt = jnp.where(up, mix, 0.0).astype(BF16)
            kt = jnp.where(up, 0.0, mix).astype(BF16)
            att = jnp.where(lv == l, _dot_nt(qt, kt), att)
        att = jnp.where(lv == n_lv, _dot_nt(q.astype(BF16), k.astype(BF16)), att)
        b = x[n_lv * c:(n_lv + 1) * c]
        rem = x[(n_lv + 1) * c:(n_lv + 2) * c]
        st = st_sc[...]
        o = _dot(att.astype(BF16), v) + _dot_nt((q * jnp.exp(b)).astype(BF16), st.astype(BF16))
        kd = (k * jnp.exp(rem)).astype(BF16)
        st_sc[...] = st * jnp.exp(b[c - 1:c, :]) + _dot_tn(v, kd)
        r = r_ref[0, rs, :].astype(F32)
        o_ref[rs, :] = (_head_rms(o) * nw_ref[...] * _silu(r)).astype(BF16)

    @pl.when(t == pl.num_programs(2) - 1)
    def _():
        s_out_ref[0, 0] = st_sc[...].T


def _gla(z3, log_a3, s0, a_stack, lv, norm_w, c, tb):
    bsz, t, _ = z3.shape
    nt = t // tb
    dk, dv = C_QK_DIM, C_V_DIM
    return pl.pallas_call(
        functools.partial(_gla_kernel, c=c, n_chunks=tb // c),
        grid=(bsz, C_HEADS, nt),
        in_specs=[pl.BlockSpec((1, tb, dk), lambda b, h, i: (b, i, h)),
                  pl.BlockSpec((1, tb, dk), lambda b, h, i: (b, i, C_HEADS + h)),
                  pl.BlockSpec((1, tb, dv), lambda b, h, i: (b, i, C_HEADS + h)),
                  pl.BlockSpec((1, tb, dv), lambda b, h, i: (b, i, 2 * C_HEADS + h)),
                  pl.BlockSpec((1, tb, dk), lambda b, h, i: (b, i, h)),
                  pl.BlockSpec((1, 1, dk, dv), lambda b, h, i: (b, h, 0, 0)),
                  pl.BlockSpec(a_stack.shape, lambda b, h, i: (0, 0)),
                  pl.BlockSpec((c, c), lambda b, h, i: (0, 0)),
                  pl.BlockSpec((1, dv), lambda b, h, i: (0, 0))],
        out_specs=[pl.BlockSpec((tb, dv), lambda b, h, i: (b * nt + i, h)),
                   pl.BlockSpec((1, 1, dk, dv), lambda b, h, i: (b, h, 0, 0))],
        out_shape=[jax.ShapeDtypeStruct((bsz * t, C_HEADS * dv), BF16),
                   jax.ShapeDtypeStruct((bsz, C_HEADS, dk, dv), F32)],
        scratch_shapes=[pltpu.VMEM((dv, dk), F32)],
        compiler_params=_params(3),
    )(z3, z3, z3, z3, log_a3, s0, a_stack, lv, norm_w.reshape(1, dv))


def _cross_kernel(x_ref, g_ref, wq_ref, mk_ref, mv_ref, wo_ref, o_ref):
    x = x_ref[...]
    h = _rms(x, g_ref[...]).astype(BF16)
    q = (_dot(h, wq_ref[...]) * (M_HEAD_DIM ** -0.5)).astype(BF16)
    mk = mk_ref[0].astype(BF16)
    mv = mv_ref[0].astype(BF16)
    outs = []
    for hd in range(M_HEADS):
        cs = slice(hd * M_HEAD_DIM, (hd + 1) * M_HEAD_DIM)
        s = _dot_nt(q[:, cs], mk[:, cs])
        p = jnp.exp(s - jnp.max(s, axis=-1, keepdims=True))
        l = jnp.sum(p, axis=-1, keepdims=True)
        outs.append((_dot(p.astype(BF16), mv[:, cs]) / l).astype(BF16))
    o_ref[...] = x + _dot(jnp.concatenate(outs, axis=1), wo_ref[...])


def _cross(x2d, gamma, wq, mk3, mv3, wo, t, tm):
    n = x2d.shape[0]
    nt = t // tm
    mem_len, width = mk3.shape[1], mk3.shape[2]
    return pl.pallas_call(
        _cross_kernel,
        grid=(n // t, nt),
        in_specs=[pl.BlockSpec((tm, D_MODEL), lambda b, i: (b * nt + i, 0)),
                  pl.BlockSpec((1, D_MODEL), lambda b, i: (0, 0)),
                  pl.BlockSpec((D_MODEL, width), lambda b, i: (0, 0)),
                  pl.BlockSpec((1, mem_len, width), lambda b, i: (b, 0, 0)),
                  pl.BlockSpec((1, mem_len, width), lambda b, i: (b, 0, 0)),
                  pl.BlockSpec((width, D_MODEL), lambda b, i: (0, 0))],
        out_specs=pl.BlockSpec((tm, D_MODEL), lambda b, i: (b * nt + i, 0)),
        out_shape=jax.ShapeDtypeStruct((n, D_MODEL), F32),
        compiler_params=_params(2),
    )(x2d, gamma.reshape(1, D_MODEL), wq, mk3, mv3, wo)


def _ffn_act_kernel(x_ref, g_ref, wg_ref, wu_ref, cw_ref, cb_ref, prev_ref, act_ref, tail_ref,
                    gbuf, *, tm):
    t = pl.program_id(1)
    lo = SUBLANES - 2

    @pl.when(t == 0)
    def _():
        gbuf[lo:SUBLANES, :] = prev_ref[0]

    h = _rms(x_ref[...], g_ref[...]).astype(BF16)
    for c in range(D_FF // FF_CHUNK):
        cs = slice(c * FF_CHUNK, (c + 1) * FF_CHUNK)
        gate = _dot(h, wg_ref[:, cs])
        gbuf[SUBLANES:SUBLANES + tm, cs] = gate
        conv = cb_ref[:, cs] + cw_ref[0:1, cs] * gbuf[lo:lo + tm, cs]
        conv = conv + cw_ref[1:2, cs] * gbuf[lo + 1:lo + 1 + tm, cs]
        conv = conv + cw_ref[2:3, cs] * gate
        up = _dot(h, wu_ref[:, cs])
        act_ref[:, cs] = (_gelu_tanh(conv) * up).astype(BF16)
    tail = gbuf[tm + lo:tm + SUBLANES, :]
    gbuf[lo:SUBLANES, :] = tail

    @pl.when(t == pl.num_programs(1) - 1)
    def _():
        tail_ref[0] = tail


def _ffn_act(x2d, gamma, wg, wu, conv_w, conv_b, prev, t, tm):
    n = x2d.shape[0]
    nt = t // tm
    bsz = n // t
    return pl.pallas_call(
        functools.partial(_ffn_act_kernel, tm=tm),
        grid=(bsz, nt),
        in_specs=[pl.BlockSpec((tm, D_MODEL), lambda b, i: (b * nt + i, 0)),
                  pl.BlockSpec((1, D_MODEL), lambda b, i: (0, 0)),
                  pl.BlockSpec((D_MODEL, D_FF), lambda b, i: (0, 0)),
                  pl.BlockSpec((D_MODEL, D_FF), lambda b, i: (0, 0)),
                  pl.BlockSpec((3, D_FF), lambda b, i: (0, 0)),
                  pl.BlockSpec((1, D_FF), lambda b, i: (0, 0)),
                  pl.BlockSpec((1, 2, D_FF), lambda b, i: (b, 0, 0))],
        out_specs=[pl.BlockSpec((tm, D_FF), lambda b, i: (b * nt + i, 0)),
                   pl.BlockSpec((1, 2, D_FF), lambda b, i: (b, 0, 0))],
        out_shape=[jax.ShapeDtypeStruct((n, D_FF), BF16),
                   jax.ShapeDtypeStruct((bsz, 2, D_FF), F32)],
        scratch_shapes=[pltpu.VMEM((tm + SUBLANES, D_FF), F32)],
        compiler_params=_params(2),
    )(x2d, gamma.reshape(1, D_MODEL), wg, wu, conv_w, conv_b.reshape(1, D_FF), prev)


def _rel_bucket(rel):
    nb = REL_BUCKETS // 2
    max_exact = nb // 2
    n = jnp.abs(rel)
    nf = jnp.maximum(n, 1).astype(F32)
    large = max_exact + (jnp.log(nf / max_exact) / math.log(REL_MAX_DIST / max_exact)
                         * (nb - max_exact)).astype(jnp.int32)
    large = jnp.minimum(large, nb - 1)
    return jnp.where(rel > 0, nb, 0) + jnp.where(n < max_exact, n, large)


def _bias_tiles(rel_bias, tq, nd):
    r = np.arange(tq)[:, None]
    c = np.arange(tq)[None, :]
    rel = np.stack([(d - (nd - 1)) * tq + c - r for d in range(nd)]).astype(np.int32)
    return jnp.transpose(rel_bias[_rel_bucket(jnp.asarray(rel))], (3, 0, 1, 2)).astype(F32)


def _bias_rows(rel_bias, t, past):
    rel = (np.arange(past + t)[None, :] - (past + np.arange(t))[:, None]).astype(np.int32)
    bias = jnp.transpose(rel_bias[_rel_bucket(jnp.asarray(rel))], (2, 0, 1)).astype(F32)
    return bias[:, :, :past], bias[:, :, past:]


def _forward(x, past_k, past_v, ret_state, gla_state, conv_prev, mem_k, mem_v, p, w):
    bsz, t, _ = x.shape
    n = bsz * t
    past = 0 if past_k is None else past_k.shape[2]
    tm = min(512, n)
    tseq = min(512, t)
    tq = min(256, t)
    c_ret = min(256, t)
    c_gla = CHUNK if t % CHUNK == 0 else t
    tb_gla = min(256, t)
    x2 = x.reshape(n, D_MODEL)
    pos = past + jnp.arange(t, dtype=jnp.int32)
    new_conv = []

    z, ak, av = _in_even(x2, p["ln_mix"][0], w["w_in_even"], tm)
    z3 = z.reshape(bsz, t, EVEN_IN)
    lam_init = 0.8 - 0.6 * math.exp(-0.3 * 0)
    lam = (jnp.exp(jnp.sum(p["diff_lq1"][0].astype(F32) * p["diff_lk1"][0].astype(F32)))
           - jnp.exp(jnp.sum(p["diff_lq2"][0].astype(F32) * p["diff_lk2"][0].astype(F32)))
           + lam_init).reshape(1, 1).astype(F32)
    if past == 0:
        o_a = _diff_attn(lam, z3, _bias_tiles(p["rel_bias"], tq, t // tq), p["diff_subln"][0],
                         tq, 1.0 - lam_init)
    else:
        width = A_HEADS * A_V_DIM
        bias_past, bias_new = _bias_rows(p["rel_bias"], t, past)
        o_a = _diff_attn_cached(lam, z3, past_k[0].reshape(bsz, past, width),
                                past_v[0].reshape(bsz, past, width), bias_past, bias_new,
                                p["diff_subln"][0], 1.0 - lam_init)
    cos, sin = _rotary_tables(pos)
    o_b, s_ret = _retention(z3, cos, sin, _retention_consts(c_ret), ret_state[0], c_ret)
    x2 = _mm_res(x2, [o_a, o_b], [w["w_out_even_a"], w["w_out_even_b"]], tm)
    x2 = _cross(x2, p["ln_cross"][0], w["w_cq"][0], mem_k[0], mem_v[0], w["w_co"][0], t, tseq)
    act, tail = _ffn_act(x2, p["ln_ffn"][0], w["w_ffn_gate"][0], w["w_ffn_up"][0],
                         p["ffn_conv_w"][0], p["ffn_conv_b"][0], conv_prev[0], t, tseq)
    new_conv.append(tail)
    x2 = _mm_res(x2, [act], [w["w_ffn_down"][0]], tm)

    zc, log_a = _in_odd(x2, p["ln_mix"][1], w["w_in_odd"], w["w_gate_lr"], p["b_gate"][0], tm)
    a_stack, lv = _gla_consts(c_gla)
    o_c, s_gla = _gla(zc.reshape(bsz, t, ODD_Z), log_a.reshape(bsz, t, C_HEADS * C_QK_DIM),
                      gla_state[0], a_stack, lv, p["gla_norm"][0], c_gla, tb_gla)
    x2 = _mm_res(x2, [o_c], [w["w_out_odd"]], tm)
    x2 = _cross(x2, p["ln_cross"][1], w["w_cq"][1], mem_k[1], mem_v[1], w["w_co"][1], t, tseq)
    act, tail = _ffn_act(x2, p["ln_ffn"][1], w["w_ffn_gate"][1], w["w_ffn_up"][1],
                         p["ffn_conv_w"][1], p["ffn_conv_b"][1], conv_prev[1], t, tseq)
    new_conv.append(tail)
    _, y = _mm_res(x2, [act], [w["w_ffn_down"][1]], tm, final_gamma=p["ln_final"])

    return (y.reshape(bsz, t, D_MODEL),
            ak.reshape(1, bsz, t, A_HEADS, A_V_DIM), av.reshape(1, bsz, t, A_HEADS, A_V_DIM),
            s_ret[None], s_gla[None], jnp.stack(new_conv))


def kernel(x_prompt, x_sample, cache_diff_k, cache_diff_v, state_retention, state_gla, cache_ffn_conv, cache_mem_k, cache_mem_v, mem_prompt, ln_mix, ln_cross, ln_ffn, ln_mem, ln_final, w_in_even, w_out_even, diff_lq1, diff_lk1, diff_lq2, diff_lk2, diff_subln, rel_bias, w_in_odd, w_gate_lr, b_gate, gla_norm, w_out_odd, w_cq, w_ck, w_cv, w_co, w_ffn_gate, w_ffn_up, ffn_conv_w, ffn_conv_b, w_ffn_down):
    p = dict(ln_mix=ln_mix, ln_cross=ln_cross, ln_ffn=ln_ffn, ln_final=ln_final,
             diff_lq1=diff_lq1, diff_lk1=diff_lk1, diff_lq2=diff_lq2, diff_lk2=diff_lk2,
             diff_subln=diff_subln, rel_bias=rel_bias, b_gate=b_gate, gla_norm=gla_norm,
             ffn_conv_w=ffn_conv_w, ffn_conv_b=ffn_conv_b)
    a_v = A_HEADS * A_V_DIM
    w = dict(
        w_in_even=w_in_even[0].astype(BF16),
        w_out_even_a=w_out_even[0, :a_v].astype(BF16),
        w_out_even_b=w_out_even[0, a_v:].astype(BF16),
        w_in_odd=jnp.pad(w_in_odd[0], ((0, 0), (0, ODD_IN_PAD - w_in_odd.shape[2]))).astype(BF16),
        w_gate_lr=jnp.pad(w_gate_lr[0], ((0, LANES - C_GATE_RANK), (0, 0))).astype(BF16),
        w_out_odd=w_out_odd[0].astype(BF16),
        w_cq=w_cq.astype(BF16), w_co=w_co.astype(BF16),
        w_ffn_gate=w_ffn_gate.astype(BF16), w_ffn_up=w_ffn_up.astype(BF16),
        w_ffn_down=w_ffn_down.astype(BF16))

    bp, mem_len, _ = mem_prompt.shape
    bs = x_sample.shape[0]
    m_width = M_HEADS * M_HEAD_DIM
    mem2 = mem_prompt.reshape(bp * mem_len, D_MODEL)
    depth = ln_mem.shape[0]
    mk_p = [_norm_mm(mem2, ln_mem[l], w_ck[l].astype(BF16), 512) for l in range(depth)]
    mv_p = [_norm_mm(mem2, ln_mem[l], w_cv[l].astype(BF16), 512) for l in range(depth)]
    mem_k_p = jnp.stack(mk_p).reshape(depth, bp, mem_len, M_HEADS, M_HEAD_DIM)
    mem_v_p = jnp.stack(mv_p).reshape(depth, bp, mem_len, M_HEADS, M_HEAD_DIM)

    dt = x_prompt.dtype
    zero_ret = jnp.zeros((1, bp, B_HEADS, B_QK_DIM, B_QK_DIM), dt)
    zero_gla = jnp.zeros((1, bp, C_HEADS, C_QK_DIM, C_V_DIM), dt)
    zero_conv = jnp.zeros((depth, bp, 2, D_FF), dt)
    y_p, dk_p, dv_p, ret_p, gla_p, conv_p = _forward(
        x_prompt, None, None, zero_ret, zero_gla, zero_conv,
        [m.reshape(bp, mem_len, m_width) for m in mk_p],
        [m.reshape(bp, mem_len, m_width) for m in mv_p], p, w)
    y_s, dk_s, dv_s, ret_s, gla_s, conv_s = _forward(
        x_sample, cache_diff_k, cache_diff_v, state_retention, state_gla, cache_ffn_conv,
        cache_mem_k.reshape(depth, bs, mem_len, m_width),
        cache_mem_v.reshape(depth, bs, mem_len, m_width), p, w)
    return (y_p, y_s, dk_p, dv_p, ret_p, gla_p, conv_p, mem_k_p, mem_v_p,
            dk_s, dv_s, ret_s, gla_s, conv_s)
```

```python
import functools
import math

import numpy as np
import jax
import jax.numpy as jnp
from jax import lax
from jax.experimental import pallas as pl
from jax.experimental.pallas import tpu as pltpu

F32 = jnp.float32
BF16 = jnp.bfloat16

D_MODEL = 1024
CHUNK = 64
A_HEADS = 4
A_QK_DIM = 64
A_V_DIM = 128
B_HEADS = 4
B_QK_DIM = 128
C_HEADS = 4
C_QK_DIM = 128
C_V_DIM = 256
C_GATE_RANK = 16
C_GATE_TAU = 16.0
M_HEADS = 4
M_HEAD_DIM = 256
REL_BUCKETS = 32
REL_MAX_DIST = 128
D_FF = 2816
ROPE_BASE = 10000.0
EPS = 1e-6
NEG_INF = -1e30

EVEN_IN = 3584
ODD_Z = 3072
LANES = 128
SUBLANES = 8
ODD_IN_PAD = ODD_Z + LANES
FF_CHUNK = 256
V7X_VMEM_LIMIT_BYTES = 56 * 1024 * 1024


def _params(n_axes):
    return pltpu.CompilerParams(dimension_semantics=("arbitrary",) * n_axes,
                                vmem_limit_bytes=V7X_VMEM_LIMIT_BYTES)


def _dot(a, b):
    return jnp.dot(a, b, preferred_element_type=F32)


def _dot_nt(a, b):
    return lax.dot_general(a, b, (((1,), (1,)), ((), ())), preferred_element_type=F32)


def _dot_tn(a, b):
    return lax.dot_general(a, b, (((0,), (0,)), ((), ())), preferred_element_type=F32)


def _rms(x, g):
    return x * lax.rsqrt(jnp.mean(x * x, axis=-1, keepdims=True) + EPS) * g


def _head_rms(x):
    return x * lax.rsqrt(jnp.mean(x * x, axis=-1, keepdims=True) + EPS)


def _silu(x):
    return x * (1.0 / (1.0 + jnp.exp(-x)))


def _gelu_tanh(x):
    return 0.5 * x * (1.0 + jnp.tanh(math.sqrt(2.0 / math.pi) * (x + 0.044715 * (x * x * x))))


def _log_sigmoid(x):
    return jnp.minimum(x, 0.0) - jnp.log1p(jnp.exp(-jnp.abs(x)))


def _in_even_kernel(x_ref, g_ref, w_ref, z_ref, k_ref, v_ref):
    h = _rms(x_ref[...], g_ref[...]).astype(BF16)
    width = 4 * A_V_DIM
    for c in range(EVEN_IN // width):
        zc = _dot(h, w_ref[:, c * width:(c + 1) * width])
        z_ref[:, c * width:(c + 1) * width] = zc.astype(BF16)
        if c == 1:
            k_ref[...] = zc
        if c == 2:
            v_ref[...] = zc


def _in_even(x2d, gamma, w, tm):
    n = x2d.shape[0]
    width = 4 * A_V_DIM
    return pl.pallas_call(
        _in_even_kernel,
        grid=(n // tm,),
        in_specs=[pl.BlockSpec((tm, D_MODEL), lambda i: (i, 0)),
                  pl.BlockSpec((1, D_MODEL), lambda i: (0, 0)),
                  pl.BlockSpec((D_MODEL, EVEN_IN), lambda i: (0, 0))],
        out_specs=[pl.BlockSpec((tm, EVEN_IN), lambda i: (i, 0)),
                   pl.BlockSpec((tm, width), lambda i: (i, 0)),
                   pl.BlockSpec((tm, width), lambda i: (i, 0))],
        out_shape=[jax.ShapeDtypeStruct((n, EVEN_IN), BF16),
                   jax.ShapeDtypeStruct((n, width), F32),
                   jax.ShapeDtypeStruct((n, width), F32)],
        compiler_params=_params(1),
    )(x2d, gamma.reshape(1, D_MODEL), w)


def _in_odd_kernel(x_ref, g_ref, w_ref, wlr_ref, bg_ref, z_ref, la_ref):
    h = _rms(x_ref[...], g_ref[...]).astype(BF16)
    width = 512
    for c in range(ODD_Z // width):
        z_ref[:, c * width:(c + 1) * width] = _dot(h, w_ref[:, c * width:(c + 1) * width]).astype(BF16)
    ca = _dot(h, w_ref[:, ODD_Z:ODD_IN_PAD]).astype(BF16)
    pre = _dot(ca, wlr_ref[...]) + bg_ref[...]
    la_ref[...] = _log_sigmoid(pre) / C_GATE_TAU


def _in_odd(x2d, gamma, w_pad, wlr_pad, b_gate, tm):
    n = x2d.shape[0]
    qk = C_HEADS * C_QK_DIM
    return pl.pallas_call(
        _in_odd_kernel,
        grid=(n // tm,),
        in_specs=[pl.BlockSpec((tm, D_MODEL), lambda i: (i, 0)),
                  pl.BlockSpec((1, D_MODEL), lambda i: (0, 0)),
                  pl.BlockSpec((D_MODEL, ODD_IN_PAD), lambda i: (0, 0)),
                  pl.BlockSpec((LANES, qk), lambda i: (0, 0)),
                  pl.BlockSpec((1, qk), lambda i: (0, 0))],
        out_specs=[pl.BlockSpec((tm, ODD_Z), lambda i: (i, 0)),
                   pl.BlockSpec((tm, qk), lambda i: (i, 0))],
        out_shape=[jax.ShapeDtypeStruct((n, ODD_Z), BF16),
                   jax.ShapeDtypeStruct((n, qk), F32)],
        compiler_params=_params(1),
    )(x2d, gamma.reshape(1, D_MODEL), w_pad, wlr_pad, b_gate.reshape(1, qk))


def _norm_mm_kernel(x_ref, g_ref, w_ref, o_ref):
    h = _rms(x_ref[...], g_ref[...]).astype(BF16)
    o_ref[...] = _dot(h, w_ref[...])


def _norm_mm(x2d, gamma, w, tm):
    n, nout = x2d.shape[0], w.shape[1]
    return pl.pallas_call(
        _norm_mm_kernel,
        grid=(n // tm,),
        in_specs=[pl.BlockSpec((tm, D_MODEL), lambda i: (i, 0)),
                  pl.BlockSpec((1, D_MODEL), lambda i: (0, 0)),
                  pl.BlockSpec((D_MODEL, nout), lambda i: (0, 0))],
        out_specs=pl.BlockSpec((tm, nout), lambda i: (i, 0)),
        out_shape=jax.ShapeDtypeStruct((n, nout), F32),
        compiler_params=_params(1),
    )(x2d, gamma.reshape(1, D_MODEL), w)


def _mm_res_kernel(*refs, n_in, final_norm):
    res_ref = refs[0]
    a_refs = refs[1:1 + n_in]
    w_refs = refs[1 + n_in:1 + 2 * n_in]
    rest = refs[1 + 2 * n_in:]
    acc = res_ref[...]
    for a_ref, w_ref in zip(a_refs, w_refs):
        acc = acc + _dot(a_ref[...], w_ref[...])
    if final_norm:
        gf_ref, o_ref, y_ref = rest
        o_ref[...] = acc
        y_ref[...] = _rms(acc, gf_ref[...])
    else:
        (o_ref,) = rest
        o_ref[...] = acc


def _mm_res(res, a_list, w_list, tm, final_gamma=None):
    n = res.shape[0]
    n_in = len(a_list)
    in_specs = [pl.BlockSpec((tm, D_MODEL), lambda i: (i, 0))]
    in_specs += [pl.BlockSpec((tm, a.shape[1]), lambda i: (i, 0)) for a in a_list]
    in_specs += [pl.BlockSpec(w.shape, lambda i: (0, 0)) for w in w_list]
    args = [res] + list(a_list) + list(w_list)
    out_spec = pl.BlockSpec((tm, D_MODEL), lambda i: (i, 0))
    out_sds = jax.ShapeDtypeStruct((n, D_MODEL), F32)
    if final_gamma is not None:
        in_specs.append(pl.BlockSpec((1, D_MODEL), lambda i: (0, 0)))
        args.append(final_gamma.reshape(1, D_MODEL))
        out_specs, out_shape = [out_spec, out_spec], [out_sds, out_sds]
    else:
        out_specs, out_shape = out_spec, out_sds
    return pl.pallas_call(
        functools.partial(_mm_res_kernel, n_in=n_in, final_norm=final_gamma is not None),
        grid=(n // tm,),
        in_specs=in_specs,
        out_specs=out_specs,
        out_shape=out_shape,
        compiler_params=_params(1),
    )(*args)


def _diff_attn_kernel(lam_ref, q_ref, k_ref, v_ref, bias_ref, subln_ref, o_ref,
                      vt_sc, m_sc, l_sc, acc_sc, *, tq, nd, out_scale):
    i = pl.program_id(1)
    dv = A_V_DIM

    @pl.when(i == 0)
    def _():
        for h in range(A_HEADS):
            for jj in range(nd):
                vt_sc[h, jj] = v_ref[0, jj * tq:(jj + 1) * tq, h * dv:(h + 1) * dv].T

    lane = lax.broadcasted_iota(jnp.int32, (tq, dv), 1)
    qps = []
    for h in range(A_HEADS):
        q = q_ref[0, :, h * dv:(h + 1) * dv] * (A_QK_DIM ** -0.5)
        zero = jnp.zeros_like(q)
        qps.append(jnp.concatenate([jnp.where(lane < A_QK_DIM, q, zero),
                                    jnp.where(lane >= A_QK_DIM, q, zero)], axis=0))
    m_sc[...] = jnp.full(m_sc.shape, NEG_INF, F32)
    l_sc[...] = jnp.zeros(l_sc.shape, F32)
    acc_sc[...] = jnp.zeros(acc_sc.shape, F32)

    def step(j, masked):
        start = pl.multiple_of(j * tq, tq)
        for h in range(A_HEADS):
            ks = k_ref[0, pl.ds(start, tq), h * dv:(h + 1) * dv]
            b = bias_ref[h, j - i + (nd - 1)]
            s = _dot_nt(ks, qps[h]) + jnp.concatenate([b, b], axis=1)
            if masked:
                c = lax.broadcasted_iota(jnp.int32, (tq, 2 * tq), 0)
                r = lax.broadcasted_iota(jnp.int32, (tq, 2 * tq), 1) % tq
                s = jnp.where((c // CHUNK) <= (r // CHUNK), s, NEG_INF)
            m_old = m_sc[h]
            m_new = jnp.maximum(m_old, jnp.max(s, axis=0, keepdims=True))
            p = jnp.exp(s - m_new)
            alpha = jnp.exp(m_old - m_new)
            l_sc[h] = alpha * l_sc[h] + jnp.sum(p, axis=0, keepdims=True)
            acc_sc[h] = alpha * acc_sc[h] + _dot(vt_sc[h, j], p.astype(BF16))
            m_sc[h] = m_new

    def body(j, carry):
        step(j, False)
        return carry

    lax.fori_loop(0, i, body, 0)
    step(i, True)
    for h in range(A_HEADS):
        out = acc_sc[h] / l_sc[h]
        o = (out[:, :tq] - lam_ref[0, 0] * out[:, tq:]).T
        o_ref[:, h * dv:(h + 1) * dv] = (_head_rms(o) * subln_ref[...] * out_scale).astype(BF16)


def _diff_attn(lam, z3, bias_tiles, subln, tq, out_scale):
    bsz, t, _ = z3.shape
    nd = t // tq
    width = A_HEADS * A_V_DIM
    return pl.pallas_call(
        functools.partial(_diff_attn_kernel, tq=tq, nd=nd, out_scale=out_scale),
        grid=(bsz, nd),
        in_specs=[pl.BlockSpec(memory_space=pltpu.SMEM),
                  pl.BlockSpec((1, tq, width), lambda b, i: (b, i, 0)),
                  pl.BlockSpec((1, t, width), lambda b, i: (b, 0, 1)),
                  pl.BlockSpec((1, t, width), lambda b, i: (b, 0, 2)),
                  pl.BlockSpec((A_HEADS, nd, tq, tq), lambda b, i: (0, 0, 0, 0)),
                  pl.BlockSpec((1, A_V_DIM), lambda b, i: (0, 0))],
        out_specs=pl.BlockSpec((tq, width), lambda b, i: (b * nd + i, 0)),
        out_shape=jax.ShapeDtypeStruct((bsz * t, width), BF16),
        scratch_shapes=[pltpu.VMEM((A_HEADS, nd, A_V_DIM, tq), BF16),
                        pltpu.VMEM((A_HEADS, 1, 2 * tq), F32), pltpu.VMEM((A_HEADS, 1, 2 * tq), F32),
                        pltpu.VMEM((A_HEADS, A_V_DIM, 2 * tq), F32)],
        compiler_params=_params(2),
    )(lam, z3, z3, z3, bias_tiles, subln.reshape(1, A_V_DIM))


def _diff_attn_cached_kernel(lam_ref, q_ref, kn_ref, vn_ref, kp_ref, vp_ref, bp_ref, bn_ref,
                             subln_ref, o_ref, *, t, past, out_scale):
    lane = lax.broadcasted_iota(jnp.int32, (t, A_V_DIM), 1)
    qpos_p = past + lax.broadcasted_iota(jnp.int32, (2 * t, past), 0) % t
    kpos_p = lax.broadcasted_iota(jnp.int32, (2 * t, past), 1)
    ok_p = (kpos_p // CHUNK) <= (qpos_p // CHUNK)
    qpos_n = past + lax.broadcasted_iota(jnp.int32, (2 * t, t), 0) % t
    kpos_n = past + lax.broadcasted_iota(jnp.int32, (2 * t, t), 1)
    ok_n = (kpos_n // CHUNK) <= (qpos_n // CHUNK)
    for h in range(A_HEADS):
        cs = slice(h * A_V_DIM, (h + 1) * A_V_DIM)
        q = q_ref[0, :, cs] * (A_QK_DIM ** -0.5)
        zero = jnp.zeros_like(q)
        qp = jnp.concatenate([jnp.where(lane < A_QK_DIM, q, zero),
                              jnp.where(lane >= A_QK_DIM, q, zero)], axis=0)
        kp = kp_ref[0, :, cs].astype(BF16)
        vp = vp_ref[0, :, cs].astype(BF16)
        kn = kn_ref[0, :, cs]
        vn = vn_ref[0, :, cs]
        bp = bp_ref[h]
        bn = bn_ref[h]
        sp = jnp.where(ok_p, _dot_nt(qp, kp) + jnp.concatenate([bp, bp], axis=0), NEG_INF)
        sn = jnp.where(ok_n, _dot_nt(qp, kn) + jnp.concatenate([bn, bn], axis=0), NEG_INF)
        m = jnp.maximum(jnp.max(sp, axis=-1, keepdims=True), jnp.max(sn, axis=-1, keepdims=True))
        pp = jnp.exp(sp - m)
        pn = jnp.exp(sn - m)
        l = jnp.sum(pp, axis=-1, keepdims=True) + jnp.sum(pn, axis=-1, keepdims=True)
        out = (_dot(pp.astype(BF16), vp) + _dot(pn.astype(BF16), vn)) / l
        o = out[:t] - lam_ref[0, 0] * out[t:]
        o_ref[:, cs] = (_head_rms(o) * subln_ref[...] * out_scale).astype(BF16)


def _diff_attn_cached(lam, z3, past_k, past_v, bias_past, bias_new, subln, out_scale):
    bsz, t, _ = z3.shape
    past = past_k.shape[1]
    width = A_HEADS * A_V_DIM
    return pl.pallas_call(
        functools.partial(_diff_attn_cached_kernel, t=t, past=past, out_scale=out_scale),
        grid=(bsz,),
        in_specs=[pl.BlockSpec(memory_space=pltpu.SMEM),
                  pl.BlockSpec((1, t, width), lambda b: (b, 0, 0)),
                  pl.BlockSpec((1, t, width), lambda b: (b, 0, 1)),
                  pl.BlockSpec((1, t, width), lambda b: (b, 0, 2)),
                  pl.BlockSpec((1, past, width), lambda b: (b, 0, 0)),
                  pl.BlockSpec((1, past, width), lambda b: (b, 0, 0)),
                  pl.BlockSpec((A_HEADS, t, past), lambda b: (0, 0, 0)),
                  pl.BlockSpec((A_HEADS, t, t), lambda b: (0, 0, 0)),
                  pl.BlockSpec((1, A_V_DIM), lambda b: (0, 0))],
        out_specs=pl.BlockSpec((t, width), lambda b: (b, 0)),
        out_shape=jax.ShapeDtypeStruct((bsz * t, width), BF16),
        compiler_params=_params(1),
    )(lam, z3, z3, z3, past_k, past_v, bias_past, bias_new, subln.reshape(1, A_V_DIM))


def _retention_kernel(q_ref, k_ref, v_ref, gt_ref, cos_ref, sin_ref, dec_ref, qd_ref, kd_ref,
                      cd_ref, s0_ref, o_ref, s_out_ref, s_sc):
    t = pl.program_id(1)

    @pl.when(t == 0)
    def _():
        s_sc[...] = s0_ref[0]

    d = B_QK_DIM
    half = d // 2
    cos = cos_ref[...]
    sin = sin_ref[...]
    for h in range(B_HEADS):
        cs = slice(h * d, (h + 1) * d)
        q = q_ref[0, :, cs].astype(F32)
        k = k_ref[0, :, cs].astype(F32)
        qr = (q * cos + pltpu.roll(q, half, 1) * sin) * (d ** -0.5)
        kr = k * cos + pltpu.roll(k, half, 1) * sin
        v = v_ref[0, :, cs]
        att = _dot_nt(qr.astype(BF16), kr.astype(BF16)) * dec_ref[h]
        s = s_sc[h]
        o = _dot(att.astype(BF16), v) + _dot((qr * qd_ref[h]).astype(BF16), s.astype(BF16))
        s_sc[h] = s * cd_ref[h] + _dot_tn((kr * kd_ref[h]).astype(BF16), v)
        gt = gt_ref[0, :, cs].astype(F32)
        o_ref[:, cs] = (_head_rms(o) * _silu(gt)).astype(BF16)

    @pl.when(t == pl.num_programs(1) - 1)
    def _():
        s_out_ref[0] = s_sc[...]


def _retention(z3, cos, sin, consts, s0, c):
    bsz, t, _ = z3.shape
    nt = t // c
    dec, qd, kd, cd = consts
    d = B_QK_DIM
    width = B_HEADS * d
    base = 3
    full3 = lambda b, i: (0, 0, 0)
    return pl.pallas_call(
        _retention_kernel,
        grid=(bsz, nt),
        in_specs=[pl.BlockSpec((1, c, width), lambda b, i: (b, i, base)),
                  pl.BlockSpec((1, c, width), lambda b, i: (b, i, base + 1)),
                  pl.BlockSpec((1, c, width), lambda b, i: (b, i, base + 2)),
                  pl.BlockSpec((1, c, width), lambda b, i: (b, i, base + 3)),
                  pl.BlockSpec((c, d), lambda b, i: (i, 0)),
                  pl.BlockSpec((c, d), lambda b, i: (i, 0)),
                  pl.BlockSpec((B_HEADS, c, c), full3),
                  pl.BlockSpec((B_HEADS, c, d), full3),
                  pl.BlockSpec((B_HEADS, c, d), full3),
                  pl.BlockSpec((B_HEADS, 1, d), full3),
                  pl.BlockSpec((1, B_HEADS, d, d), lambda b, i: (b, 0, 0, 0))],
        out_specs=[pl.BlockSpec((c, width), lambda b, i: (b * nt + i, 0)),
                   pl.BlockSpec((1, B_HEADS, d, d), lambda b, i: (b, 0, 0, 0))],
        out_shape=[jax.ShapeDtypeStruct((bsz * t, width), BF16),
                   jax.ShapeDtypeStruct((bsz, B_HEADS, d, d), F32)],
        scratch_shapes=[pltpu.VMEM((B_HEADS, d, d), F32)],
        compiler_params=_params(2),
    )(z3, z3, z3, z3, cos, sin, dec, qd, kd, cd, s0)


def _retention_consts(c):
    log_g = jnp.log1p(-jnp.exp2(-5.0 - jnp.arange(B_HEADS, dtype=F32)))
    idx = jnp.arange(c, dtype=F32)
    dist = idx[:, None] - idx[None, :]
    dec = jnp.where(dist >= 0, jnp.exp(jnp.maximum(dist, 0.0)[None] * log_g[:, None, None]), 0.0)
    qd = jnp.exp((idx[None, :] + 1.0) * log_g[:, None])
    kd = jnp.exp((c - 1.0 - idx)[None, :] * log_g[:, None])
    cd = jnp.exp(c * log_g)
    bc = lambda a: jnp.broadcast_to(a[..., None], a.shape + (B_QK_DIM,))
    return dec, bc(qd), bc(kd), bc(cd[:, None])


def _rotary_tables(pos):
    half = B_QK_DIM // 2
    inv = ROPE_BASE ** (-jnp.arange(half, dtype=F32) / half)
    ang = pos.astype(F32)[:, None] * inv[None, :]
    cos, sin = jnp.cos(ang), jnp.sin(ang)
    return jnp.concatenate([cos, cos], axis=-1), jnp.concatenate([-sin, sin], axis=-1)


def _gla_levels(c):
    return [c >> (l + 1) for l in range(int(math.log2(c)))]


def _gla_consts(c):
    levels = _gla_levels(c)
    rows = np.arange(c)
    mats = []
    for s in levels:
        ref = (rows // (2 * s)) * 2 * s + s - 1
        a = np.zeros((c, c), np.float32)
        for i in range(c):
            if i & s:
                a[i, ref[i] + 1:i + 1] = 1.0
            else:
                a[i, i + 1:ref[i] + 1] = 1.0
        mats.append(a)
    mats.append(np.tril(np.ones((c, c), np.float32)))
    mats.append(np.triu(np.ones((c, c), np.float32), 1))
    lv = np.full((c, c), -1, np.int32)
    for i in range(c):
        lv[i, i] = len(levels)
        for j in range(i):
            lv[i, j] = levels.index(1 << int(math.floor(math.log2(i ^ j))))
    return jnp.asarray(np.concatenate(mats, axis=0), BF16), jnp.asarray(lv)


def _gla_kernel(q_ref, k_ref, v_ref, r_ref, g_ref, s0_ref, a_ref, lv_ref, nw_ref,
                o_ref, s_out_ref, st_sc, *, c, n_chunks):
    t = pl.program_id(2)

    @pl.when(t == 0)
    def _():
        st_sc[...] = s0_ref[0, 0].T

    levels = _gla_levels(c)
    n_lv = len(levels)
    a = a_ref[...]
    lv = lv_ref[...]
    row = lax.broadcasted_iota(jnp.int32, (c, C_QK_DIM), 0)
    for ci in range(n_chunks):
        rs = slice(ci * c, (ci + 1) * c)
        q = q_ref[0, rs, :].astype(F32) * (C_QK_DIM ** -0.5)
        k = k_ref[0, rs, :].astype(F32)
        v = v_ref[0, rs, :]
        g = g_ref[0, rs, :]
        g_hi = g.astype(BF16)
        g_lo = (g - g_hi.astype(F32)).astype(BF16)
        x2 = _dot(a, jnp.concatenate([g_hi, g_lo], axis=1))
        x = x2[:, :C_QK_DIM] + x2[:, C_QK_DIM:]
        att = jnp.zeros((c, c), F32)
        for l, s in enumerate(levels):
            e = jnp.exp(x[l * c:(l + 1) * c])
            up = (row & s) != 0
            mix = jnp.where(up, q, k) * e
            qt = jnp.where(up, mix, 0.0).astype(BF16)
            kt = jnp.where(up, 0.0, mix).astype(BF16)
            att = jnp.where(lv == l, _dot_nt(qt, kt), att)
        att = jnp.where(lv == n_lv, _dot_nt(q.astype(BF16), k.astype(BF16)), att)
        b = x[n_lv * c:(n_lv + 1) * c]
        rem = x[(n_lv + 1) * c:(n_lv + 2) * c]
        st = st_sc[...]
        o = _dot(att.astype(BF16), v) + _dot_nt((q * jnp.exp(b)).astype(BF16), st.astype(BF16))
        kd = (k * jnp.exp(rem)).astype(BF16)
        st_sc[...] = st * jnp.exp(b[c - 1:c, :]) + _dot_tn(v, kd)
        r = r_ref[0, rs, :].astype(F32)
        o_ref[rs, :] = (_head_rms(o) * nw_ref[...] * _silu(r)).astype(BF16)

    @pl.when(t == pl.num_programs(2) - 1)
    def _():
        s_out_ref[0, 0] = st_sc[...].T


def _gla(z3, log_a3, s0, a_stack, lv, norm_w, c, tb):
    bsz, t, _ = z3.shape
    nt = t // tb
    dk, dv = C_QK_DIM, C_V_DIM
    return pl.pallas_call(
        functools.partial(_gla_kernel, c=c, n_chunks=tb // c),
        grid=(bsz, C_HEADS, nt),
        in_specs=[pl.BlockSpec((1, tb, dk), lambda b, h, i: (b, i, h)),
                  pl.BlockSpec((1, tb, dk), lambda b, h, i: (b, i, C_HEADS + h)),
                  pl.BlockSpec((1, tb, dv), lambda b, h, i: (b, i, C_HEADS + h)),
                  pl.BlockSpec((1, tb, dv), lambda b, h, i: (b, i, 2 * C_HEADS + h)),
                  pl.BlockSpec((1, tb, dk), lambda b, h, i: (b, i, h)),
                  pl.BlockSpec((1, 1, dk, dv), lambda b, h, i: (b, h, 0, 0)),
                  pl.BlockSpec(a_stack.shape, lambda b, h, i: (0, 0)),
                  pl.BlockSpec((c, c), lambda b, h, i: (0, 0)),
                  pl.BlockSpec((1, dv), lambda b, h, i: (0, 0))],
        out_specs=[pl.BlockSpec((tb, dv), lambda b, h, i: (b * nt + i, h)),
                   pl.BlockSpec((1, 1, dk, dv), lambda b, h, i: (b, h, 0, 0))],
        out_shape=[jax.ShapeDtypeStruct((bsz * t, C_HEADS * dv), BF16),
                   jax.ShapeDtypeStruct((bsz, C_HEADS, dk, dv), F32)],
        scratch_shapes=[pltpu.VMEM((dv, dk), F32)],
        compiler_params=_params(3),
    )(z3, z3, z3, z3, log_a3, s0, a_stack, lv, norm_w.reshape(1, dv))


def _cross_kernel(x_ref, g_ref, wq_ref, mk_ref, mv_ref, wo_ref, o_ref):
    x = x_ref[...]
    h = _rms(x, g_ref[...]).astype(BF16)
    q = (_dot(h, wq_ref[...]) * (M_HEAD_DIM ** -0.5)).astype(BF16)
    mk = mk_ref[0].astype(BF16)
    mv = mv_ref[0].astype(BF16)
    outs = []
    for hd in range(M_HEADS):
        cs = slice(hd * M_HEAD_DIM, (hd + 1) * M_HEAD_DIM)
        s = _dot_nt(q[:, cs], mk[:, cs])
        p = jnp.exp(s - jnp.max(s, axis=-1, keepdims=True))
        l = jnp.sum(p, axis=-1, keepdims=True)
        outs.append((_dot(p.astype(BF16), mv[:, cs]) / l).astype(BF16))
    o_ref[...] = x + _dot(jnp.concatenate(outs, axis=1), wo_ref[...])


def _cross(x2d, gamma, wq, mk3, mv3, wo, t, tm):
    n = x2d.shape[0]
    nt = t // tm
    mem_len, width = mk3.shape[1], mk3.shape[2]
    return pl.pallas_call(
        _cross_kernel,
        grid=(n // t, nt),
        in_specs=[pl.BlockSpec((tm, D_MODEL), lambda b, i: (b * nt + i, 0)),
                  pl.BlockSpec((1, D_MODEL), lambda b, i: (0, 0)),
                  pl.BlockSpec((D_MODEL, width), lambda b, i: (0, 0)),
                  pl.BlockSpec((1, mem_len, width), lambda b, i: (b, 0, 0)),
                  pl.BlockSpec((1, mem_len, width), lambda b, i: (b, 0, 0)),
                  pl.BlockSpec((width, D_MODEL), lambda b, i: (0, 0))],
        out_specs=pl.BlockSpec((tm, D_MODEL), lambda b, i: (b * nt + i, 0)),
        out_shape=jax.ShapeDtypeStruct((n, D_MODEL), F32),
        compiler_params=_params(2),
    )(x2d, gamma.reshape(1, D_MODEL), wq, mk3, mv3, wo)


def _ffn_act_kernel(x_ref, g_ref, wg_ref, wu_ref, cw_ref, cb_ref, prev_ref, act_ref, tail_ref,
                    gbuf, *, tm):
    t = pl.program_id(1)
    lo = SUBLANES - 2

    @pl.when(t == 0)
    def _():
        gbuf[lo:SUBLANES, :] = prev_ref[0]

    h = _rms(x_ref[...], g_ref[...]).astype(BF16)
    for c in range(D_FF // FF_CHUNK):
        cs = slice(c * FF_CHUNK, (c + 1) * FF_CHUNK)
        gate = _dot(h, wg_ref[:, cs])
        gbuf[SUBLANES:SUBLANES + tm, cs] = gate
        conv = cb_ref[:, cs] + cw_ref[0:1, cs] * gbuf[lo:lo + tm, cs]
        conv = conv + cw_ref[1:2, cs] * gbuf[lo + 1:lo + 1 + tm, cs]
        conv = conv + cw_ref[2:3, cs] * gate
        up = _dot(h, wu_ref[:, cs])
        act_ref[:, cs] = (_gelu_tanh(conv) * up).astype(BF16)
    tail = gbuf[tm + lo:tm + SUBLANES, :]
    gbuf[lo:SUBLANES, :] = tail

    @pl.when(t == pl.num_programs(1) - 1)
    def _():
        tail_ref[0] = tail


def _ffn_act(x2d, gamma, wg, wu, conv_w, conv_b, prev, t, tm):
    n = x2d.shape[0]
    nt = t // tm
    bsz = n // t
    return pl.pallas_call(
        functools.partial(_ffn_act_kernel, tm=tm),
        grid=(bsz, nt),
        in_specs=[pl.BlockSpec((tm, D_MODEL), lambda b, i: (b * nt + i, 0)),
                  pl.BlockSpec((1, D_MODEL), lambda b, i: (0, 0)),
                  pl.BlockSpec((D_MODEL, D_FF), lambda b, i: (0, 0)),
                  pl.BlockSpec((D_MODEL, D_FF), lambda b, i: (0, 0)),
                  pl.BlockSpec((3, D_FF), lambda b, i: (0, 0)),
                  pl.BlockSpec((1, D_FF), lambda b, i: (0, 0)),
                  pl.BlockSpec((1, 2, D_FF), lambda b, i: (b, 0, 0))],
        out_specs=[pl.BlockSpec((tm, D_FF), lambda b, i: (b * nt + i, 0)),
                   pl.BlockSpec((1, 2, D_FF), lambda b, i: (b, 0, 0))],
        out_shape=[jax.ShapeDtypeStruct((n, D_FF), BF16),
                   jax.ShapeDtypeStruct((bsz, 2, D_FF), F32)],
        scratch_shapes=[pltpu.VMEM((tm + SUBLANES, D_FF), F32)],
        compiler_params=_params(2),
    )(x2d, gamma.reshape(1, D_MODEL), wg, wu, conv_w, conv_b.reshape(1, D_FF), prev)


def _rel_bucket(rel):
    nb = REL_BUCKETS // 2
    max_exact = nb // 2
    n = jnp.abs(rel)
    nf = jnp.maximum(n, 1).astype(F32)
    large = max_exact + (jnp.log(nf / max_exact) / math.log(REL_MAX_DIST / max_exact)
                         * (nb - max_exact)).astype(jnp.int32)
    large = jnp.minimum(large, nb - 1)
    return jnp.where(rel > 0, nb, 0) + jnp.where(n < max_exact, n, large)


def _bias_tiles(rel_bias, tq, nd):
    period = 2 * tq
    n = np.arange(period)
    rel = np.stack([np.where(n < tq, (d - (nd - 1)) * tq - n, (d - (nd - 1)) * tq + period - n)
                    for d in range(nd)]).astype(np.int32)
    w = jnp.transpose(rel_bias[_rel_bucket(jnp.asarray(rel))], (2, 0, 1)).astype(F32)
    flat = jnp.tile(w, (1, 1, tq))[:, :, :tq * (period - 1)]
    return flat.reshape(A_HEADS, nd, tq, period - 1)[:, :, :, :tq]


def _bias_rows(rel_bias, t, past):
    rel = (np.arange(past + t)[None, :] - (past + np.arange(t))[:, None]).astype(np.int32)
    bias = jnp.transpose(rel_bias[_rel_bucket(jnp.asarray(rel))], (2, 0, 1)).astype(F32)
    return bias[:, :, :past], bias[:, :, past:]


def _forward(x, past_k, past_v, ret_state, gla_state, conv_prev, mem_k, mem_v, p, w):
    bsz, t, _ = x.shape
    n = bsz * t
    past = 0 if past_k is None else past_k.shape[2]
    tm = min(512, n)
    tseq = min(512, t)
    tq = min(256, t)
    c_ret = min(256, t)
    c_gla = CHUNK if t % CHUNK == 0 else t
    tb_gla = min(256, t)
    x2 = x.reshape(n, D_MODEL)
    pos = past + jnp.arange(t, dtype=jnp.int32)
    new_conv = []

    z, ak, av = _in_even(x2, p["ln_mix"][0], w["w_in_even"], tm)
    z3 = z.reshape(bsz, t, EVEN_IN)
    lam_init = 0.8 - 0.6 * math.exp(-0.3 * 0)
    lam = (jnp.exp(jnp.sum(p["diff_lq1"][0].astype(F32) * p["diff_lk1"][0].astype(F32)))
           - jnp.exp(jnp.sum(p["diff_lq2"][0].astype(F32) * p["diff_lk2"][0].astype(F32)))
           + lam_init).reshape(1, 1).astype(F32)
    if past == 0:
        o_a = _diff_attn(lam, z3, _bias_tiles(p["rel_bias"], tq, t // tq), p["diff_subln"][0],
                         tq, 1.0 - lam_init)
    else:
        width = A_HEADS * A_V_DIM
        bias_past, bias_new = _bias_rows(p["rel_bias"], t, past)
        o_a = _diff_attn_cached(lam, z3, past_k[0].reshape(bsz, past, width),
                                past_v[0].reshape(bsz, past, width), bias_past, bias_new,
                                p["diff_subln"][0], 1.0 - lam_init)
    cos, sin = _rotary_tables(pos)
    o_b, s_ret = _retention(z3, cos, sin, _retention_consts(c_ret), ret_state[0], c_ret)
    x2 = _mm_res(x2, [o_a, o_b], [w["w_out_even_a"], w["w_out_even_b"]], tm)
    x2 = _cross(x2, p["ln_cross"][0], w["w_cq"][0], mem_k[0], mem_v[0], w["w_co"][0], t, tseq)
    act, tail = _ffn_act(x2, p["ln_ffn"][0], w["w_ffn_gate"][0], w["w_ffn_up"][0],
                         p["ffn_conv_w"][0], p["ffn_conv_b"][0], conv_prev[0], t, tseq)
    new_conv.append(tail)
    x2 = _mm_res(x2, [act], [w["w_ffn_down"][0]], tm)

    zc, log_a = _in_odd(x2, p["ln_mix"][1], w["w_in_odd"], w["w_gate_lr"], p["b_gate"][0], tm)
    a_stack, lv = _gla_consts(c_gla)
    o_c, s_gla = _gla(zc.reshape(bsz, t, ODD_Z), log_a.reshape(bsz, t, C_HEADS * C_QK_DIM),
                      gla_state[0], a_stack, lv, p["gla_norm"][0], c_gla, tb_gla)
    x2 = _mm_res(x2, [o_c], [w["w_out_odd"]], tm)
    x2 = _cross(x2, p["ln_cross"][1], w["w_cq"][1], mem_k[1], mem_v[1], w["w_co"][1], t, tseq)
    act, tail = _ffn_act(x2, p["ln_ffn"][1], w["w_ffn_gate"][1], w["w_ffn_up"][1],
                         p["ffn_conv_w"][1], p["ffn_conv_b"][1], conv_prev[1], t, tseq)
    new_conv.append(tail)
    _, y = _mm_res(x2, [act], [w["w_ffn_down"][1]], tm, final_gamma=p["ln_final"])

    return (y.reshape(bsz, t, D_MODEL),
            ak.reshape(1, bsz, t, A_HEADS, A_V_DIM), av.reshape(1, bsz, t, A_HEADS, A_V_DIM),
            s_ret[None], s_gla[None], jnp.stack(new_conv))


def kernel(x_prompt, x_sample, cache_diff_k, cache_diff_v, state_retention, state_gla, cache_ffn_conv, cache_mem_k, cache_mem_v, mem_prompt, ln_mix, ln_cross, ln_ffn, ln_mem, ln_final, w_in_even, w_out_even, diff_lq1, diff_lk1, diff_lq2, diff_lk2, diff_subln, rel_bias, w_in_odd, w_gate_lr, b_gate, gla_norm, w_out_odd, w_cq, w_ck, w_cv, w_co, w_ffn_gate, w_ffn_up, ffn_conv_w, ffn_conv_b, w_ffn_down):
    p = dict(ln_mix=ln_mix, ln_cross=ln_cross, ln_ffn=ln_ffn, ln_final=ln_final,
             diff_lq1=diff_lq1, diff_lk1=diff_lk1, diff_lq2=diff_lq2, diff_lk2=diff_lk2,
             diff_subln=diff_subln, rel_bias=rel_bias, b_gate=b_gate, gla_norm=gla_norm,
             ffn_conv_w=ffn_conv_w, ffn_conv_b=ffn_conv_b)
    a_v = A_HEADS * A_V_DIM
    w = dict(
        w_in_even=w_in_even[0].astype(BF16),
        w_out_even_a=w_out_even[0, :a_v].astype(BF16),
        w_out_even_b=w_out_even[0, a_v:].astype(BF16),
        w_in_odd=jnp.pad(w_in_odd[0], ((0, 0), (0, ODD_IN_PAD - w_in_odd.shape[2]))).astype(BF16),
        w_gate_lr=jnp.pad(w_gate_lr[0], ((0, LANES - C_GATE_RANK), (0, 0))).astype(BF16),
        w_out_odd=w_out_odd[0].astype(BF16),
        w_cq=w_cq.astype(BF16), w_co=w_co.astype(BF16),
        w_ffn_gate=w_ffn_gate.astype(BF16), w_ffn_up=w_ffn_up.astype(BF16),
        w_ffn_down=w_ffn_down.astype(BF16))

    bp, mem_len, _ = mem_prompt.shape
    bs = x_sample.shape[0]
    m_width = M_HEADS * M_HEAD_DIM
    mem2 = mem_prompt.reshape(bp * mem_len, D_MODEL)
    depth = ln_mem.shape[0]
    mk_p = [_norm_mm(mem2, ln_mem[l], w_ck[l].astype(BF16), 512) for l in range(depth)]
    mv_p = [_norm_mm(mem2, ln_mem[l], w_cv[l].astype(BF16), 512) for l in range(depth)]
    mem_k_p = jnp.stack(mk_p).reshape(depth, bp, mem_len, M_HEADS, M_HEAD_DIM)
    mem_v_p = jnp.stack(mv_p).reshape(depth, bp, mem_len, M_HEADS, M_HEAD_DIM)

    dt = x_prompt.dtype
    zero_ret = jnp.zeros((1, bp, B_HEADS, B_QK_DIM, B_QK_DIM), dt)
    zero_gla = jnp.zeros((1, bp, C_HEADS, C_QK_DIM, C_V_DIM), dt)
    zero_conv = jnp.zeros((depth, bp, 2, D_FF), dt)
    y_p, dk_p, dv_p, ret_p, gla_p, conv_p = _forward(
        x_prompt, None, None, zero_ret, zero_gla, zero_conv,
        [m.reshape(bp, mem_len, m_width) for m in mk_p],
        [m.reshape(bp, mem_len, m_width) for m in mv_p], p, w)
    y_s, dk_s, dv_s, ret_s, gla_s, conv_s = _forward(
        x_sample, cache_diff_k, cache_diff_v, state_retention, state_gla, cache_ffn_conv,
        cache_mem_k.reshape(depth, bs, mem_len, m_width),
        cache_mem_v.reshape(depth, bs, mem_len, m_width), p, w)
    return (y_p, y_s, dk_p, dv_p, ret_p, gla_p, conv_p, mem_k_p, mem_v_p,
            dk_s, dv_s, ret_s, gla_s, conv_s)
```

```python
import functools
import math

import numpy as np
import jax
import jax.numpy as jnp
from jax import lax
from jax.experimental import pallas as pl
from jax.experimental.pallas import tpu as pltpu

F32 = jnp.float32
BF16 = jnp.bfloat16

D_MODEL = 1024
CHUNK = 64
A_HEADS = 4
A_QK_DIM = 64
A_V_DIM = 128
B_HEADS = 4
B_QK_DIM = 128
C_HEADS = 4
C_QK_DIM = 128
C_V_DIM = 256
C_GATE_RANK = 16
C_GATE_TAU = 16.0
M_HEADS = 4
M_HEAD_DIM = 256
REL_BUCKETS = 32
REL_MAX_DIST = 128
D_FF = 2816
ROPE_BASE = 10000.0
EPS = 1e-6
NEG_INF = -1e30

EVEN_IN = 3584
ODD_Z = 3072
LANES = 128
SUBLANES = 8
ODD_IN_PAD = ODD_Z + LANES
FF_CHUNK = 256
V7X_VMEM_LIMIT_BYTES = 56 * 1024 * 1024


def _params(n_axes):
    return pltpu.CompilerParams(dimension_semantics=("arbitrary",) * n_axes,
                                vmem_limit_bytes=V7X_VMEM_LIMIT_BYTES)


def _dot(a, b):
    return jnp.dot(a, b, preferred_element_type=F32)


def _dot_nt(a, b):
    return lax.dot_general(a, b, (((1,), (1,)), ((), ())), preferred_element_type=F32)


def _dot_tn(a, b):
    return lax.dot_general(a, b, (((0,), (0,)), ((), ())), preferred_element_type=F32)


def _rms(x, g):
    return x * lax.rsqrt(jnp.mean(x * x, axis=-1, keepdims=True) + EPS) * g


def _head_rms(x):
    return x * lax.rsqrt(jnp.mean(x * x, axis=-1, keepdims=True) + EPS)


def _silu(x):
    return x * (1.0 / (1.0 + jnp.exp(-x)))


def _gelu_tanh(x):
    c0 = math.sqrt(2.0 / math.pi)
    return x * (0.5 + 0.5 * jnp.tanh(x * (c0 + (c0 * 0.044715) * (x * x))))


def _log_sigmoid(x):
    return jnp.minimum(x, 0.0) - jnp.log1p(jnp.exp(-jnp.abs(x)))


def _in_even_kernel(x_ref, g_ref, w_ref, z_ref, k_ref, v_ref):
    h = _rms(x_ref[...], g_ref[...]).astype(BF16)
    width = 4 * A_V_DIM
    for c in range(EVEN_IN // width):
        zc = _dot(h, w_ref[:, c * width:(c + 1) * width])
        z_ref[:, c * width:(c + 1) * width] = zc.astype(BF16)
        if c == 1:
            k_ref[...] = zc
        if c == 2:
            v_ref[...] = zc


def _in_even(x2d, gamma, w, tm):
    n = x2d.shape[0]
    width = 4 * A_V_DIM
    return pl.pallas_call(
        _in_even_kernel,
        grid=(n // tm,),
        in_specs=[pl.BlockSpec((tm, D_MODEL), lambda i: (i, 0)),
                  pl.BlockSpec((1, D_MODEL), lambda i: (0, 0)),
                  pl.BlockSpec((D_MODEL, EVEN_IN), lambda i: (0, 0))],
        out_specs=[pl.BlockSpec((tm, EVEN_IN), lambda i: (i, 0)),
                   pl.BlockSpec((tm, width), lambda i: (i, 0)),
                   pl.BlockSpec((tm, width), lambda i: (i, 0))],
        out_shape=[jax.ShapeDtypeStruct((n, EVEN_IN), BF16),
                   jax.ShapeDtypeStruct((n, width), F32),
                   jax.ShapeDtypeStruct((n, width), F32)],
        compiler_params=_params(1),
    )(x2d, gamma.reshape(1, D_MODEL), w)


def _in_odd_kernel(x_ref, g_ref, w_ref, wlr_ref, bg_ref, z_ref, la_ref):
    h = _rms(x_ref[...], g_ref[...]).astype(BF16)
    width = 512
    for c in range(ODD_Z // width):
        z_ref[:, c * width:(c + 1) * width] = _dot(h, w_ref[:, c * width:(c + 1) * width]).astype(BF16)
    ca = _dot(h, w_ref[:, ODD_Z:ODD_IN_PAD]).astype(BF16)
    pre = _dot(ca, wlr_ref[...]) + bg_ref[...]
    la_ref[...] = _log_sigmoid(pre) / C_GATE_TAU


def _in_odd(x2d, gamma, w_pad, wlr_pad, b_gate, tm):
    n = x2d.shape[0]
    qk = C_HEADS * C_QK_DIM
    return pl.pallas_call(
        _in_odd_kernel,
        grid=(n // tm,),
        in_specs=[pl.BlockSpec((tm, D_MODEL), lambda i: (i, 0)),
                  pl.BlockSpec((1, D_MODEL), lambda i: (0, 0)),
                  pl.BlockSpec((D_MODEL, ODD_IN_PAD), lambda i: (0, 0)),
                  pl.BlockSpec((LANES, qk), lambda i: (0, 0)),
                  pl.BlockSpec((1, qk), lambda i: (0, 0))],
        out_specs=[pl.BlockSpec((tm, ODD_Z), lambda i: (i, 0)),
                   pl.BlockSpec((tm, qk), lambda i: (i, 0))],
        out_shape=[jax.ShapeDtypeStruct((n, ODD_Z), BF16),
                   jax.ShapeDtypeStruct((n, qk), F32)],
        compiler_params=_params(1),
    )(x2d, gamma.reshape(1, D_MODEL), w_pad, wlr_pad, b_gate.reshape(1, qk))


def _norm_mm_kernel(x_ref, g_ref, w_ref, o_ref):
    h = _rms(x_ref[...], g_ref[...]).astype(BF16)
    o_ref[...] = _dot(h, w_ref[...])


def _norm_mm(x2d, gamma, w, tm):
    n, nout = x2d.shape[0], w.shape[1]
    return pl.pallas_call(
        _norm_mm_kernel,
        grid=(n // tm,),
        in_specs=[pl.BlockSpec((tm, D_MODEL), lambda i: (i, 0)),
                  pl.BlockSpec((1, D_MODEL), lambda i: (0, 0)),
                  pl.BlockSpec((D_MODEL, nout), lambda i: (0, 0))],
        out_specs=pl.BlockSpec((tm, nout), lambda i: (i, 0)),
        out_shape=jax.ShapeDtypeStruct((n, nout), F32),
        compiler_params=_params(1),
    )(x2d, gamma.reshape(1, D_MODEL), w)


def _mm_res_kernel(*refs, n_in):
    res_ref = refs[0]
    a_refs = refs[1:1 + n_in]
    w_refs = refs[1 + n_in:1 + 2 * n_in]
    o_ref = refs[1 + 2 * n_in]
    acc = res_ref[...]
    for a_ref, w_ref in zip(a_refs, w_refs):
        acc = acc + _dot(a_ref[...], w_ref[...])
    o_ref[...] = acc


def _mm_res(res, a_list, w_list, tm):
    n = res.shape[0]
    n_in = len(a_list)
    in_specs = [pl.BlockSpec((tm, D_MODEL), lambda i: (i, 0))]
    in_specs += [pl.BlockSpec((tm, a.shape[1]), lambda i: (i, 0)) for a in a_list]
    in_specs += [pl.BlockSpec(w.shape, lambda i: (0, 0)) for w in w_list]
    return pl.pallas_call(
        functools.partial(_mm_res_kernel, n_in=n_in),
        grid=(n // tm,),
        in_specs=in_specs,
        out_specs=pl.BlockSpec((tm, D_MODEL), lambda i: (i, 0)),
        out_shape=jax.ShapeDtypeStruct((n, D_MODEL), F32),
        compiler_params=_params(1),
    )(res, *a_list, *w_list)


def _diff_attn_kernel(lam_ref, q_ref, k_ref, v_ref, bias_ref, subln_ref, o_ref,
                      vt_sc, m_sc, l_sc, acc_sc, *, tq, nd, out_scale):
    i = pl.program_id(1)
    dv = A_V_DIM

    @pl.when(i == 0)
    def _():
        for h in range(A_HEADS):
            for jj in range(nd):
                vt_sc[h, jj] = v_ref[0, jj * tq:(jj + 1) * tq, h * dv:(h + 1) * dv].T

    lane = lax.broadcasted_iota(jnp.int32, (tq, dv), 1)
    qps = []
    for h in range(A_HEADS):
        q = q_ref[0, :, h * dv:(h + 1) * dv] * (A_QK_DIM ** -0.5)
        zero = jnp.zeros_like(q)
        qps.append(jnp.concatenate([jnp.where(lane < A_QK_DIM, q, zero),
                                    jnp.where(lane >= A_QK_DIM, q, zero)], axis=0))
    m_sc[...] = jnp.full(m_sc.shape, NEG_INF, F32)
    l_sc[...] = jnp.zeros(l_sc.shape, F32)
    acc_sc[...] = jnp.zeros(acc_sc.shape, F32)

    def step(j, masked):
        start = pl.multiple_of(j * tq, tq)
        scores = [_dot_nt(k_ref[0, pl.ds(start, tq), h * dv:(h + 1) * dv], qps[h])
                  for h in range(A_HEADS)]
        for h in range(A_HEADS):
            b = bias_ref[h, j - i + (nd - 1)]
            s = scores[h] + jnp.concatenate([b, b], axis=1)
            if masked:
                c = lax.broadcasted_iota(jnp.int32, (tq, 2 * tq), 0)
                r = lax.broadcasted_iota(jnp.int32, (tq, 2 * tq), 1) % tq
                s = jnp.where((c // CHUNK) <= (r // CHUNK), s, NEG_INF)
            m_old = m_sc[h]
            m_new = jnp.maximum(m_old, jnp.max(s, axis=0, keepdims=True))
            p = jnp.exp(s - m_new)
            alpha = jnp.exp(m_old - m_new)
            l_sc[h] = alpha * l_sc[h] + jnp.sum(p, axis=0, keepdims=True)
            acc_sc[h] = alpha * acc_sc[h] + _dot(vt_sc[h, j], p.astype(BF16))
            m_sc[h] = m_new

    def body(j, carry):
        step(j, False)
        return carry

    lax.fori_loop(0, i, body, 0)
    step(i, True)
    for h in range(A_HEADS):
        out = acc_sc[h] / l_sc[h]
        o = (out[:, :tq] - lam_ref[0, 0] * out[:, tq:]).T
        o_ref[:, h * dv:(h + 1) * dv] = (_head_rms(o) * subln_ref[...] * out_scale).astype(BF16)


def _diff_attn(lam, z3, bias_tiles, subln, tq, out_scale):
    bsz, t, _ = z3.shape
    nd = t // tq
    width = A_HEADS * A_V_DIM
    return pl.pallas_call(
        functools.partial(_diff_attn_kernel, tq=tq, nd=nd, out_scale=out_scale),
        grid=(bsz, nd),
        in_specs=[pl.BlockSpec(memory_space=pltpu.SMEM),
                  pl.BlockSpec((1, tq, width), lambda b, i: (b, i, 0)),
                  pl.BlockSpec((1, t, width), lambda b, i: (b, 0, 1)),
                  pl.BlockSpec((1, t, width), lambda b, i: (b, 0, 2)),
                  pl.BlockSpec((A_HEADS, nd, tq, tq), lambda b, i: (0, 0, 0, 0)),
                  pl.BlockSpec((1, A_V_DIM), lambda b, i: (0, 0))],
        out_specs=pl.BlockSpec((tq, width), lambda b, i: (b * nd + i, 0)),
        out_shape=jax.ShapeDtypeStruct((bsz * t, width), BF16),
        scratch_shapes=[pltpu.VMEM((A_HEADS, nd, A_V_DIM, tq), BF16),
                        pltpu.VMEM((A_HEADS, 1, 2 * tq), F32), pltpu.VMEM((A_HEADS, 1, 2 * tq), F32),
                        pltpu.VMEM((A_HEADS, A_V_DIM, 2 * tq), F32)],
        compiler_params=_params(2),
    )(lam, z3, z3, z3, bias_tiles, subln.reshape(1, A_V_DIM))


def _diff_attn_cached_kernel(lam_ref, q_ref, kn_ref, vn_ref, kp_ref, vp_ref, bp_ref, bn_ref,
                             subln_ref, o_ref, *, t, past, out_scale):
    lane = lax.broadcasted_iota(jnp.int32, (t, A_V_DIM), 1)
    qpos_p = past + lax.broadcasted_iota(jnp.int32, (2 * t, past), 0) % t
    kpos_p = lax.broadcasted_iota(jnp.int32, (2 * t, past), 1)
    ok_p = (kpos_p // CHUNK) <= (qpos_p // CHUNK)
    qpos_n = past + lax.broadcasted_iota(jnp.int32, (2 * t, t), 0) % t
    kpos_n = past + lax.broadcasted_iota(jnp.int32, (2 * t, t), 1)
    ok_n = (kpos_n // CHUNK) <= (qpos_n // CHUNK)
    for h in range(A_HEADS):
        cs = slice(h * A_V_DIM, (h + 1) * A_V_DIM)
        q = q_ref[0, :, cs] * (A_QK_DIM ** -0.5)
        zero = jnp.zeros_like(q)
        qp = jnp.concatenate([jnp.where(lane < A_QK_DIM, q, zero),
                              jnp.where(lane >= A_QK_DIM, q, zero)], axis=0)
        kp = kp_ref[0, :, cs].astype(BF16)
        vp = vp_ref[0, :, cs].astype(BF16)
        kn = kn_ref[0, :, cs]
        vn = vn_ref[0, :, cs]
        bp = bp_ref[h]
        bn = bn_ref[h]
        sp = jnp.where(ok_p, _dot_nt(qp, kp) + jnp.concatenate([bp, bp], axis=0), NEG_INF)
        sn = jnp.where(ok_n, _dot_nt(qp, kn) + jnp.concatenate([bn, bn], axis=0), NEG_INF)
        m = jnp.maximum(jnp.max(sp, axis=-1, keepdims=True), jnp.max(sn, axis=-1, keepdims=True))
        pp = jnp.exp(sp - m)
        pn = jnp.exp(sn - m)
        l = jnp.sum(pp, axis=-1, keepdims=True) + jnp.sum(pn, axis=-1, keepdims=True)
        out = (_dot(pp.astype(BF16), vp) + _dot(pn.astype(BF16), vn)) / l
        o = out[:t] - lam_ref[0, 0] * out[t:]
        o_ref[:, cs] = (_head_rms(o) * subln_ref[...] * out_scale).astype(BF16)


def _diff_attn_cached(lam, z3, past_k, past_v, bias_past, bias_new, subln, out_scale):
    bsz, t, _ = z3.shape
    past = past_k.shape[1]
    width = A_HEADS * A_V_DIM
    return pl.pallas_call(
        functools.partial(_diff_attn_cached_kernel, t=t, past=past, out_scale=out_scale),
        grid=(bsz,),
        in_specs=[pl.BlockSpec(memory_space=pltpu.SMEM),
                  pl.BlockSpec((1, t, width), lambda b: (b, 0, 0)),
                  pl.BlockSpec((1, t, width), lambda b: (b, 0, 1)),
                  pl.BlockSpec((1, t, width), lambda b: (b, 0, 2)),
                  pl.BlockSpec((1, past, width), lambda b: (b, 0, 0)),
                  pl.BlockSpec((1, past, width), lambda b: (b, 0, 0)),
                  pl.BlockSpec((A_HEADS, t, past), lambda b: (0, 0, 0)),
                  pl.BlockSpec((A_HEADS, t, t), lambda b: (0, 0, 0)),
                  pl.BlockSpec((1, A_V_DIM), lambda b: (0, 0))],
        out_specs=pl.BlockSpec((t, width), lambda b: (b, 0)),
        out_shape=jax.ShapeDtypeStruct((bsz * t, width), BF16),
        compiler_params=_params(1),
    )(lam, z3, z3, z3, past_k, past_v, bias_past, bias_new, subln.reshape(1, A_V_DIM))


def _retention_kernel(q_ref, k_ref, v_ref, gt_ref, cos_ref, sin_ref, dec_ref, qd_ref, kd_ref,
                      cd_ref, s0_ref, o_ref, s_out_ref, s_sc):
    t = pl.program_id(1)

    @pl.when(t == 0)
    def _():
        s_sc[...] = s0_ref[0]

    d = B_QK_DIM
    half = d // 2
    cos = cos_ref[...]
    sin = sin_ref[...]
    for h in range(B_HEADS):
        cs = slice(h * d, (h + 1) * d)
        q = q_ref[0, :, cs].astype(F32)
        k = k_ref[0, :, cs].astype(F32)
        qr = (q * cos + pltpu.roll(q, half, 1) * sin) * (d ** -0.5)
        kr = k * cos + pltpu.roll(k, half, 1) * sin
        v = v_ref[0, :, cs]
        att = _dot_nt(qr.astype(BF16), kr.astype(BF16)) * dec_ref[h]
        s = s_sc[h]
        o = _dot(att.astype(BF16), v) + _dot((qr * qd_ref[h]).astype(BF16), s.astype(BF16))
        s_sc[h] = s * cd_ref[h] + _dot_tn((kr * kd_ref[h]).astype(BF16), v)
        gt = gt_ref[0, :, cs].astype(F32)
        o_ref[:, cs] = (_head_rms(o) * _silu(gt)).astype(BF16)

    @pl.when(t == pl.num_programs(1) - 1)
    def _():
        s_out_ref[0] = s_sc[...]


def _retention(z3, cos, sin, consts, s0, c):
    bsz, t, _ = z3.shape
    nt = t // c
    dec, qd, kd, cd = consts
    d = B_QK_DIM
    width = B_HEADS * d
    base = 3
    full3 = lambda b, i: (0, 0, 0)
    return pl.pallas_call(
        _retention_kernel,
        grid=(bsz, nt),
        in_specs=[pl.BlockSpec((1, c, width), lambda b, i: (b, i, base)),
                  pl.BlockSpec((1, c, width), lambda b, i: (b, i, base + 1)),
                  pl.BlockSpec((1, c, width), lambda b, i: (b, i, base + 2)),
                  pl.BlockSpec((1, c, width), lambda b, i: (b, i, base + 3)),
                  pl.BlockSpec((c, d), lambda b, i: (i, 0)),
                  pl.BlockSpec((c, d), lambda b, i: (i, 0)),
                  pl.BlockSpec((B_HEADS, c, c), full3),
                  pl.BlockSpec((B_HEADS, c, d), full3),
                  pl.BlockSpec((B_HEADS, c, d), full3),
                  pl.BlockSpec((B_HEADS, 1, d), full3),
                  pl.BlockSpec((1, B_HEADS, d, d), lambda b, i: (b, 0, 0, 0))],
        out_specs=[pl.BlockSpec((c, width), lambda b, i: (b * nt + i, 0)),
                   pl.BlockSpec((1, B_HEADS, d, d), lambda b, i: (b, 0, 0, 0))],
        out_shape=[jax.ShapeDtypeStruct((bsz * t, width), BF16),
                   jax.ShapeDtypeStruct((bsz, B_HEADS, d, d), F32)],
        scratch_shapes=[pltpu.VMEM((B_HEADS, d, d), F32)],
        compiler_params=_params(2),
    )(z3, z3, z3, z3, cos, sin, dec, qd, kd, cd, s0)


def _retention_consts(c):
    log_g = jnp.log1p(-jnp.exp2(-5.0 - jnp.arange(B_HEADS, dtype=F32)))
    idx = jnp.arange(c, dtype=F32)
    dist = idx[:, None] - idx[None, :]
    dec = jnp.where(dist >= 0, jnp.exp(jnp.maximum(dist, 0.0)[None] * log_g[:, None, None]), 0.0)
    qd = jnp.exp((idx[None, :] + 1.0) * log_g[:, None])
    kd = jnp.exp((c - 1.0 - idx)[None, :] * log_g[:, None])
    cd = jnp.exp(c * log_g)
    bc = lambda a: jnp.broadcast_to(a[..., None], a.shape + (B_QK_DIM,))
    return dec, bc(qd), bc(kd), bc(cd[:, None])


def _rotary_tables(pos):
    half = B_QK_DIM // 2
    inv = ROPE_BASE ** (-jnp.arange(half, dtype=F32) / half)
    ang = pos.astype(F32)[:, None] * inv[None, :]
    cos, sin = jnp.cos(ang), jnp.sin(ang)
    return jnp.concatenate([cos, cos], axis=-1), jnp.concatenate([-sin, sin], axis=-1)


def _gla_levels(c):
    return [c >> (l + 1) for l in range(int(math.log2(c)))]


def _gla_consts(c):
    levels = _gla_levels(c)
    rows = np.arange(c)
    mats = []
    for s in levels:
        ref = (rows // (2 * s)) * 2 * s + s - 1
        a = np.zeros((c, c), np.float32)
        for i in range(c):
            if i & s:
                a[i, ref[i] + 1:i + 1] = 1.0
            else:
                a[i, i + 1:ref[i] + 1] = 1.0
        mats.append(a)
    mats.append(np.tril(np.ones((c, c), np.float32)))
    mats.append(np.triu(np.ones((c, c), np.float32), 1))
    lv = np.full((c, c), -1, np.int32)
    for i in range(c):
        lv[i, i] = len(levels)
        for j in range(i):
            lv[i, j] = levels.index(1 << int(math.floor(math.log2(i ^ j))))
    return jnp.asarray(np.concatenate(mats, axis=0), BF16), jnp.asarray(lv)


def _gla_kernel(q_ref, k_ref, v_ref, r_ref, g_ref, s0_ref, a_ref, lv_ref, nw_ref,
                o_ref, s_out_ref, st_sc, *, c, n_chunks):
    t = pl.program_id(2)

    @pl.when(t == 0)
    def _():
        st_sc[...] = s0_ref[0, 0].T

    levels = _gla_levels(c)
    n_lv = len(levels)
    a = a_ref[...]
    lv = lv_ref[...]
    row = lax.broadcasted_iota(jnp.int32, (c, C_QK_DIM), 0)
    for ci in range(n_chunks):
        rs = slice(ci * c, (ci + 1) * c)
        q = q_ref[0, rs, :].astype(F32) * (C_QK_DIM ** -0.5)
        k = k_ref[0, rs, :].astype(F32)
        v = v_ref[0, rs, :]
        g = g_ref[0, rs, :]
        g_hi = g.astype(BF16)
        g_lo = (g - g_hi.astype(F32)).astype(BF16)
        x2 = _dot(a, jnp.concatenate([g_hi, g_lo], axis=1))
        x = x2[:, :C_QK_DIM] + x2[:, C_QK_DIM:]
        att = jnp.zeros((c, c), F32)
        for l, s in enumerate(levels):
            e = jnp.exp(x[l * c:(l + 1) * c])
            up = (row & s) != 0
            mix = jnp.where(up, q, k) * e
            qt = jnp.where(up, mix, 0.0).astype(BF16)
            kt = jnp.where(up, 0.0, mix).astype(BF16)
            att = jnp.where(lv == l, _dot_nt(qt, kt), att)
        att = jnp.where(lv == n_lv, _dot_nt(q.astype(BF16), k.astype(BF16)), att)
        b = x[n_lv * c:(n_lv + 1) * c]
        rem = x[(n_lv + 1) * c:(n_lv + 2) * c]
        st = st_sc[...]
        o = _dot(att.astype(BF16), v) + _dot_nt((q * jnp.exp(b)).astype(BF16), st.astype(BF16))
        kd = (k * jnp.exp(rem)).astype(BF16)
        st_sc[...] = st * jnp.exp(b[c - 1:c, :]) + _dot_tn(v, kd)
        r = r_ref[0, rs, :].astype(F32)
        o_ref[rs, :] = (_head_rms(o) * nw_ref[...] * _silu(r)).astype(BF16)

    @pl.when(t == pl.num_programs(2) - 1)
    def _():
        s_out_ref[0, 0] = st_sc[...].T


def _gla(z3, log_a3, s0, a_stack, lv, norm_w, c, tb):
    bsz, t, _ = z3.shape
    nt = t // tb
    dk, dv = C_QK_DIM, C_V_DIM
    return pl.pallas_call(
        functools.partial(_gla_kernel, c=c, n_chunks=tb // c),
        grid=(bsz, C_HEADS, nt),
        in_specs=[pl.BlockSpec((1, tb, dk), lambda b, h, i: (b, i, h)),
                  pl.BlockSpec((1, tb, dk), lambda b, h, i: (b, i, C_HEADS + h)),
                  pl.BlockSpec((1, tb, dv), lambda b, h, i: (b, i, C_HEADS + h)),
                  pl.BlockSpec((1, tb, dv), lambda b, h, i: (b, i, 2 * C_HEADS + h)),
                  pl.BlockSpec((1, tb, dk), lambda b, h, i: (b, i, h)),
                  pl.BlockSpec((1, 1, dk, dv), lambda b, h, i: (b, h, 0, 0)),
                  pl.BlockSpec(a_stack.shape, lambda b, h, i: (0, 0)),
                  pl.BlockSpec((c, c), lambda b, h, i: (0, 0)),
                  pl.BlockSpec((1, dv), lambda b, h, i: (0, 0))],
        out_specs=[pl.BlockSpec((tb, dv), lambda b, h, i: (b * nt + i, h)),
                   pl.BlockSpec((1, 1, dk, dv), lambda b, h, i: (b, h, 0, 0))],
        out_shape=[jax.ShapeDtypeStruct((bsz * t, C_HEADS * dv), BF16),
                   jax.ShapeDtypeStruct((bsz, C_HEADS, dk, dv), F32)],
        scratch_shapes=[pltpu.VMEM((dv, dk), F32)],
        compiler_params=_params(3),
    )(z3, z3, z3, z3, log_a3, s0, a_stack, lv, norm_w.reshape(1, dv))


def _cross_kernel(x_ref, g_ref, wq_ref, mk_ref, mv_ref, wo_ref, o_ref):
    x = x_ref[...]
    h = _rms(x, g_ref[...]).astype(BF16)
    q = (_dot(h, wq_ref[...]) * (M_HEAD_DIM ** -0.5)).astype(BF16)
    mk = mk_ref[0].astype(BF16)
    mv = mv_ref[0].astype(BF16)
    outs = []
    for hd in range(M_HEADS):
        cs = slice(hd * M_HEAD_DIM, (hd + 1) * M_HEAD_DIM)
        s = _dot_nt(q[:, cs], mk[:, cs])
        p = jnp.exp(s - jnp.max(s, axis=-1, keepdims=True))
        l = jnp.sum(p, axis=-1, keepdims=True)
        outs.append((_dot(p.astype(BF16), mv[:, cs]) / l).astype(BF16))
    o_ref[...] = x + _dot(jnp.concatenate(outs, axis=1), wo_ref[...])


def _cross(x2d, gamma, wq, mk3, mv3, wo, t, tm):
    n = x2d.shape[0]
    nt = t // tm
    mem_len, width = mk3.shape[1], mk3.shape[2]
    return pl.pallas_call(
        _cross_kernel,
        grid=(n // t, nt),
        in_specs=[pl.BlockSpec((tm, D_MODEL), lambda b, i: (b * nt + i, 0)),
                  pl.BlockSpec((1, D_MODEL), lambda b, i: (0, 0)),
                  pl.BlockSpec((D_MODEL, width), lambda b, i: (0, 0)),
                  pl.BlockSpec((1, mem_len, width), lambda b, i: (b, 0, 0)),
                  pl.BlockSpec((1, mem_len, width), lambda b, i: (b, 0, 0)),
                  pl.BlockSpec((width, D_MODEL), lambda b, i: (0, 0))],
        out_specs=pl.BlockSpec((tm, D_MODEL), lambda b, i: (b * nt + i, 0)),
        out_shape=jax.ShapeDtypeStruct((n, D_MODEL), F32),
        compiler_params=_params(2),
    )(x2d, gamma.reshape(1, D_MODEL), wq, mk3, mv3, wo)


def _ffn_kernel(x_ref, g_ref, wg_ref, wu_ref, cw_ref, cb_ref, prev_ref, wd_ref, *rest,
                tm, final_norm):
    if final_norm:
        gf_ref, o_ref, tail_ref, carry = rest
    else:
        o_ref, tail_ref, carry = rest
    t = pl.program_id(1)
    lo = SUBLANES - 2

    @pl.when(t == 0)
    def _():
        carry[lo:SUBLANES, :] = prev_ref[0]

    x = x_ref[...]
    h = _rms(x, g_ref[...]).astype(BF16)
    row = lax.broadcasted_iota(jnp.int32, (SUBLANES, FF_CHUNK), 0)
    acc = x
    n_chunks = D_FF // FF_CHUNK
    chunk = lambda c: slice(c * FF_CHUNK, (c + 1) * FF_CHUNK)
    nxt = (_dot(h, wg_ref[:, chunk(0)]), _dot(h, wu_ref[:, chunk(0)]))
    for c in range(n_chunks):
        cs = chunk(c)
        gate, up = nxt
        if c + 1 < n_chunks:
            nxt = (_dot(h, wg_ref[:, chunk(c + 1)]), _dot(h, wu_ref[:, chunk(c + 1)]))
        p1 = carry[SUBLANES - 1:SUBLANES, cs]
        p2 = carry[lo:lo + 1, cs]
        r1 = pltpu.roll(gate, 1, 0)
        r2 = pltpu.roll(gate, 2, 0)
        h1 = jnp.where(row == 0, p1, r1[:SUBLANES])
        h2 = jnp.where(row == 0, p2, jnp.where(row == 1, p1, r2[:SUBLANES]))
        g1 = jnp.concatenate([h1, r1[SUBLANES:]], axis=0)
        g2 = jnp.concatenate([h2, r2[SUBLANES:]], axis=0)
        carry[:, cs] = gate[tm - SUBLANES:tm]
        conv = cb_ref[:, cs] + cw_ref[0:1, cs] * g2
        conv = conv + cw_ref[1:2, cs] * g1
        conv = conv + cw_ref[2:3, cs] * gate
        act = (_gelu_tanh(conv) * up).astype(BF16)
        acc = acc + _dot(act, wd_ref[cs, :])
    if final_norm:
        o_ref[...] = _rms(acc, gf_ref[...])
    else:
        o_ref[...] = acc

    @pl.when(t == pl.num_programs(1) - 1)
    def _():
        tail_ref[0] = carry[lo:SUBLANES, :]


def _ffn(x2d, gamma, wg, wu, conv_w, conv_b, prev, wd, t, tm, final_gamma=None):
    n = x2d.shape[0]
    nt = t // tm
    bsz = n // t
    const = lambda b, i: (0, 0)
    resident = dict(pipeline_mode=pl.Buffered(1))
    in_specs = [pl.BlockSpec((tm, D_MODEL), lambda b, i: (b * nt + i, 0)),
                pl.BlockSpec((1, D_MODEL), const),
                pl.BlockSpec((D_MODEL, D_FF), const, **resident),
                pl.BlockSpec((D_MODEL, D_FF), const, **resident),
                pl.BlockSpec((3, D_FF), const),
                pl.BlockSpec((1, D_FF), const),
                pl.BlockSpec((1, 2, D_FF), lambda b, i: (b, 0, 0)),
                pl.BlockSpec((D_FF, D_MODEL), const, **resident)]
    args = [x2d, gamma.reshape(1, D_MODEL), wg, wu, conv_w, conv_b.reshape(1, D_FF), prev, wd]
    if final_gamma is not None:
        in_specs.append(pl.BlockSpec((1, D_MODEL), const))
        args.append(final_gamma.reshape(1, D_MODEL))
    return pl.pallas_call(
        functools.partial(_ffn_kernel, tm=tm, final_norm=final_gamma is not None),
        grid=(bsz, nt),
        in_specs=in_specs,
        out_specs=[pl.BlockSpec((tm, D_MODEL), lambda b, i: (b * nt + i, 0)),
                   pl.BlockSpec((1, 2, D_FF), lambda b, i: (b, 0, 0))],
        out_shape=[jax.ShapeDtypeStruct((n, D_MODEL), F32),
                   jax.ShapeDtypeStruct((bsz, 2, D_FF), F32)],
        scratch_shapes=[pltpu.VMEM((SUBLANES, D_FF), F32)],
        compiler_params=_params(2),
    )(*args)


def _rel_bucket(rel):
    nb = REL_BUCKETS // 2
    max_exact = nb // 2
    n = jnp.abs(rel)
    nf = jnp.maximum(n, 1).astype(F32)
    large = max_exact + (jnp.log(nf / max_exact) / math.log(REL_MAX_DIST / max_exact)
                         * (nb - max_exact)).astype(jnp.int32)
    large = jnp.minimum(large, nb - 1)
    return jnp.where(rel > 0, nb, 0) + jnp.where(n < max_exact, n, large)


def _bias_tiles(rel_bias, tq, nd):
    period = 2 * tq
    n = np.arange(period)
    rel = np.stack([np.where(n < tq, (d - (nd - 1)) * tq - n, (d - (nd - 1)) * tq + period - n)
                    for d in range(nd)]).astype(np.int32)
    w = jnp.transpose(rel_bias[_rel_bucket(jnp.asarray(rel))], (2, 0, 1)).astype(F32)
    flat = jnp.tile(w, (1, 1, tq))[:, :, :tq * (period - 1)]
    return flat.reshape(A_HEADS, nd, tq, period - 1)[:, :, :, :tq]


def _bias_rows(rel_bias, t, past):
    rel = (np.arange(past + t)[None, :] - (past + np.arange(t))[:, None]).astype(np.int32)
    bias = jnp.transpose(rel_bias[_rel_bucket(jnp.asarray(rel))], (2, 0, 1)).astype(F32)
    return bias[:, :, :past], bias[:, :, past:]


def _forward(x, past_k, past_v, ret_state, gla_state, conv_prev, mem_k, mem_v, p, w):
    bsz, t, _ = x.shape
    n = bsz * t
    past = 0 if past_k is None else past_k.shape[2]
    tm = min(512, n)
    tseq = min(512, t)
    tffn = min(256, t)
    tq = min(256, t)
    c_ret = min(256, t)
    c_gla = min(256, t)
    tb_gla = c_gla
    x2 = x.reshape(n, D_MODEL)
    pos = past + jnp.arange(t, dtype=jnp.int32)
    new_conv = []

    z, ak, av = _in_even(x2, p["ln_mix"][0], w["w_in_even"], tm)
    z3 = z.reshape(bsz, t, EVEN_IN)
    lam_init = 0.8 - 0.6 * math.exp(-0.3 * 0)
    lam = (jnp.exp(jnp.sum(p["diff_lq1"][0].astype(F32) * p["diff_lk1"][0].astype(F32)))
           - jnp.exp(jnp.sum(p["diff_lq2"][0].astype(F32) * p["diff_lk2"][0].astype(F32)))
           + lam_init).reshape(1, 1).astype(F32)
    if past == 0:
        o_a = _diff_attn(lam, z3, _bias_tiles(p["rel_bias"], tq, t // tq), p["diff_subln"][0],
                         tq, 1.0 - lam_init)
    else:
        width = A_HEADS * A_V_DIM
        bias_past, bias_new = _bias_rows(p["rel_bias"], t, past)
        o_a = _diff_attn_cached(lam, z3, past_k[0].reshape(bsz, past, width),
                                past_v[0].reshape(bsz, past, width), bias_past, bias_new,
                                p["diff_subln"][0], 1.0 - lam_init)
    cos, sin = _rotary_tables(pos)
    o_b, s_ret = _retention(z3, cos, sin, _retention_consts(c_ret), ret_state[0], c_ret)
    x2 = _mm_res(x2, [o_a, o_b], [w["w_out_even_a"], w["w_out_even_b"]], tm)
    x2 = _cross(x2, p["ln_cross"][0], w["w_cq"][0], mem_k[0], mem_v[0], w["w_co"][0], t, tseq)
    x2, tail = _ffn(x2, p["ln_ffn"][0], w["w_ffn_gate"][0], w["w_ffn_up"][0], p["ffn_conv_w"][0],
                    p["ffn_conv_b"][0], conv_prev[0], w["w_ffn_down"][0], t, tffn)
    new_conv.append(tail)

    zc, log_a = _in_odd(x2, p["ln_mix"][1], w["w_in_odd"], w["w_gate_lr"], p["b_gate"][0], tm)
    a_stack, lv = _gla_consts(c_gla)
    o_c, s_gla = _gla(zc.reshape(bsz, t, ODD_Z), log_a.reshape(bsz, t, C_HEADS * C_QK_DIM),
                      gla_state[0], a_stack, lv, p["gla_norm"][0], c_gla, tb_gla)
    x2 = _mm_res(x2, [o_c], [w["w_out_odd"]], tm)
    x2 = _cross(x2, p["ln_cross"][1], w["w_cq"][1], mem_k[1], mem_v[1], w["w_co"][1], t, tseq)
    y, tail = _ffn(x2, p["ln_ffn"][1], w["w_ffn_gate"][1], w["w_ffn_up"][1], p["ffn_conv_w"][1],
                   p["ffn_conv_b"][1], conv_prev[1], w["w_ffn_down"][1], t, tffn,
                   final_gamma=p["ln_final"])
    new_conv.append(tail)

    return (y.reshape(bsz, t, D_MODEL),
            ak.reshape(1, bsz, t, A_HEADS, A_V_DIM), av.reshape(1, bsz, t, A_HEADS, A_V_DIM),
            s_ret[None], s_gla[None], jnp.stack(new_conv))


def kernel(x_prompt, x_sample, cache_diff_k, cache_diff_v, state_retention, state_gla, cache_ffn_conv, cache_mem_k, cache_mem_v, mem_prompt, ln_mix, ln_cross, ln_ffn, ln_mem, ln_final, w_in_even, w_out_even, diff_lq1, diff_lk1, diff_lq2, diff_lk2, diff_subln, rel_bias, w_in_odd, w_gate_lr, b_gate, gla_norm, w_out_odd, w_cq, w_ck, w_cv, w_co, w_ffn_gate, w_ffn_up, ffn_conv_w, ffn_conv_b, w_ffn_down):
    p = dict(ln_mix=ln_mix, ln_cross=ln_cross, ln_ffn=ln_ffn, ln_final=ln_final,
             diff_lq1=diff_lq1, diff_lk1=diff_lk1, diff_lq2=diff_lq2, diff_lk2=diff_lk2,
             diff_subln=diff_subln, rel_bias=rel_bias, b_gate=b_gate, gla_norm=gla_norm,
             ffn_conv_w=ffn_conv_w, ffn_conv_b=ffn_conv_b)
    a_v = A_HEADS * A_V_DIM
    w = dict(
        w_in_even=w_in_even[0].astype(BF16),
        w_out_even_a=w_out_even[0, :a_v].astype(BF16),
        w_out_even_b=w_out_even[0, a_v:].astype(BF16),
        w_in_odd=jnp.pad(w_in_odd[0], ((0, 0), (0, ODD_IN_PAD - w_in_odd.shape[2]))).astype(BF16),
        w_gate_lr=jnp.pad(w_gate_lr[0], ((0, LANES - C_GATE_RANK), (0, 0))).astype(BF16),
        w_out_odd=w_out_odd[0].astype(BF16),
        w_cq=w_cq.astype(BF16), w_co=w_co.astype(BF16),
        w_ffn_gate=w_ffn_gate.astype(BF16), w_ffn_up=w_ffn_up.astype(BF16),
        w_ffn_down=w_ffn_down.astype(BF16))

    bp, mem_len, _ = mem_prompt.shape
    bs = x_sample.shape[0]
    m_width = M_HEADS * M_HEAD_DIM
    mem2 = mem_prompt.reshape(bp * mem_len, D_MODEL)
    depth = ln_mem.shape[0]
    mk_p = [_norm_mm(mem2, ln_mem[l], w_ck[l].astype(BF16), 512) for l in range(depth)]
    mv_p = [_norm_mm(mem2, ln_mem[l], w_cv[l].astype(BF16), 512) for l in range(depth)]
    mem_k_p = jnp.stack(mk_p).reshape(depth, bp, mem_len, M_HEADS, M_HEAD_DIM)
    mem_v_p = jnp.stack(mv_p).reshape(depth, bp, mem_len, M_HEADS, M_HEAD_DIM)

    dt = x_prompt.dtype
    zero_ret = jnp.zeros((1, bp, B_HEADS, B_QK_DIM, B_QK_DIM), dt)
    zero_gla = jnp.zeros((1, bp, C_HEADS, C_QK_DIM, C_V_DIM), dt)
    zero_conv = jnp.zeros((depth, bp, 2, D_FF), dt)
    y_p, dk_p, dv_p, ret_p, gla_p, conv_p = _forward(
        x_prompt, None, None, zero_ret, zero_gla, zero_conv,
        [m.reshape(bp, mem_len, m_width) for m in mk_p],
        [m.reshape(bp, mem_len, m_width) for m in mv_p], p, w)
    y_s, dk_s, dv_s, ret_s, gla_s, conv_s = _forward(
        x_sample, cache_diff_k, cache_diff_v, state_retention, state_gla, cache_ffn_conv,
        cache_mem_k.reshape(depth, bs, mem_len, m_width),
        cache_mem_v.reshape(depth, bs, mem_len, m_width), p, w)
    return (y_p, y_s, dk_p, dv_p, ret_p, gla_p, conv_p, mem_k_p, mem_v_p,
            dk_s, dv_s, ret_s, gla_s, conv_s)
```

```python
import functools
import math

import numpy as np
import jax
import jax.numpy as jnp
from jax import lax
from jax.experimental import pallas as pl
from jax.experimental.pallas import tpu as pltpu

F32 = jnp.float32
BF16 = jnp.bfloat16

D_MODEL = 1024
CHUNK = 64
A_HEADS = 4
A_QK_DIM = 64
A_V_DIM = 128
B_HEADS = 4
B_QK_DIM = 128
C_HEADS = 4
C_QK_DIM = 128
C_V_DIM = 256
C_GATE_RANK = 16
C_GATE_TAU = 16.0
M_HEADS = 4
M_HEAD_DIM = 256
REL_BUCKETS = 32
REL_MAX_DIST = 128
D_FF = 2816
ROPE_BASE = 10000.0
EPS = 1e-6
NEG_INF = -1e30

EVEN_IN = 3584
ODD_Z = 3072
LANES = 128
SUBLANES = 8
ODD_IN_PAD = ODD_Z + LANES
FF_CHUNK = 256
V7X_VMEM_LIMIT_BYTES = 56 * 1024 * 1024


def _params(n_axes):
    return pltpu.CompilerParams(dimension_semantics=("arbitrary",) * n_axes,
                                vmem_limit_bytes=V7X_VMEM_LIMIT_BYTES)


def _dot(a, b):
    return jnp.dot(a, b, preferred_element_type=F32)


def _dot_nt(a, b):
    return lax.dot_general(a, b, (((1,), (1,)), ((), ())), preferred_element_type=F32)


def _dot_tn(a, b):
    return lax.dot_general(a, b, (((0,), (0,)), ((), ())), preferred_element_type=F32)


def _rms(x, g):
    return x * lax.rsqrt(jnp.mean(x * x, axis=-1, keepdims=True) + EPS) * g


def _head_rms(x):
    return x * lax.rsqrt(jnp.mean(x * x, axis=-1, keepdims=True) + EPS)


def _silu(x):
    return x * (1.0 / (1.0 + jnp.exp(-x)))


def _gelu_tanh(x):
    c0 = math.sqrt(2.0 / math.pi)
    return x * (0.5 + 0.5 * jnp.tanh(x * (c0 + (c0 * 0.044715) * (x * x))))


def _log_sigmoid(x):
    return jnp.minimum(x, 0.0) - jnp.log1p(jnp.exp(-jnp.abs(x)))


def _store_head_major(o_ref, val, tm, n_heads, head_dim):
    tiles = head_dim // LANES
    group = n_heads * tiles
    for hd in range(n_heads):
        for part in range(tiles):
            col = hd * head_dim + part * LANES
            o_ref[pl.ds(part * n_heads + hd, tm, stride=group), :] = val[:, col:col + LANES]


def _in_even_kernel(x_ref, g_ref, w_ref, z_ref, k_ref, v_ref, *, tm):
    h = _rms(x_ref[...], g_ref[...]).astype(BF16)
    width = A_HEADS * A_V_DIM
    for c in range(EVEN_IN // width):
        zc = _dot(h, w_ref[:, c * width:(c + 1) * width])
        z_ref[:, c * width:(c + 1) * width] = zc.astype(BF16)
        if c == 1:
            _store_head_major(k_ref, zc, tm, A_HEADS, A_V_DIM)
        if c == 2:
            _store_head_major(v_ref, zc, tm, A_HEADS, A_V_DIM)


def _in_even(x2d, gamma, w, tm):
    n = x2d.shape[0]
    return pl.pallas_call(
        functools.partial(_in_even_kernel, tm=tm),
        grid=(n // tm,),
        in_specs=[pl.BlockSpec((tm, D_MODEL), lambda i: (i, 0)),
                  pl.BlockSpec((1, D_MODEL), lambda i: (0, 0)),
                  pl.BlockSpec((D_MODEL, EVEN_IN), lambda i: (0, 0))],
        out_specs=[pl.BlockSpec((tm, EVEN_IN), lambda i: (i, 0)),
                   pl.BlockSpec((tm * A_HEADS, A_V_DIM), lambda i: (i, 0)),
                   pl.BlockSpec((tm * A_HEADS, A_V_DIM), lambda i: (i, 0))],
        out_shape=[jax.ShapeDtypeStruct((n, EVEN_IN), BF16),
                   jax.ShapeDtypeStruct((n * A_HEADS, A_V_DIM), F32),
                   jax.ShapeDtypeStruct((n * A_HEADS, A_V_DIM), F32)],
        compiler_params=_params(1),
    )(x2d, gamma.reshape(1, D_MODEL), w)


def _in_odd_kernel(x_ref, g_ref, w_ref, wlr_ref, bg_ref, z_ref, la_ref):
    h = _rms(x_ref[...], g_ref[...]).astype(BF16)
    width = 512
    for c in range(ODD_Z // width):
        z_ref[:, c * width:(c + 1) * width] = _dot(h, w_ref[:, c * width:(c + 1) * width]).astype(BF16)
    ca = _dot(h, w_ref[:, ODD_Z:ODD_IN_PAD]).astype(BF16)
    pre = _dot(ca, wlr_ref[...]) + bg_ref[...]
    la_ref[...] = _log_sigmoid(pre) / C_GATE_TAU


def _in_odd(x2d, gamma, w_pad, wlr_pad, b_gate, tm):
    n = x2d.shape[0]
    qk = C_HEADS * C_QK_DIM
    return pl.pallas_call(
        _in_odd_kernel,
        grid=(n // tm,),
        in_specs=[pl.BlockSpec((tm, D_MODEL), lambda i: (i, 0)),
                  pl.BlockSpec((1, D_MODEL), lambda i: (0, 0)),
                  pl.BlockSpec((D_MODEL, ODD_IN_PAD), lambda i: (0, 0)),
                  pl.BlockSpec((LANES, qk), lambda i: (0, 0)),
                  pl.BlockSpec((1, qk), lambda i: (0, 0))],
        out_specs=[pl.BlockSpec((tm, ODD_Z), lambda i: (i, 0)),
                   pl.BlockSpec((tm, qk), lambda i: (i, 0))],
        out_shape=[jax.ShapeDtypeStruct((n, ODD_Z), BF16),
                   jax.ShapeDtypeStruct((n, qk), F32)],
        compiler_params=_params(1),
    )(x2d, gamma.reshape(1, D_MODEL), w_pad, wlr_pad, b_gate.reshape(1, qk))


def _norm_mm_kernel(x_ref, g_ref, w_ref, o_ref):
    h = _rms(x_ref[...], g_ref[...]).astype(BF16)
    o_ref[...] = _dot(h, w_ref[...])


def _norm_mm(x2d, gamma, w, tm):
    n, nout = x2d.shape[0], w.shape[1]
    return pl.pallas_call(
        _norm_mm_kernel,
        grid=(n // tm,),
        in_specs=[pl.BlockSpec((tm, D_MODEL), lambda i: (i, 0)),
                  pl.BlockSpec((1, D_MODEL), lambda i: (0, 0)),
                  pl.BlockSpec((D_MODEL, nout), lambda i: (0, 0))],
        out_specs=pl.BlockSpec((tm, nout), lambda i: (i, 0)),
        out_shape=jax.ShapeDtypeStruct((n, nout), F32),
        compiler_params=_params(1),
    )(x2d, gamma.reshape(1, D_MODEL), w)


def _mm_res_kernel(*refs, n_in):
    res_ref = refs[0]
    a_refs = refs[1:1 + n_in]
    w_refs = refs[1 + n_in:1 + 2 * n_in]
    o_ref = refs[1 + 2 * n_in]
    acc = res_ref[...]
    for a_ref, w_ref in zip(a_refs, w_refs):
        acc = acc + _dot(a_ref[...], w_ref[...])
    o_ref[...] = acc


def _mm_res(res, a_list, w_list, tm):
    n = res.shape[0]
    n_in = len(a_list)
    in_specs = [pl.BlockSpec((tm, D_MODEL), lambda i: (i, 0))]
    in_specs += [pl.BlockSpec((tm, a.shape[1]), lambda i: (i, 0)) for a in a_list]
    in_specs += [pl.BlockSpec(w.shape, lambda i: (0, 0)) for w in w_list]
    return pl.pallas_call(
        functools.partial(_mm_res_kernel, n_in=n_in),
        grid=(n // tm,),
        in_specs=in_specs,
        out_specs=pl.BlockSpec((tm, D_MODEL), lambda i: (i, 0)),
        out_shape=jax.ShapeDtypeStruct((n, D_MODEL), F32),
        compiler_params=_params(1),
    )(res, *a_list, *w_list)


def _diff_attn_kernel(lam_ref, q_ref, k_ref, v_ref, bias_ref, subln_ref, o_ref,
                      vt_sc, s_sc, acc_sc, *, tq, nd, out_scale):
    i = pl.program_id(1)
    dv = A_V_DIM
    chains = [(h, c) for h in range(A_HEADS) for c in range(2)]
    n_ch = len(chains)

    @pl.when(i == 0)
    def _():
        for h in range(A_HEADS):
            for jj in range(nd):
                vt_sc[h, jj] = v_ref[0, jj * tq:(jj + 1) * tq, h * dv:(h + 1) * dv].T

    row = lax.broadcasted_iota(jnp.int32, (dv, tq), 0)
    qzt = []
    for h in range(A_HEADS):
        qt = (q_ref[0, :, h * dv:(h + 1) * dv] * (A_QK_DIM ** -0.5)).T
        for c in range(2):
            qzt.append(jnp.where((row >= A_QK_DIM) == (c == 1), qt, jnp.zeros_like(qt)))

    def scores(j, n):
        h = chains[n][0]
        start = pl.multiple_of(j * tq, tq)
        return (_dot(k_ref[0, pl.ds(start, tq), h * dv:(h + 1) * dv], qzt[n])
                + bias_ref[h, j - i + (nd - 1)])

    def pass1(j, ms):
        new = []
        for n in range(n_ch):
            s = scores(j, n)
            s_sc[n, j] = s
            new.append(jnp.maximum(ms[n], jnp.max(s, axis=0, keepdims=True)))
        return tuple(new)

    ms = lax.fori_loop(0, i, pass1, tuple(jnp.full((1, tq), NEG_INF, F32) for _ in range(n_ch)))
    kk = lax.broadcasted_iota(jnp.int32, (tq, tq), 0)
    qq = lax.broadcasted_iota(jnp.int32, (tq, tq), 1)
    allowed = (kk // CHUNK) <= (qq // CHUNK)
    m_fin = []
    for n in range(n_ch):
        s = jnp.where(allowed, scores(i, n), NEG_INF)
        s_sc[n, i] = s
        m_fin.append(jnp.maximum(ms[n], jnp.max(s, axis=0, keepdims=True)))

    acc_sc[...] = jnp.zeros(acc_sc.shape, F32)

    def pass2(j, ls):
        new = []
        p_next = jnp.exp(s_sc[0, j] - m_fin[0])
        for n, (h, c) in enumerate(chains):
            p = p_next
            if n + 1 < n_ch:
                p_next = jnp.exp(s_sc[n + 1, j] - m_fin[n + 1])
            new.append(ls[n] + jnp.sum(p, axis=0, keepdims=True))
            acc_sc[n] += _dot(vt_sc[h, j], p.astype(BF16))
        return tuple(new)

    ls = lax.fori_loop(0, i + 1, pass2, tuple(jnp.zeros((1, tq), F32) for _ in range(n_ch)))
    for h in range(A_HEADS):
        out0 = acc_sc[2 * h] / ls[2 * h]
        out1 = acc_sc[2 * h + 1] / ls[2 * h + 1]
        o = (out0 - lam_ref[0, 0] * out1).T
        o_ref[:, h * dv:(h + 1) * dv] = (_head_rms(o) * subln_ref[...] * out_scale).astype(BF16)


def _diff_attn(lam, z3, bias_tiles, subln, tq, out_scale):
    bsz, t, _ = z3.shape
    nd = t // tq
    width = A_HEADS * A_V_DIM
    return pl.pallas_call(
        functools.partial(_diff_attn_kernel, tq=tq, nd=nd, out_scale=out_scale),
        grid=(bsz, nd),
        in_specs=[pl.BlockSpec(memory_space=pltpu.SMEM),
                  pl.BlockSpec((1, tq, width), lambda b, i: (b, i, 0)),
                  pl.BlockSpec((1, t, width), lambda b, i: (b, 0, 1)),
                  pl.BlockSpec((1, t, width), lambda b, i: (b, 0, 2)),
                  pl.BlockSpec((A_HEADS, nd, tq, tq), lambda b, i: (0, 0, 0, 0),
                               pipeline_mode=pl.Buffered(1)),
                  pl.BlockSpec((1, A_V_DIM), lambda b, i: (0, 0))],
        out_specs=pl.BlockSpec((tq, width), lambda b, i: (b * nd + i, 0)),
        out_shape=jax.ShapeDtypeStruct((bsz * t, width), BF16),
        scratch_shapes=[pltpu.VMEM((A_HEADS, nd, A_V_DIM, tq), BF16),
                        pltpu.VMEM((2 * A_HEADS, nd, tq, tq), F32),
                        pltpu.VMEM((2 * A_HEADS, A_V_DIM, tq), F32)],
        compiler_params=_params(2),
    )(lam, z3, z3, z3, bias_tiles, subln.reshape(1, A_V_DIM))


def _diff_attn_cached_kernel(lam_ref, q_ref, kn_ref, vn_ref, kp_ref, vp_ref, bp_ref, bn_ref,
                             subln_ref, o_ref, *, t, past, out_scale):
    lane = lax.broadcasted_iota(jnp.int32, (t, A_V_DIM), 1)
    qpos_p = past + lax.broadcasted_iota(jnp.int32, (2 * t, past), 0) % t
    kpos_p = lax.broadcasted_iota(jnp.int32, (2 * t, past), 1)
    ok_p = (kpos_p // CHUNK) <= (qpos_p // CHUNK)
    qpos_n = past + lax.broadcasted_iota(jnp.int32, (2 * t, t), 0) % t
    kpos_n = past + lax.broadcasted_iota(jnp.int32, (2 * t, t), 1)
    ok_n = (kpos_n // CHUNK) <= (qpos_n // CHUNK)
    for h in range(A_HEADS):
        cs = slice(h * A_V_DIM, (h + 1) * A_V_DIM)
        q = q_ref[0, :, cs] * (A_QK_DIM ** -0.5)
        zero = jnp.zeros_like(q)
        qp = jnp.concatenate([jnp.where(lane < A_QK_DIM, q, zero),
                              jnp.where(lane >= A_QK_DIM, q, zero)], axis=0)
        kp = kp_ref[0, :, cs].astype(BF16)
        vp = vp_ref[0, :, cs].astype(BF16)
        kn = kn_ref[0, :, cs]
        vn = vn_ref[0, :, cs]
        bp = bp_ref[h]
        bn = bn_ref[h]
        sp = jnp.where(ok_p, _dot_nt(qp, kp) + jnp.concatenate([bp, bp], axis=0), NEG_INF)
        sn = jnp.where(ok_n, _dot_nt(qp, kn) + jnp.concatenate([bn, bn], axis=0), NEG_INF)
        m = jnp.maximum(jnp.max(sp, axis=-1, keepdims=True), jnp.max(sn, axis=-1, keepdims=True))
        pp = jnp.exp(sp - m)
        pn = jnp.exp(sn - m)
        l = jnp.sum(pp, axis=-1, keepdims=True) + jnp.sum(pn, axis=-1, keepdims=True)
        out = (_dot(pp.astype(BF16), vp) + _dot(pn.astype(BF16), vn)) / l
        o = out[:t] - lam_ref[0, 0] * out[t:]
        o_ref[:, cs] = (_head_rms(o) * subln_ref[...] * out_scale).astype(BF16)


def _diff_attn_cached(lam, z3, past_k, past_v, bias_past, bias_new, subln, out_scale):
    bsz, t, _ = z3.shape
    past = past_k.shape[1]
    width = A_HEADS * A_V_DIM
    return pl.pallas_call(
        functools.partial(_diff_attn_cached_kernel, t=t, past=past, out_scale=out_scale),
        grid=(bsz,),
        in_specs=[pl.BlockSpec(memory_space=pltpu.SMEM),
                  pl.BlockSpec((1, t, width), lambda b: (b, 0, 0)),
                  pl.BlockSpec((1, t, width), lambda b: (b, 0, 1)),
                  pl.BlockSpec((1, t, width), lambda b: (b, 0, 2)),
                  pl.BlockSpec((1, past, width), lambda b: (b, 0, 0)),
                  pl.BlockSpec((1, past, width), lambda b: (b, 0, 0)),
                  pl.BlockSpec((A_HEADS, t, past), lambda b: (0, 0, 0)),
                  pl.BlockSpec((A_HEADS, t, t), lambda b: (0, 0, 0)),
                  pl.BlockSpec((1, A_V_DIM), lambda b: (0, 0))],
        out_specs=pl.BlockSpec((t, width), lambda b: (b, 0)),
        out_shape=jax.ShapeDtypeStruct((bsz * t, width), BF16),
        compiler_params=_params(1),
    )(lam, z3, z3, z3, past_k, past_v, bias_past, bias_new, subln.reshape(1, A_V_DIM))


def _retention_kernel(q_ref, k_ref, v_ref, gt_ref, cos_ref, sin_ref, dec_ref, qd_ref, kd_ref,
                      cd_ref, s0_ref, o_ref, s_out_ref, s_sc):
    t = pl.program_id(1)

    @pl.when(t == 0)
    def _():
        s_sc[...] = s0_ref[0]

    d = B_QK_DIM
    half = d // 2
    cos = cos_ref[...]
    sin = sin_ref[...]
    for h in range(B_HEADS):
        cs = slice(h * d, (h + 1) * d)
        q = q_ref[0, :, cs].astype(F32)
        k = k_ref[0, :, cs].astype(F32)
        qr = (q * cos + pltpu.roll(q, half, 1) * sin) * (d ** -0.5)
        kr = k * cos + pltpu.roll(k, half, 1) * sin
        v = v_ref[0, :, cs]
        att = _dot_nt(qr.astype(BF16), kr.astype(BF16)) * dec_ref[h]
        s = s_sc[h]
        o = _dot(att.astype(BF16), v) + _dot((qr * qd_ref[h]).astype(BF16), s.astype(BF16))
        s_sc[h] = s * cd_ref[h] + _dot_tn((kr * kd_ref[h]).astype(BF16), v)
        gt = gt_ref[0, :, cs].astype(F32)
        o_ref[:, cs] = (_head_rms(o) * _silu(gt)).astype(BF16)

    @pl.when(t == pl.num_programs(1) - 1)
    def _():
        s_out_ref[0] = s_sc[...]


def _retention(z3, cos, sin, consts, s0, c):
    bsz, t, _ = z3.shape
    nt = t // c
    dec, qd, kd, cd = consts
    d = B_QK_DIM
    width = B_HEADS * d
    base = 3
    full3 = lambda b, i: (0, 0, 0)
    return pl.pallas_call(
        _retention_kernel,
        grid=(bsz, nt),
        in_specs=[pl.BlockSpec((1, c, width), lambda b, i: (b, i, base)),
                  pl.BlockSpec((1, c, width), lambda b, i: (b, i, base + 1)),
                  pl.BlockSpec((1, c, width), lambda b, i: (b, i, base + 2)),
                  pl.BlockSpec((1, c, width), lambda b, i: (b, i, base + 3)),
                  pl.BlockSpec((c, d), lambda b, i: (i, 0)),
                  pl.BlockSpec((c, d), lambda b, i: (i, 0)),
                  pl.BlockSpec((B_HEADS, c, c), full3),
                  pl.BlockSpec((B_HEADS, c, d), full3),
                  pl.BlockSpec((B_HEADS, c, d), full3),
                  pl.BlockSpec((B_HEADS, 1, d), full3),
                  pl.BlockSpec((1, B_HEADS, d, d), lambda b, i: (b, 0, 0, 0))],
        out_specs=[pl.BlockSpec((c, width), lambda b, i: (b * nt + i, 0)),
                   pl.BlockSpec((1, B_HEADS, d, d), lambda b, i: (b, 0, 0, 0))],
        out_shape=[jax.ShapeDtypeStruct((bsz * t, width), BF16),
                   jax.ShapeDtypeStruct((bsz, B_HEADS, d, d), F32)],
        scratch_shapes=[pltpu.VMEM((B_HEADS, d, d), F32)],
        compiler_params=_params(2),
    )(z3, z3, z3, z3, cos, sin, dec, qd, kd, cd, s0)


def _retention_consts(c):
    log_g = jnp.log1p(-jnp.exp2(-5.0 - jnp.arange(B_HEADS, dtype=F32)))
    idx = jnp.arange(c, dtype=F32)
    dist = idx[:, None] - idx[None, :]
    dec = jnp.where(dist >= 0, jnp.exp(jnp.maximum(dist, 0.0)[None] * log_g[:, None, None]), 0.0)
    qd = jnp.exp((idx[None, :] + 1.0) * log_g[:, None])
    kd = jnp.exp((c - 1.0 - idx)[None, :] * log_g[:, None])
    cd = jnp.exp(c * log_g)
    bc = lambda a: jnp.broadcast_to(a[..., None], a.shape + (B_QK_DIM,))
    return dec, bc(qd), bc(kd), bc(cd[:, None])


def _rotary_tables(pos):
    half = B_QK_DIM // 2
    inv = ROPE_BASE ** (-jnp.arange(half, dtype=F32) / half)
    ang = pos.astype(F32)[:, None] * inv[None, :]
    cos, sin = jnp.cos(ang), jnp.sin(ang)
    return jnp.concatenate([cos, cos], axis=-1), jnp.concatenate([-sin, sin], axis=-1)


def _gla_levels(c):
    return [c >> (l + 1) for l in range(int(math.log2(c)))]


def _gla_consts(c):
    levels = _gla_levels(c)
    rows = np.arange(c)
    mats = []
    for s in levels:
        ref = (rows // (2 * s)) * 2 * s + s - 1
        a = np.zeros((c, c), np.float32)
        for i in range(c):
            if i & s:
                a[i, ref[i] + 1:i + 1] = 1.0
            else:
                a[i, i + 1:ref[i] + 1] = 1.0
        mats.append(a)
    mats.append(np.tril(np.ones((c, c), np.float32)))
    mats.append(np.triu(np.ones((c, c), np.float32), 1))
    lv = np.full((c, c), -1, np.int32)
    for i in range(c):
        lv[i, i] = len(levels)
        for j in range(i):
            lv[i, j] = levels.index(1 << int(math.floor(math.log2(i ^ j))))
    return jnp.asarray(np.concatenate(mats, axis=0), BF16), jnp.asarray(lv)


def _gla_kernel(q_ref, k_ref, v_ref, r_ref, g_ref, s0_ref, a_ref, lv_ref, nw_ref,
                o_ref, s_out_ref, st_sc, *, c, n_chunks):
    t = pl.program_id(2)

    @pl.when(t == 0)
    def _():
        st_sc[...] = s0_ref[0, 0].T

    levels = _gla_levels(c)
    n_lv = len(levels)
    a = a_ref[...]
    lv = lv_ref[...]
    row = lax.broadcasted_iota(jnp.int32, (c, C_QK_DIM), 0)
    for ci in range(n_chunks):
        rs = slice(ci * c, (ci + 1) * c)
        q = q_ref[0, rs, :].astype(F32) * (C_QK_DIM ** -0.5)
        k = k_ref[0, rs, :].astype(F32)
        v = v_ref[0, rs, :]
        g = g_ref[0, rs, :]
        g_hi = g.astype(BF16)
        g_lo = (g - g_hi.astype(F32)).astype(BF16)
        x2 = _dot(a, jnp.concatenate([g_hi, g_lo], axis=1))
        x = x2[:, :C_QK_DIM] + x2[:, C_QK_DIM:]
        att = jnp.zeros((c, c), F32)
        pending = (n_lv, _dot_nt(q.astype(BF16), k.astype(BF16)))
        for l, s in enumerate(levels):
            e = jnp.exp(x[l * c:(l + 1) * c])
            up = (row & s) != 0
            mix = jnp.where(up, q, k) * e
            qt = jnp.where(up, mix, 0.0).astype(BF16)
            kt = jnp.where(up, 0.0, mix).astype(BF16)
            prod = _dot_nt(qt, kt)
            att = jnp.where(lv == pending[0], pending[1], att)
            pending = (l, prod)
        att = jnp.where(lv == pending[0], pending[1], att)
        b = x[n_lv * c:(n_lv + 1) * c]
        rem = x[(n_lv + 1) * c:(n_lv + 2) * c]
        st = st_sc[...]
        o = _dot(att.astype(BF16), v) + _dot_nt((q * jnp.exp(b)).astype(BF16), st.astype(BF16))
        kd = (k * jnp.exp(rem)).astype(BF16)
        st_sc[...] = st * jnp.exp(b[c - 1:c, :]) + _dot_tn(v, kd)
        r = r_ref[0, rs, :].astype(F32)
        o_ref[rs, :] = (_head_rms(o) * nw_ref[...] * _silu(r)).astype(BF16)

    @pl.when(t == pl.num_programs(2) - 1)
    def _():
        s_out_ref[0, 0] = st_sc[...].T


def _gla(z3, log_a3, s0, a_stack, lv, norm_w, c, tb):
    bsz, t, _ = z3.shape
    nt = t // tb
    dk, dv = C_QK_DIM, C_V_DIM
    return pl.pallas_call(
        functools.partial(_gla_kernel, c=c, n_chunks=tb // c),
        grid=(bsz, C_HEADS, nt),
        in_specs=[pl.BlockSpec((1, tb, dk), lambda b, h, i: (b, i, h)),
                  pl.BlockSpec((1, tb, dk), lambda b, h, i: (b, i, C_HEADS + h)),
                  pl.BlockSpec((1, tb, dv), lambda b, h, i: (b, i, C_HEADS + h)),
                  pl.BlockSpec((1, tb, dv), lambda b, h, i: (b, i, 2 * C_HEADS + h)),
                  pl.BlockSpec((1, tb, dk), lambda b, h, i: (b, i, h)),
                  pl.BlockSpec((1, 1, dk, dv), lambda b, h, i: (b, h, 0, 0)),
                  pl.BlockSpec(a_stack.shape, lambda b, h, i: (0, 0)),
                  pl.BlockSpec((c, c), lambda b, h, i: (0, 0)),
                  pl.BlockSpec((1, dv), lambda b, h, i: (0, 0))],
        out_specs=[pl.BlockSpec((tb, dv), lambda b, h, i: (b * nt + i, h)),
                   pl.BlockSpec((1, 1, dk, dv), lambda b, h, i: (b, h, 0, 0))],
        out_shape=[jax.ShapeDtypeStruct((bsz * t, C_HEADS * dv), BF16),
                   jax.ShapeDtypeStruct((bsz, C_HEADS, dk, dv), F32)],
        scratch_shapes=[pltpu.VMEM((dv, dk), F32)],
        compiler_params=_params(3),
    )(z3, z3, z3, z3, log_a3, s0, a_stack, lv, norm_w.reshape(1, dv))


def _cross_kernel(x_ref, g_ref, wq_ref, mk_ref, mv_ref, wo_ref, o_ref):
    x = x_ref[...]
    h = _rms(x, g_ref[...]).astype(BF16)
    q = (_dot(h, wq_ref[...]) * (M_HEAD_DIM ** -0.5)).astype(BF16)
    mk = mk_ref[0].astype(BF16)
    mv = mv_ref[0].astype(BF16)
    outs = []
    for hd in range(M_HEADS):
        cs = slice(hd * M_HEAD_DIM, (hd + 1) * M_HEAD_DIM)
        s = _dot_nt(q[:, cs], mk[:, cs])
        p = jnp.exp(s - jnp.max(s, axis=-1, keepdims=True))
        l = jnp.sum(p, axis=-1, keepdims=True)
        outs.append((_dot(p.astype(BF16), mv[:, cs]) / l).astype(BF16))
    o_ref[...] = x + _dot(jnp.concatenate(outs, axis=1), wo_ref[...])


def _cross(x2d, gamma, wq, mk3, mv3, wo, t, tm):
    n = x2d.shape[0]
    nt = t // tm
    mem_len, width = mk3.shape[1], mk3.shape[2]
    return pl.pallas_call(
        _cross_kernel,
        grid=(n // t, nt),
        in_specs=[pl.BlockSpec((tm, D_MODEL), lambda b, i: (b * nt + i, 0)),
                  pl.BlockSpec((1, D_MODEL), lambda b, i: (0, 0)),
                  pl.BlockSpec((D_MODEL, width), lambda b, i: (0, 0)),
                  pl.BlockSpec((1, mem_len, width), lambda b, i: (b, 0, 0)),
                  pl.BlockSpec((1, mem_len, width), lambda b, i: (b, 0, 0)),
                  pl.BlockSpec((width, D_MODEL), lambda b, i: (0, 0))],
        out_specs=pl.BlockSpec((tm, D_MODEL), lambda b, i: (b * nt + i, 0)),
        out_shape=jax.ShapeDtypeStruct((n, D_MODEL), F32),
        compiler_params=_params(2),
    )(x2d, gamma.reshape(1, D_MODEL), wq, mk3, mv3, wo)


def _ffn_kernel(x_ref, g_ref, wg_ref, wu_ref, cw_ref, cb_ref, prev_ref, wd_ref, *rest,
                tm, final_norm):
    if final_norm:
        gf_ref, o_ref, tail_ref, carry = rest
    else:
        o_ref, tail_ref, carry = rest
    t = pl.program_id(1)
    lo = SUBLANES - 2

    @pl.when(t == 0)
    def _():
        carry[lo:SUBLANES, :] = prev_ref[0]

    x = x_ref[...]
    h = _rms(x, g_ref[...]).astype(BF16)
    row = lax.broadcasted_iota(jnp.int32, (SUBLANES, FF_CHUNK), 0)
    acc = x
    n_chunks = D_FF // FF_CHUNK
    chunk = lambda c: slice(c * FF_CHUNK, (c + 1) * FF_CHUNK)
    nxt = (_dot(h, wg_ref[:, chunk(0)]), _dot(h, wu_ref[:, chunk(0)]))
    for c in range(n_chunks):
        cs = chunk(c)
        gate, up = nxt
        if c + 1 < n_chunks:
            nxt = (_dot(h, wg_ref[:, chunk(c + 1)]), _dot(h, wu_ref[:, chunk(c + 1)]))
        p1 = carry[SUBLANES - 1:SUBLANES, cs]
        p2 = carry[lo:lo + 1, cs]
        r1 = pltpu.roll(gate, 1, 0)
        r2 = pltpu.roll(gate, 2, 0)
        h1 = jnp.where(row == 0, p1, r1[:SUBLANES])
        h2 = jnp.where(row == 0, p2, jnp.where(row == 1, p1, r2[:SUBLANES]))
        g1 = jnp.concatenate([h1, r1[SUBLANES:]], axis=0)
        g2 = jnp.concatenate([h2, r2[SUBLANES:]], axis=0)
        carry[:, cs] = gate[tm - SUBLANES:tm]
        conv = cb_ref[:, cs] + cw_ref[0:1, cs] * g2
        conv = conv + cw_ref[1:2, cs] * g1
        conv = conv + cw_ref[2:3, cs] * gate
        act = (_gelu_tanh(conv) * up).astype(BF16)
        acc = acc + _dot(act, wd_ref[cs, :])
    if final_norm:
        o_ref[...] = _rms(acc, gf_ref[...])
    else:
        o_ref[...] = acc

    @pl.when(t == pl.num_programs(1) - 1)
    def _():
        tail_ref[0] = carry[lo:SUBLANES, :]


def _ffn(x2d, gamma, wg, wu, conv_w, conv_b, prev, wd, t, tm, final_gamma=None):
    n = x2d.shape[0]
    nt = t // tm
    bsz = n // t
    const = lambda b, i: (0, 0)
    resident = dict(pipeline_mode=pl.Buffered(1))
    in_specs = [pl.BlockSpec((tm, D_MODEL), lambda b, i: (b * nt + i, 0)),
                pl.BlockSpec((1, D_MODEL), const),
                pl.BlockSpec((D_MODEL, D_FF), const, **resident),
                pl.BlockSpec((D_MODEL, D_FF), const, **resident),
                pl.BlockSpec((3, D_FF), const),
                pl.BlockSpec((1, D_FF), const),
                pl.BlockSpec((1, 2, D_FF), lambda b, i: (b, 0, 0)),
                pl.BlockSpec((D_FF, D_MODEL), const, **resident)]
    args = [x2d, gamma.reshape(1, D_MODEL), wg, wu, conv_w, conv_b.reshape(1, D_FF), prev, wd]
    if final_gamma is not None:
        in_specs.append(pl.BlockSpec((1, D_MODEL), const))
        args.append(final_gamma.reshape(1, D_MODEL))
    return pl.pallas_call(
        functools.partial(_ffn_kernel, tm=tm, final_norm=final_gamma is not None),
        grid=(bsz, nt),
        in_specs=in_specs,
        out_specs=[pl.BlockSpec((tm, D_MODEL), lambda b, i: (b * nt + i, 0)),
                   pl.BlockSpec((1, 2, D_FF), lambda b, i: (b, 0, 0))],
        out_shape=[jax.ShapeDtypeStruct((n, D_MODEL), F32),
                   jax.ShapeDtypeStruct((bsz, 2, D_FF), F32)],
        scratch_shapes=[pltpu.VMEM((SUBLANES, D_FF), F32)],
        compiler_params=_params(2),
    )(*args)


def _rel_bucket(rel):
    nb = REL_BUCKETS // 2
    max_exact = nb // 2
    n = jnp.abs(rel)
    nf = jnp.maximum(n, 1).astype(F32)
    large = max_exact + (jnp.log(nf / max_exact) / math.log(REL_MAX_DIST / max_exact)
                         * (nb - max_exact)).astype(jnp.int32)
    large = jnp.minimum(large, nb - 1)
    return jnp.where(rel > 0, nb, 0) + jnp.where(n < max_exact, n, large)


def _bias_tiles(rel_bias, tq, nd):
    period = 2 * tq
    n = np.arange(period)
    rel = np.stack([np.where(n < tq, (d - (nd - 1)) * tq - n, (d - (nd - 1)) * tq + period - n)
                    for d in range(nd)]).astype(np.int32)
    w = jnp.transpose(rel_bias[_rel_bucket(jnp.asarray(rel))], (2, 0, 1)).astype(F32)
    flat = jnp.tile(w, (1, 1, tq))[:, :, :tq * (period - 1)]
    return flat.reshape(A_HEADS, nd, tq, period - 1)[:, :, :, :tq]


def _bias_rows(rel_bias, t, past):
    rel = (np.arange(past + t)[None, :] - (past + np.arange(t))[:, None]).astype(np.int32)
    bias = jnp.transpose(rel_bias[_rel_bucket(jnp.asarray(rel))], (2, 0, 1)).astype(F32)
    return bias[:, :, :past], bias[:, :, past:]


def _forward(x, past_k, past_v, ret_state, gla_state, conv_prev, mem_k, mem_v, p, w):
    bsz, t, _ = x.shape
    n = bsz * t
    past = 0 if past_k is None else past_k.shape[2]
    tm = min(512, n)
    tseq = min(512, t)
    tffn = min(256, t)
    tq = min(256, t)
    c_ret = min(256, t)
    c_gla = min(256, t)
    tb_gla = c_gla
    x2 = x.reshape(n, D_MODEL)
    pos = past + jnp.arange(t, dtype=jnp.int32)
    new_conv = []

    z, ak, av = _in_even(x2, p["ln_mix"][0], w["w_in_even"], tm)
    z3 = z.reshape(bsz, t, EVEN_IN)
    lam_init = 0.8 - 0.6 * math.exp(-0.3 * 0)
    lam = (jnp.exp(jnp.sum(p["diff_lq1"][0].astype(F32) * p["diff_lk1"][0].astype(F32)))
           - jnp.exp(jnp.sum(p["diff_lq2"][0].astype(F32) * p["diff_lk2"][0].astype(F32)))
           + lam_init).reshape(1, 1).astype(F32)
    if past == 0:
        o_a = _diff_attn(lam, z3, _bias_tiles(p["rel_bias"], tq, t // tq), p["diff_subln"][0],
                         tq, 1.0 - lam_init)
    else:
        width = A_HEADS * A_V_DIM
        bias_past, bias_new = _bias_rows(p["rel_bias"], t, past)
        o_a = _diff_attn_cached(lam, z3, past_k[0].reshape(bsz, past, width),
                                past_v[0].reshape(bsz, past, width), bias_past, bias_new,
                                p["diff_subln"][0], 1.0 - lam_init)
    cos, sin = _rotary_tables(pos)
    o_b, s_ret = _retention(z3, cos, sin, _retention_consts(c_ret), ret_state[0], c_ret)
    x2 = _mm_res(x2, [o_a, o_b], [w["w_out_even_a"], w["w_out_even_b"]], tm)
    x2 = _cross(x2, p["ln_cross"][0], w["w_cq"][0], mem_k[0], mem_v[0], w["w_co"][0], t, tseq)
    x2, tail = _ffn(x2, p["ln_ffn"][0], w["w_ffn_gate"][0], w["w_ffn_up"][0], p["ffn_conv_w"][0],
                    p["ffn_conv_b"][0], conv_prev[0], w["w_ffn_down"][0], t, tffn)
    new_conv.append(tail)

    zc, log_a = _in_odd(x2, p["ln_mix"][1], w["w_in_odd"], w["w_gate_lr"], p["b_gate"][0], tm)
    a_stack, lv = _gla_consts(c_gla)
    o_c, s_gla = _gla(zc.reshape(bsz, t, ODD_Z), log_a.reshape(bsz, t, C_HEADS * C_QK_DIM),
                      gla_state[0], a_stack, lv, p["gla_norm"][0], c_gla, tb_gla)
    x2 = _mm_res(x2, [o_c], [w["w_out_odd"]], tm)
    x2 = _cross(x2, p["ln_cross"][1], w["w_cq"][1], mem_k[1], mem_v[1], w["w_co"][1], t, tseq)
    y, tail = _ffn(x2, p["ln_ffn"][1], w["w_ffn_gate"][1], w["w_ffn_up"][1], p["ffn_conv_w"][1],
                   p["ffn_conv_b"][1], conv_prev[1], w["w_ffn_down"][1], t, tffn,
                   final_gamma=p["ln_final"])
    new_conv.append(tail)

    return (y.reshape(bsz, t, D_MODEL),
            ak.reshape(1, bsz, t, A_HEADS, A_V_DIM), av.reshape(1, bsz, t, A_HEADS, A_V_DIM),
            s_ret[None], s_gla[None], jnp.stack(new_conv))


def kernel(x_prompt, x_sample, cache_diff_k, cache_diff_v, state_retention, state_gla, cache_ffn_conv, cache_mem_k, cache_mem_v, mem_prompt, ln_mix, ln_cross, ln_ffn, ln_mem, ln_final, w_in_even, w_out_even, diff_lq1, diff_lk1, diff_lq2, diff_lk2, diff_subln, rel_bias, w_in_odd, w_gate_lr, b_gate, gla_norm, w_out_odd, w_cq, w_ck, w_cv, w_co, w_ffn_gate, w_ffn_up, ffn_conv_w, ffn_conv_b, w_ffn_down):
    p = dict(ln_mix=ln_mix, ln_cross=ln_cross, ln_ffn=ln_ffn, ln_final=ln_final,
             diff_lq1=diff_lq1, diff_lk1=diff_lk1, diff_lq2=diff_lq2, diff_lk2=diff_lk2,
             diff_subln=diff_subln, rel_bias=rel_bias, b_gate=b_gate, gla_norm=gla_norm,
             ffn_conv_w=ffn_conv_w, ffn_conv_b=ffn_conv_b)
    a_v = A_HEADS * A_V_DIM
    w = dict(
        w_in_even=w_in_even[0].astype(BF16),
        w_out_even_a=w_out_even[0, :a_v].astype(BF16),
        w_out_even_b=w_out_even[0, a_v:].astype(BF16),
        w_in_odd=jnp.pad(w_in_odd[0], ((0, 0), (0, ODD_IN_PAD - w_in_odd.shape[2]))).astype(BF16),
        w_gate_lr=jnp.pad(w_gate_lr[0], ((0, LANES - C_GATE_RANK), (0, 0))).astype(BF16),
        w_out_odd=w_out_odd[0].astype(BF16),
        w_cq=w_cq.astype(BF16), w_co=w_co.astype(BF16),
        w_ffn_gate=w_ffn_gate.astype(BF16), w_ffn_up=w_ffn_up.astype(BF16),
        w_ffn_down=w_ffn_down.astype(BF16))

    bp, mem_len, _ = mem_prompt.shape
    bs = x_sample.shape[0]
    m_width = M_HEADS * M_HEAD_DIM
    mem2 = mem_prompt.reshape(bp * mem_len, D_MODEL)
    depth = ln_mem.shape[0]
    mk_p = [_norm_mm(mem2, ln_mem[l], w_ck[l].astype(BF16), 512) for l in range(depth)]
    mv_p = [_norm_mm(mem2, ln_mem[l], w_cv[l].astype(BF16), 512) for l in range(depth)]
    mem_k_p = jnp.stack(mk_p).reshape(depth, bp, mem_len, M_HEADS, M_HEAD_DIM)
    mem_v_p = jnp.stack(mv_p).reshape(depth, bp, mem_len, M_HEADS, M_HEAD_DIM)

    dt = x_prompt.dtype
    zero_ret = jnp.zeros((1, bp, B_HEADS, B_QK_DIM, B_QK_DIM), dt)
    zero_gla = jnp.zeros((1, bp, C_HEADS, C_QK_DIM, C_V_DIM), dt)
    zero_conv = jnp.zeros((depth, bp, 2, D_FF), dt)
    y_p, dk_p, dv_p, ret_p, gla_p, conv_p = _forward(
        x_prompt, None, None, zero_ret, zero_gla, zero_conv,
        [m.reshape(bp, mem_len, m_width) for m in mk_p],
        [m.reshape(bp, mem_len, m_width) for m in mv_p], p, w)
    y_s, dk_s, dv_s, ret_s, gla_s, conv_s = _forward(
        x_sample, cache_diff_k, cache_diff_v, state_retention, state_gla, cache_ffn_conv,
        cache_mem_k.reshape(depth, bs, mem_len, m_width),
        cache_mem_v.reshape(depth, bs, mem_len, m_width), p, w)
    return (y_p, y_s, dk_p, dv_p, ret_p, gla_p, conv_p, mem_k_p, mem_v_p,
            dk_s, dv_s, ret_s, gla_s, conv_s)
```

```python
import functools
import math

import numpy as np
import jax
import jax.numpy as jnp
from jax import lax
from jax.experimental import pallas as pl
from jax.experimental.pallas import tpu as pltpu

F32 = jnp.float32
BF16 = jnp.bfloat16

D_MODEL = 1024
CHUNK = 64
A_HEADS = 4
A_QK_DIM = 64
A_V_DIM = 128
B_HEADS = 4
B_QK_DIM = 128
C_HEADS = 4
C_QK_DIM = 128
C_V_DIM = 256
C_GATE_RANK = 16
C_GATE_TAU = 16.0
M_HEADS = 4
M_HEAD_DIM = 256
REL_BUCKETS = 32
REL_MAX_DIST = 128
D_FF = 2816
ROPE_BASE = 10000.0
EPS = 1e-6
NEG_INF = -1e30

EVEN_IN = 3584
ODD_Z = 3072
LANES = 128
SUBLANES = 8
ODD_IN_PAD = ODD_Z + LANES
FF_CHUNK = 256
V7X_VMEM_LIMIT_BYTES = 56 * 1024 * 1024


def _params(n_axes):
    return pltpu.CompilerParams(dimension_semantics=("arbitrary",) * n_axes,
                                vmem_limit_bytes=V7X_VMEM_LIMIT_BYTES)


def _dot(a, b):
    return jnp.dot(a, b, preferred_element_type=F32)


def _dot_nt(a, b):
    return lax.dot_general(a, b, (((1,), (1,)), ((), ())), preferred_element_type=F32)


def _dot_tn(a, b):
    return lax.dot_general(a, b, (((0,), (0,)), ((), ())), preferred_element_type=F32)


def _rms(x, g):
    return x * lax.rsqrt(jnp.mean(x * x, axis=-1, keepdims=True) + EPS) * g


def _head_rms(x):
    return x * lax.rsqrt(jnp.mean(x * x, axis=-1, keepdims=True) + EPS)


def _silu(x):
    return x * (1.0 / (1.0 + jnp.exp(-x)))


def _gelu_tanh(x):
    c0 = math.sqrt(2.0 / math.pi)
    return x * (0.5 + 0.5 * jnp.tanh(x * (c0 + (c0 * 0.044715) * (x * x))))


def _log_sigmoid(x):
    return jnp.minimum(x, 0.0) - jnp.log1p(jnp.exp(-jnp.abs(x)))


def _store_head_major(o_ref, val, tm, n_heads, head_dim):
    tiles = head_dim // LANES
    group = n_heads * tiles
    for hd in range(n_heads):
        for part in range(tiles):
            col = hd * head_dim + part * LANES
            o_ref[pl.ds(part * n_heads + hd, tm, stride=group), :] = val[:, col:col + LANES]


def _in_even_kernel(x_ref, g_ref, w_ref, z_ref, k_ref, v_ref, *, tm):
    h = _rms(x_ref[...], g_ref[...]).astype(BF16)
    width = A_HEADS * A_V_DIM
    for c in range(EVEN_IN // width):
        zc = _dot(h, w_ref[:, c * width:(c + 1) * width])
        z_ref[:, c * width:(c + 1) * width] = zc.astype(BF16)
        if c == 1:
            _store_head_major(k_ref, zc, tm, A_HEADS, A_V_DIM)
        if c == 2:
            _store_head_major(v_ref, zc, tm, A_HEADS, A_V_DIM)


def _in_even(x2d, gamma, w, tm):
    n = x2d.shape[0]
    return pl.pallas_call(
        functools.partial(_in_even_kernel, tm=tm),
        grid=(n // tm,),
        in_specs=[pl.BlockSpec((tm, D_MODEL), lambda i: (i, 0)),
                  pl.BlockSpec((1, D_MODEL), lambda i: (0, 0)),
                  pl.BlockSpec((D_MODEL, EVEN_IN), lambda i: (0, 0))],
        out_specs=[pl.BlockSpec((tm, EVEN_IN), lambda i: (i, 0)),
                   pl.BlockSpec((tm * A_HEADS, A_V_DIM), lambda i: (i, 0)),
                   pl.BlockSpec((tm * A_HEADS, A_V_DIM), lambda i: (i, 0))],
        out_shape=[jax.ShapeDtypeStruct((n, EVEN_IN), BF16),
                   jax.ShapeDtypeStruct((n * A_HEADS, A_V_DIM), F32),
                   jax.ShapeDtypeStruct((n * A_HEADS, A_V_DIM), F32)],
        compiler_params=_params(1),
    )(x2d, gamma.reshape(1, D_MODEL), w)


def _in_odd_kernel(x_ref, g_ref, w_ref, wlr_ref, bg_ref, z_ref, la_ref):
    h = _rms(x_ref[...], g_ref[...]).astype(BF16)
    width = 512
    ca = _dot(h, w_ref[:, ODD_Z:ODD_IN_PAD]).astype(BF16)
    pre = _dot(ca, wlr_ref[...]) + bg_ref[...]
    la_ref[...] = _log_sigmoid(pre) / C_GATE_TAU
    for c in range(ODD_Z // width):
        z_ref[:, c * width:(c + 1) * width] = _dot(h, w_ref[:, c * width:(c + 1) * width]).astype(BF16)


def _in_odd(x2d, gamma, w_pad, wlr_pad, b_gate, tm):
    n = x2d.shape[0]
    qk = C_HEADS * C_QK_DIM
    return pl.pallas_call(
        _in_odd_kernel,
        grid=(n // tm,),
        in_specs=[pl.BlockSpec((tm, D_MODEL), lambda i: (i, 0)),
                  pl.BlockSpec((1, D_MODEL), lambda i: (0, 0)),
                  pl.BlockSpec((D_MODEL, ODD_IN_PAD), lambda i: (0, 0)),
                  pl.BlockSpec((LANES, qk), lambda i: (0, 0)),
                  pl.BlockSpec((1, qk), lambda i: (0, 0))],
        out_specs=[pl.BlockSpec((tm, ODD_Z), lambda i: (i, 0)),
                   pl.BlockSpec((tm, qk), lambda i: (i, 0))],
        out_shape=[jax.ShapeDtypeStruct((n, ODD_Z), BF16),
                   jax.ShapeDtypeStruct((n, qk), F32)],
        compiler_params=_params(1),
    )(x2d, gamma.reshape(1, D_MODEL), w_pad, wlr_pad, b_gate.reshape(1, qk))


def _mem_kv_kernel(x_ref, g_ref, wk_ref, wv_ref, k5_ref, v5_ref, kb_ref, vb_ref, *, seqs, mem_len):
    h = _rms(x_ref[...], g_ref[0]).astype(BF16)
    for w_ref, o5_ref, ob_ref in ((wk_ref, k5_ref, kb_ref), (wv_ref, v5_ref, vb_ref)):
        y = _dot(h, w_ref[0])
        ob_ref[0] = y.astype(BF16)
        for s in range(seqs):
            for hd in range(M_HEADS):
                o5_ref[0, s, :, hd, :] = y[s * mem_len:(s + 1) * mem_len,
                                           hd * M_HEAD_DIM:(hd + 1) * M_HEAD_DIM]


def _mem_kv(mem, ln_mem, wk, wv, seqs):
    bsz, mem_len, _ = mem.shape
    depth = ln_mem.shape[0]
    width = M_HEADS * M_HEAD_DIM
    n = bsz * mem_len
    tm = seqs * mem_len
    out5 = jax.ShapeDtypeStruct((depth, bsz, mem_len, M_HEADS, M_HEAD_DIM), F32)
    outb = jax.ShapeDtypeStruct((depth, n, width), BF16)
    spec5 = pl.BlockSpec((1, seqs, mem_len, M_HEADS, M_HEAD_DIM), lambda l, i: (l, i, 0, 0, 0))
    specb = pl.BlockSpec((1, tm, width), lambda l, i: (l, i, 0))
    wspec = pl.BlockSpec((1, D_MODEL, width), lambda l, i: (l, 0, 0))
    return pl.pallas_call(
        functools.partial(_mem_kv_kernel, seqs=seqs, mem_len=mem_len),
        grid=(depth, n // tm),
        in_specs=[pl.BlockSpec((tm, D_MODEL), lambda l, i: (i, 0)),
                  pl.BlockSpec((1, 1, D_MODEL), lambda l, i: (l, 0, 0)),
                  wspec, wspec],
        out_specs=[spec5, spec5, specb, specb],
        out_shape=[out5, out5, outb, outb],
        compiler_params=_params(2),
    )(mem.reshape(n, D_MODEL), ln_mem.reshape(depth, 1, D_MODEL), wk, wv)


def _diff_attn_kernel(lam_ref, q_ref, k_ref, v_ref, bias_ref, subln_ref, o_ref,
                      vt_sc, s_sc, acc_sc, *, tq, nd, out_scale):
    i = pl.program_id(1)
    dv = A_V_DIM
    chains = [(h, c) for h in range(A_HEADS) for c in range(2)]
    n_ch = len(chains)

    @pl.when(i == 0)
    def _():
        for h in range(A_HEADS):
            for jj in range(nd):
                vt_sc[h, jj] = v_ref[0, jj * tq:(jj + 1) * tq, h * dv:(h + 1) * dv].T

    row = lax.broadcasted_iota(jnp.int32, (dv, tq), 0)
    qzt = []
    for h in range(A_HEADS):
        qt = (q_ref[0, :, h * dv:(h + 1) * dv] * (A_QK_DIM ** -0.5)).T
        for c in range(2):
            qzt.append(jnp.where((row >= A_QK_DIM) == (c == 1), qt, jnp.zeros_like(qt)))

    def scores(j, n):
        h = chains[n][0]
        start = pl.multiple_of(j * tq, tq)
        return (_dot(k_ref[0, pl.ds(start, tq), h * dv:(h + 1) * dv], qzt[n])
                + bias_ref[h, j - i + (nd - 1)])

    def pass1(j, ms):
        new = []
        for n in range(n_ch):
            s = scores(j, n)
            s_sc[n, j] = s
            new.append(jnp.maximum(ms[n], jnp.max(s, axis=0, keepdims=True)))
        return tuple(new)

    ms = lax.fori_loop(0, i, pass1, tuple(jnp.full((1, tq), NEG_INF, F32) for _ in range(n_ch)))
    kk = lax.broadcasted_iota(jnp.int32, (tq, tq), 0)
    qq = lax.broadcasted_iota(jnp.int32, (tq, tq), 1)
    allowed = (kk // CHUNK) <= (qq // CHUNK)
    m_fin = []
    for n in range(n_ch):
        s = jnp.where(allowed, scores(i, n), NEG_INF)
        s_sc[n, i] = s
        m_fin.append(jnp.maximum(ms[n], jnp.max(s, axis=0, keepdims=True)))

    acc_sc[...] = jnp.zeros(acc_sc.shape, F32)

    def pass2(j, ls):
        new = []
        p_next = jnp.exp(s_sc[0, j] - m_fin[0])
        for n, (h, c) in enumerate(chains):
            p = p_next
            if n + 1 < n_ch:
                p_next = jnp.exp(s_sc[n + 1, j] - m_fin[n + 1])
            new.append(ls[n] + jnp.sum(p, axis=0, keepdims=True))
            acc_sc[n] += _dot(vt_sc[h, j], p.astype(BF16))
        return tuple(new)

    ls = lax.fori_loop(0, i + 1, pass2, tuple(jnp.zeros((1, tq), F32) for _ in range(n_ch)))
    for h in range(A_HEADS):
        out0 = acc_sc[2 * h] / ls[2 * h]
        out1 = acc_sc[2 * h + 1] / ls[2 * h + 1]
        o = (out0 - lam_ref[0, 0] * out1).T
        o_ref[:, h * dv:(h + 1) * dv] = (_head_rms(o) * subln_ref[...] * out_scale).astype(BF16)


def _diff_attn(lam, z3, bias_tiles, subln, tq, out_scale):
    bsz, t, _ = z3.shape
    nd = t // tq
    width = A_HEADS * A_V_DIM
    return pl.pallas_call(
        functools.partial(_diff_attn_kernel, tq=tq, nd=nd, out_scale=out_scale),
        grid=(bsz, nd),
        in_specs=[pl.BlockSpec(memory_space=pltpu.SMEM),
                  pl.BlockSpec((1, tq, width), lambda b, i: (b, i, 0)),
                  pl.BlockSpec((1, t, width), lambda b, i: (b, 0, 1)),
                  pl.BlockSpec((1, t, width), lambda b, i: (b, 0, 2)),
                  pl.BlockSpec((A_HEADS, nd, tq, tq), lambda b, i: (0, 0, 0, 0),
                               pipeline_mode=pl.Buffered(1)),
                  pl.BlockSpec((1, A_V_DIM), lambda b, i: (0, 0))],
        out_specs=pl.BlockSpec((tq, width), lambda b, i: (b * nd + i, 0)),
        out_shape=jax.ShapeDtypeStruct((bsz * t, width), BF16),
        scratch_shapes=[pltpu.VMEM((A_HEADS, nd, A_V_DIM, tq), BF16),
                        pltpu.VMEM((2 * A_HEADS, nd, tq, tq), F32),
                        pltpu.VMEM((2 * A_HEADS, A_V_DIM, tq), F32)],
        compiler_params=_params(2),
    )(lam, z3, z3, z3, bias_tiles, subln.reshape(1, A_V_DIM))


def _diff_attn_cached_kernel(lam_ref, q_ref, kn_ref, vn_ref, kp_ref, vp_ref, bp_ref, bn_ref,
                             subln_ref, o_ref, *, t, past, out_scale):
    lane = lax.broadcasted_iota(jnp.int32, (t, A_V_DIM), 1)
    qpos_p = past + lax.broadcasted_iota(jnp.int32, (2 * t, past), 0) % t
    kpos_p = lax.broadcasted_iota(jnp.int32, (2 * t, past), 1)
    ok_p = (kpos_p // CHUNK) <= (qpos_p // CHUNK)
    qpos_n = past + lax.broadcasted_iota(jnp.int32, (2 * t, t), 0) % t
    kpos_n = past + lax.broadcasted_iota(jnp.int32, (2 * t, t), 1)
    ok_n = (kpos_n // CHUNK) <= (qpos_n // CHUNK)
    for h in range(A_HEADS):
        cs = slice(h * A_V_DIM, (h + 1) * A_V_DIM)
        q = q_ref[0, :, cs] * (A_QK_DIM ** -0.5)
        zero = jnp.zeros_like(q)
        qp = jnp.concatenate([jnp.where(lane < A_QK_DIM, q, zero),
                              jnp.where(lane >= A_QK_DIM, q, zero)], axis=0)
        kp = kp_ref[0, :, cs].astype(BF16)
        vp = vp_ref[0, :, cs].astype(BF16)
        kn = kn_ref[0, :, cs]
        vn = vn_ref[0, :, cs]
        bp = bp_ref[h]
        bn = bn_ref[h]
        sp = jnp.where(ok_p, _dot_nt(qp, kp) + jnp.concatenate([bp, bp], axis=0), NEG_INF)
        sn = jnp.where(ok_n, _dot_nt(qp, kn) + jnp.concatenate([bn, bn], axis=0), NEG_INF)
        m = jnp.maximum(jnp.max(sp, axis=-1, keepdims=True), jnp.max(sn, axis=-1, keepdims=True))
        pp = jnp.exp(sp - m)
        pn = jnp.exp(sn - m)
        l = jnp.sum(pp, axis=-1, keepdims=True) + jnp.sum(pn, axis=-1, keepdims=True)
        out = (_dot(pp.astype(BF16), vp) + _dot(pn.astype(BF16), vn)) / l
        o = out[:t] - lam_ref[0, 0] * out[t:]
        o_ref[:, cs] = (_head_rms(o) * subln_ref[...] * out_scale).astype(BF16)


def _diff_attn_cached(lam, z3, past_k, past_v, bias_past, bias_new, subln, out_scale):
    bsz, t, _ = z3.shape
    past = past_k.shape[1]
    width = A_HEADS * A_V_DIM
    return pl.pallas_call(
        functools.partial(_diff_attn_cached_kernel, t=t, past=past, out_scale=out_scale),
        grid=(bsz,),
        in_specs=[pl.BlockSpec(memory_space=pltpu.SMEM),
                  pl.BlockSpec((1, t, width), lambda b: (b, 0, 0)),
                  pl.BlockSpec((1, t, width), lambda b: (b, 0, 1)),
                  pl.BlockSpec((1, t, width), lambda b: (b, 0, 2)),
                  pl.BlockSpec((1, past, width), lambda b: (b, 0, 0)),
                  pl.BlockSpec((1, past, width), lambda b: (b, 0, 0)),
                  pl.BlockSpec((A_HEADS, t, past), lambda b: (0, 0, 0)),
                  pl.BlockSpec((A_HEADS, t, t), lambda b: (0, 0, 0)),
                  pl.BlockSpec((1, A_V_DIM), lambda b: (0, 0))],
        out_specs=pl.BlockSpec((t, width), lambda b: (b, 0)),
        out_shape=jax.ShapeDtypeStruct((bsz * t, width), BF16),
        compiler_params=_params(1),
    )(lam, z3, z3, z3, past_k, past_v, bias_past, bias_new, subln.reshape(1, A_V_DIM))


def _retention_kernel(q_ref, k_ref, v_ref, gt_ref, cos_ref, sin_ref, dec_ref, qd_ref, kd_ref,
                      cd_ref, s0_ref, o_ref, s_out_ref, s_sc):
    t = pl.program_id(1)

    @pl.when(t == 0)
    def _():
        s_sc[...] = s0_ref[0]

    d = B_QK_DIM
    half = d // 2
    cos = cos_ref[...]
    sin = sin_ref[...]
    for h in range(B_HEADS):
        cs = slice(h * d, (h + 1) * d)
        q = q_ref[0, :, cs].astype(F32)
        k = k_ref[0, :, cs].astype(F32)
        qr = (q * cos + pltpu.roll(q, half, 1) * sin) * (d ** -0.5)
        kr = k * cos + pltpu.roll(k, half, 1) * sin
        v = v_ref[0, :, cs]
        att = _dot_nt(qr.astype(BF16), kr.astype(BF16)) * dec_ref[h]
        s = s_sc[h]
        o = _dot(att.astype(BF16), v) + _dot((qr * qd_ref[h]).astype(BF16), s.astype(BF16))
        s_sc[h] = s * cd_ref[h] + _dot_tn((kr * kd_ref[h]).astype(BF16), v)
        gt = gt_ref[0, :, cs].astype(F32)
        o_ref[:, cs] = (_head_rms(o) * _silu(gt)).astype(BF16)

    @pl.when(t == pl.num_programs(1) - 1)
    def _():
        s_out_ref[0] = s_sc[...]


def _retention(z3, cos, sin, consts, s0, c):
    bsz, t, _ = z3.shape
    nt = t // c
    dec, qd, kd, cd = consts
    d = B_QK_DIM
    width = B_HEADS * d
    base = 3
    full3 = lambda b, i: (0, 0, 0)
    return pl.pallas_call(
        _retention_kernel,
        grid=(bsz, nt),
        in_specs=[pl.BlockSpec((1, c, width), lambda b, i: (b, i, base)),
                  pl.BlockSpec((1, c, width), lambda b, i: (b, i, base + 1)),
                  pl.BlockSpec((1, c, width), lambda b, i: (b, i, base + 2)),
                  pl.BlockSpec((1, c, width), lambda b, i: (b, i, base + 3)),
                  pl.BlockSpec((c, d), lambda b, i: (i, 0)),
                  pl.BlockSpec((c, d), lambda b, i: (i, 0)),
                  pl.BlockSpec((B_HEADS, c, c), full3),
                  pl.BlockSpec((B_HEADS, c, d), full3),
                  pl.BlockSpec((B_HEADS, c, d), full3),
                  pl.BlockSpec((B_HEADS, 1, d), full3),
                  pl.BlockSpec((1, B_HEADS, d, d), lambda b, i: (b, 0, 0, 0))],
        out_specs=[pl.BlockSpec((c, width), lambda b, i: (b * nt + i, 0)),
                   pl.BlockSpec((1, B_HEADS, d, d), lambda b, i: (b, 0, 0, 0))],
        out_shape=[jax.ShapeDtypeStruct((bsz * t, width), BF16),
                   jax.ShapeDtypeStruct((bsz, B_HEADS, d, d), F32)],
        scratch_shapes=[pltpu.VMEM((B_HEADS, d, d), F32)],
        compiler_params=_params(2),
    )(z3, z3, z3, z3, cos, sin, dec, qd, kd, cd, s0)


def _retention_consts(c):
    log_g = jnp.log1p(-jnp.exp2(-5.0 - jnp.arange(B_HEADS, dtype=F32)))
    idx = jnp.arange(c, dtype=F32)
    dist = idx[:, None] - idx[None, :]
    dec = jnp.where(dist >= 0, jnp.exp(jnp.maximum(dist, 0.0)[None] * log_g[:, None, None]), 0.0)
    qd = jnp.exp((idx[None, :] + 1.0) * log_g[:, None])
    kd = jnp.exp((c - 1.0 - idx)[None, :] * log_g[:, None])
    cd = jnp.exp(c * log_g)
    bc = lambda a: jnp.broadcast_to(a[..., None], a.shape + (B_QK_DIM,))
    return dec, bc(qd), bc(kd), bc(cd[:, None])


def _rotary_tables(pos):
    half = B_QK_DIM // 2
    inv = ROPE_BASE ** (-jnp.arange(half, dtype=F32) / half)
    ang = pos.astype(F32)[:, None] * inv[None, :]
    cos, sin = jnp.cos(ang), jnp.sin(ang)
    return jnp.concatenate([cos, cos], axis=-1), jnp.concatenate([-sin, sin], axis=-1)


def _gla_levels(c):
    return [c >> (l + 1) for l in range(int(math.log2(c)))]


def _gla_consts(c):
    levels = _gla_levels(c)
    rows = np.arange(c)
    mats = []
    for s in levels:
        ref = (rows // (2 * s)) * 2 * s + s - 1
        a = np.zeros((c, c), np.float32)
        for i in range(c):
            if i & s:
                a[i, ref[i] + 1:i + 1] = 1.0
            else:
                a[i, i + 1:ref[i] + 1] = 1.0
        mats.append(a)
    mats.append(np.tril(np.ones((c, c), np.float32)))
    mats.append(np.triu(np.ones((c, c), np.float32), 1))
    lv = np.full((c, c), -1, np.int32)
    for i in range(c):
        lv[i, i] = len(levels)
        for j in range(i):
            lv[i, j] = levels.index(1 << int(math.floor(math.log2(i ^ j))))
    return jnp.asarray(np.concatenate(mats, axis=0), BF16), jnp.asarray(lv)


def _gla_kernel(q_ref, k_ref, v_ref, r_ref, g_ref, s0_ref, a_ref, lv_ref, nw_ref,
                o_ref, s_out_ref, st_sc, *, c):
    t = pl.program_id(1)
    dk, dv = C_QK_DIM, C_V_DIM

    @pl.when(t == 0)
    def _():
        for h in range(C_HEADS):
            st_sc[h] = s0_ref[0, h].T

    levels = _gla_levels(c)
    n_lv = len(levels)
    lv = lv_ref[...]
    row = lax.broadcasted_iota(jnp.int32, (c, dk), 0)
    g = g_ref[0]
    g_hi = g.astype(BF16)
    g_lo = (g - g_hi.astype(F32)).astype(BF16)
    width = C_HEADS * dk
    x2 = _dot(a_ref[...], jnp.concatenate([g_hi, g_lo], axis=1))
    xs = x2[:, :width] + x2[:, width:]
    for h in range(C_HEADS):
        ks = slice(h * dk, (h + 1) * dk)
        vs = slice(h * dv, (h + 1) * dv)
        x = xs[:, ks]
        q = q_ref[0, :, ks].astype(F32) * (dk ** -0.5)
        k = k_ref[0, :, ks].astype(F32)
        v = v_ref[0, :, vs]
        att = jnp.zeros((c, c), F32)
        pending = (n_lv, _dot_nt(q.astype(BF16), k.astype(BF16)))
        for l, s in enumerate(levels):
            e = jnp.exp(x[l * c:(l + 1) * c])
            up = (row & s) != 0
            mix = jnp.where(up, q, k) * e
            qt = jnp.where(up, mix, 0.0).astype(BF16)
            kt = jnp.where(up, 0.0, mix).astype(BF16)
            prod = _dot_nt(qt, kt)
            att = jnp.where(lv == pending[0], pending[1], att)
            pending = (l, prod)
        att = jnp.where(lv == pending[0], pending[1], att)
        b = x[n_lv * c:(n_lv + 1) * c]
        rem = x[(n_lv + 1) * c:(n_lv + 2) * c]
        st = st_sc[h]
        o = _dot(att.astype(BF16), v) + _dot_nt((q * jnp.exp(b)).astype(BF16), st.astype(BF16))
        kd = (k * jnp.exp(rem)).astype(BF16)
        st_sc[h] = st * jnp.exp(b[c - 1:c, :]) + _dot_tn(v, kd)
        r = r_ref[0, :, vs].astype(F32)
        o_ref[:, vs] = (_head_rms(o) * nw_ref[...] * _silu(r)).astype(BF16)

    @pl.when(t == pl.num_programs(1) - 1)
    def _():
        for h in range(C_HEADS):
            s_out_ref[0, h] = st_sc[h].T


def _gla(z3, log_a3, s0, a_stack, lv, norm_w, c):
    bsz, t, _ = z3.shape
    nt = t // c
    dk, dv = C_QK_DIM, C_V_DIM
    qk_w, v_w = C_HEADS * dk, C_HEADS * dv
    const = lambda b, i: (0, 0)
    return pl.pallas_call(
        functools.partial(_gla_kernel, c=c),
        grid=(bsz, nt),
        in_specs=[pl.BlockSpec((1, c, qk_w), lambda b, i: (b, i, 0)),
                  pl.BlockSpec((1, c, qk_w), lambda b, i: (b, i, 1)),
                  pl.BlockSpec((1, c, v_w), lambda b, i: (b, i, 1)),
                  pl.BlockSpec((1, c, v_w), lambda b, i: (b, i, 2)),
                  pl.BlockSpec((1, c, qk_w), lambda b, i: (b, i, 0)),
                  pl.BlockSpec((1, C_HEADS, dk, dv), lambda b, i: (b, 0, 0, 0)),
                  pl.BlockSpec(a_stack.shape, const),
                  pl.BlockSpec((c, c), const),
                  pl.BlockSpec((1, dv), const)],
        out_specs=[pl.BlockSpec((c, v_w), lambda b, i: (b * nt + i, 0)),
                   pl.BlockSpec((1, C_HEADS, dk, dv), lambda b, i: (b, 0, 0, 0))],
        out_shape=[jax.ShapeDtypeStruct((bsz * t, v_w), BF16),
                   jax.ShapeDtypeStruct((bsz, C_HEADS, dk, dv), F32)],
        scratch_shapes=[pltpu.VMEM((C_HEADS, dv, dk), F32)],
        compiler_params=_params(2),
    )(z3, z3, z3, z3, log_a3, s0, a_stack, lv, norm_w.reshape(1, dv))


def _cross_kernel(*refs, n_mix):
    x_ref = refs[0]
    a_refs = refs[1:1 + n_mix]
    w_refs = refs[1 + n_mix:1 + 2 * n_mix]
    g_ref, wq_ref, mk_ref, mv_ref, wo_ref, o_ref = refs[1 + 2 * n_mix:]
    x = x_ref[...]
    for a_ref, w_ref in zip(a_refs, w_refs):
        x = x + _dot(a_ref[...], w_ref[...])
    h = _rms(x, g_ref[...]).astype(BF16)
    q = (_dot(h, wq_ref[...]) * (M_HEAD_DIM ** -0.5)).astype(BF16)
    mk = mk_ref[0].astype(BF16)
    mv = mv_ref[0].astype(BF16)
    outs = []
    for hd in range(M_HEADS):
        cs = slice(hd * M_HEAD_DIM, (hd + 1) * M_HEAD_DIM)
        s = _dot_nt(q[:, cs], mk[:, cs])
        p = jnp.exp(s - jnp.max(s, axis=-1, keepdims=True))
        l = jnp.sum(p, axis=-1, keepdims=True)
        outs.append((_dot(p.astype(BF16), mv[:, cs]) / l).astype(BF16))
    o_ref[...] = x + _dot(jnp.concatenate(outs, axis=1), wo_ref[...])


def _cross(x2d, mix_list, w_mix_list, gamma, wq, mk3, mv3, wo, t, tm):
    n = x2d.shape[0]
    nt = t // tm
    mem_len, width = mk3.shape[1], mk3.shape[2]
    rows = lambda b, i: (b * nt + i, 0)
    const = lambda b, i: (0, 0)
    in_specs = [pl.BlockSpec((tm, D_MODEL), rows)]
    in_specs += [pl.BlockSpec((tm, a.shape[1]), rows) for a in mix_list]
    in_specs += [pl.BlockSpec(w.shape, const) for w in w_mix_list]
    in_specs += [pl.BlockSpec((1, D_MODEL), const),
                 pl.BlockSpec((D_MODEL, width), const),
                 pl.BlockSpec((1, mem_len, width), lambda b, i: (b, 0, 0)),
                 pl.BlockSpec((1, mem_len, width), lambda b, i: (b, 0, 0)),
                 pl.BlockSpec((width, D_MODEL), const)]
    return pl.pallas_call(
        functools.partial(_cross_kernel, n_mix=len(mix_list)),
        grid=(n // t, nt),
        in_specs=in_specs,
        out_specs=pl.BlockSpec((tm, D_MODEL), rows),
        out_shape=jax.ShapeDtypeStruct((n, D_MODEL), F32),
        compiler_params=_params(2),
    )(x2d, *mix_list, *w_mix_list, gamma.reshape(1, D_MODEL), wq, mk3, mv3, wo)


def _ffn_kernel(x_ref, g_ref, wg_ref, wu_ref, cw_ref, cb_ref, prev_ref, wd_ref, *rest,
                tm, final_norm):
    if final_norm:
        gf_ref, o_ref, tail_ref, carry = rest
    else:
        o_ref, tail_ref, carry = rest
    t = pl.program_id(1)
    lo = SUBLANES - 2

    @pl.when(t == 0)
    def _():
        carry[lo:SUBLANES, :] = prev_ref[0]

    x = x_ref[...]
    h = _rms(x, g_ref[...]).astype(BF16)
    row = lax.broadcasted_iota(jnp.int32, (SUBLANES, FF_CHUNK), 0)
    acc = x
    n_chunks = D_FF // FF_CHUNK
    chunk = lambda c: slice(c * FF_CHUNK, (c + 1) * FF_CHUNK)
    nxt = (_dot(h, wg_ref[:, chunk(0)]), _dot(h, wu_ref[:, chunk(0)]))
    for c in range(n_chunks):
        cs = chunk(c)
        gate, up = nxt
        if c + 1 < n_chunks:
            nxt = (_dot(h, wg_ref[:, chunk(c + 1)]), _dot(h, wu_ref[:, chunk(c + 1)]))
        p1 = carry[SUBLANES - 1:SUBLANES, cs]
        p2 = carry[lo:lo + 1, cs]
        r1 = pltpu.roll(gate, 1, 0)
        r2 = pltpu.roll(gate, 2, 0)
        h1 = jnp.where(row == 0, p1, r1[:SUBLANES])
        h2 = jnp.where(row == 0, p2, jnp.where(row == 1, p1, r2[:SUBLANES]))
        g1 = jnp.concatenate([h1, r1[SUBLANES:]], axis=0)
        g2 = jnp.concatenate([h2, r2[SUBLANES:]], axis=0)
        carry[:, cs] = gate[tm - SUBLANES:tm]
        conv = cb_ref[:, cs] + cw_ref[0:1, cs] * g2
        conv = conv + cw_ref[1:2, cs] * g1
        conv = conv + cw_ref[2:3, cs] * gate
        act = (_gelu_tanh(conv) * up).astype(BF16)
        acc = acc + _dot(act, wd_ref[cs, :])
    if final_norm:
        o_ref[...] = _rms(acc, gf_ref[...])
    else:
        o_ref[...] = acc

    @pl.when(t == pl.num_programs(1) - 1)
    def _():
        tail_ref[0] = carry[lo:SUBLANES, :]


def _ffn(x2d, gamma, wg, wu, conv_w, conv_b, prev, wd, t, tm, final_gamma=None):
    n = x2d.shape[0]
    nt = t // tm
    bsz = n // t
    const = lambda b, i: (0, 0)
    resident = dict(pipeline_mode=pl.Buffered(1))
    in_specs = [pl.BlockSpec((tm, D_MODEL), lambda b, i: (b * nt + i, 0)),
                pl.BlockSpec((1, D_MODEL), const),
                pl.BlockSpec((D_MODEL, D_FF), const, **resident),
                pl.BlockSpec((D_MODEL, D_FF), const, **resident),
                pl.BlockSpec((3, D_FF), const),
                pl.BlockSpec((1, D_FF), const),
                pl.BlockSpec((1, 2, D_FF), lambda b, i: (b, 0, 0)),
                pl.BlockSpec((D_FF, D_MODEL), const, **resident)]
    args = [x2d, gamma.reshape(1, D_MODEL), wg, wu, conv_w, conv_b.reshape(1, D_FF), prev, wd]
    if final_gamma is not None:
        in_specs.append(pl.BlockSpec((1, D_MODEL), const))
        args.append(final_gamma.reshape(1, D_MODEL))
    return pl.pallas_call(
        functools.partial(_ffn_kernel, tm=tm, final_norm=final_gamma is not None),
        grid=(bsz, nt),
        in_specs=in_specs,
        out_specs=[pl.BlockSpec((tm, D_MODEL), lambda b, i: (b * nt + i, 0)),
                   pl.BlockSpec((1, 2, D_FF), lambda b, i: (b, 0, 0))],
        out_shape=[jax.ShapeDtypeStruct((n, D_MODEL), F32),
                   jax.ShapeDtypeStruct((bsz, 2, D_FF), F32)],
        scratch_shapes=[pltpu.VMEM((SUBLANES, D_FF), F32)],
        compiler_params=_params(2),
    )(*args)


def _rel_bucket(rel):
    nb = REL_BUCKETS // 2
    max_exact = nb // 2
    n = jnp.abs(rel)
    nf = jnp.maximum(n, 1).astype(F32)
    large = max_exact + (jnp.log(nf / max_exact) / math.log(REL_MAX_DIST / max_exact)
                         * (nb - max_exact)).astype(jnp.int32)
    large = jnp.minimum(large, nb - 1)
    return jnp.where(rel > 0, nb, 0) + jnp.where(n < max_exact, n, large)


def _bias_tiles(rel_bias, tq, nd):
    period = 2 * tq
    n = np.arange(period)
    rel = np.stack([np.where(n < tq, (d - (nd - 1)) * tq - n, (d - (nd - 1)) * tq + period - n)
                    for d in range(nd)]).astype(np.int32)
    w = jnp.transpose(rel_bias[_rel_bucket(jnp.asarray(rel))], (2, 0, 1)).astype(F32)
    flat = jnp.tile(w, (1, 1, tq))[:, :, :tq * (period - 1)]
    return flat.reshape(A_HEADS, nd, tq, period - 1)[:, :, :, :tq]


def _bias_rows(rel_bias, t, past):
    rel = (np.arange(past + t)[None, :] - (past + np.arange(t))[:, None]).astype(np.int32)
    bias = jnp.transpose(rel_bias[_rel_bucket(jnp.asarray(rel))], (2, 0, 1)).astype(F32)
    return bias[:, :, :past], bias[:, :, past:]


def _forward(x, past_k, past_v, ret_state, gla_state, conv_prev, mem_k, mem_v, p, w):
    bsz, t, _ = x.shape
    n = bsz * t
    past = 0 if past_k is None else past_k.shape[2]
    tm = min(512, n)
    tseq = min(512, t)
    tffn = min(256, t)
    tq = min(256, t)
    c_ret = min(256, t)
    c_gla = min(256, t)
    x2 = x.reshape(n, D_MODEL)
    pos = past + jnp.arange(t, dtype=jnp.int32)
    new_conv = []

    z, ak, av = _in_even(x2, p["ln_mix"][0], w["w_in_even"], tm)
    z3 = z.reshape(bsz, t, EVEN_IN)
    lam_init = 0.8 - 0.6 * math.exp(-0.3 * 0)
    lam = (jnp.exp(jnp.sum(p["diff_lq1"][0].astype(F32) * p["diff_lk1"][0].astype(F32)))
           - jnp.exp(jnp.sum(p["diff_lq2"][0].astype(F32) * p["diff_lk2"][0].astype(F32)))
           + lam_init).reshape(1, 1).astype(F32)
    if past == 0:
        o_a = _diff_attn(lam, z3, _bias_tiles(p["rel_bias"], tq, t // tq), p["diff_subln"][0],
                         tq, 1.0 - lam_init)
    else:
        width = A_HEADS * A_V_DIM
        bias_past, bias_new = _bias_rows(p["rel_bias"], t, past)
        o_a = _diff_attn_cached(lam, z3, past_k[0].reshape(bsz, past, width),
                                past_v[0].reshape(bsz, past, width), bias_past, bias_new,
                                p["diff_subln"][0], 1.0 - lam_init)
    cos, sin = _rotary_tables(pos)
    o_b, s_ret = _retention(z3, cos, sin, _retention_consts(c_ret), ret_state[0], c_ret)
    x2 = _cross(x2, [o_a, o_b], [w["w_out_even_a"], w["w_out_even_b"]], p["ln_cross"][0],
                w["w_cq"][0], mem_k[0], mem_v[0], w["w_co"][0], t, tseq)
    x2, tail = _ffn(x2, p["ln_ffn"][0], w["w_ffn_gate"][0], w["w_ffn_up"][0], p["ffn_conv_w"][0],
                    p["ffn_conv_b"][0], conv_prev[0], w["w_ffn_down"][0], t, tffn)
    new_conv.append(tail)

    zc, log_a = _in_odd(x2, p["ln_mix"][1], w["w_in_odd"], w["w_gate_lr"], p["b_gate"][0], tm)
    a_stack, lv = _gla_consts(c_gla)
    o_c, s_gla = _gla(zc.reshape(bsz, t, ODD_Z), log_a.reshape(bsz, t, C_HEADS * C_QK_DIM),
                      gla_state[0], a_stack, lv, p["gla_norm"][0], c_gla)
    x2 = _cross(x2, [o_c], [w["w_out_odd"]], p["ln_cross"][1],
                w["w_cq"][1], mem_k[1], mem_v[1], w["w_co"][1], t, tseq)
    y, tail = _ffn(x2, p["ln_ffn"][1], w["w_ffn_gate"][1], w["w_ffn_up"][1], p["ffn_conv_w"][1],
                   p["ffn_conv_b"][1], conv_prev[1], w["w_ffn_down"][1], t, tffn,
                   final_gamma=p["ln_final"])
    new_conv.append(tail)

    return (y.reshape(bsz, t, D_MODEL),
            ak.reshape(1, bsz, t, A_HEADS, A_V_DIM), av.reshape(1, bsz, t, A_HEADS, A_V_DIM),
            s_ret[None], s_gla[None], jnp.stack(new_conv))


def kernel(x_prompt, x_sample, cache_diff_k, cache_diff_v, state_retention, state_gla, cache_ffn_conv, cache_mem_k, cache_mem_v, mem_prompt, ln_mix, ln_cross, ln_ffn, ln_mem, ln_final, w_in_even, w_out_even, diff_lq1, diff_lk1, diff_lq2, diff_lk2, diff_subln, rel_bias, w_in_odd, w_gate_lr, b_gate, gla_norm, w_out_odd, w_cq, w_ck, w_cv, w_co, w_ffn_gate, w_ffn_up, ffn_conv_w, ffn_conv_b, w_ffn_down):
    p = dict(ln_mix=ln_mix, ln_cross=ln_cross, ln_ffn=ln_ffn, ln_final=ln_final,
             diff_lq1=diff_lq1, diff_lk1=diff_lk1, diff_lq2=diff_lq2, diff_lk2=diff_lk2,
             diff_subln=diff_subln, rel_bias=rel_bias, b_gate=b_gate, gla_norm=gla_norm,
             ffn_conv_w=ffn_conv_w, ffn_conv_b=ffn_conv_b)
    a_v = A_HEADS * A_V_DIM
    w = dict(
        w_in_even=w_in_even[0].astype(BF16),
        w_out_even_a=w_out_even[0, :a_v].astype(BF16),
        w_out_even_b=w_out_even[0, a_v:].astype(BF16),
        w_in_odd=jnp.pad(w_in_odd[0], ((0, 0), (0, ODD_IN_PAD - w_in_odd.shape[2]))).astype(BF16),
        w_gate_lr=jnp.pad(w_gate_lr[0], ((0, LANES - C_GATE_RANK), (0, 0))).astype(BF16),
        w_out_odd=w_out_odd[0].astype(BF16),
        w_cq=w_cq.astype(BF16), w_co=w_co.astype(BF16),
        w_ffn_gate=w_ffn_gate.astype(BF16), w_ffn_up=w_ffn_up.astype(BF16),
        w_ffn_down=w_ffn_down.astype(BF16))

    bp, mem_len, _ = mem_prompt.shape
    bs = x_sample.shape[0]
    m_width = M_HEADS * M_HEAD_DIM
    depth = ln_mem.shape[0]
    mem_k_p, mem_v_p, mk_b, mv_b = _mem_kv(mem_prompt, ln_mem, w_ck.astype(BF16),
                                           w_cv.astype(BF16), seqs=2)

    dt = x_prompt.dtype
    zero_ret = jnp.zeros((1, bp, B_HEADS, B_QK_DIM, B_QK_DIM), dt)
    zero_gla = jnp.zeros((1, bp, C_HEADS, C_QK_DIM, C_V_DIM), dt)
    zero_conv = jnp.zeros((depth, bp, 2, D_FF), dt)
    y_p, dk_p, dv_p, ret_p, gla_p, conv_p = _forward(
        x_prompt, None, None, zero_ret, zero_gla, zero_conv,
        mk_b.reshape(depth, bp, mem_len, m_width), mv_b.reshape(depth, bp, mem_len, m_width), p, w)
    y_s, dk_s, dv_s, ret_s, gla_s, conv_s = _forward(
        x_sample, cache_diff_k, cache_diff_v, state_retention, state_gla, cache_ffn_conv,
        cache_mem_k.reshape(depth, bs, mem_len, m_width),
        cache_mem_v.reshape(depth, bs, mem_len, m_width), p, w)
    return (y_p, y_s, dk_p, dv_p, ret_p, gla_p, conv_p, mem_k_p, mem_v_p,
            dk_s, dv_s, ret_s, gla_s, conv_s)
```

```python
import functools
import math

import numpy as np
import jax
import jax.numpy as jnp
from jax import lax
from jax.experimental import pallas as pl
from jax.experimental.pallas import tpu as pltpu

F32 = jnp.float32
BF16 = jnp.bfloat16

D_MODEL = 1024
CHUNK = 64
A_HEADS = 4
A_QK_DIM = 64
A_V_DIM = 128
B_HEADS = 4
B_QK_DIM = 128
C_HEADS = 4
C_QK_DIM = 128
C_V_DIM = 256
C_GATE_RANK = 16
C_GATE_TAU = 16.0
M_HEADS = 4
M_HEAD_DIM = 256
REL_BUCKETS = 32
REL_MAX_DIST = 128
D_FF = 2816
ROPE_BASE = 10000.0
EPS = 1e-6
NEG_INF = -1e30

EVEN_IN = 3584
ODD_Z = 3072
LANES = 128
SUBLANES = 8
ODD_IN_PAD = ODD_Z + LANES
FF_CHUNK = 256
ONES_ROWS = 16
LOG2_E = math.log2(math.e)
V7X_VMEM_LIMIT_BYTES = 56 * 1024 * 1024


def _params(n_axes):
    return pltpu.CompilerParams(dimension_semantics=("arbitrary",) * n_axes,
                                vmem_limit_bytes=V7X_VMEM_LIMIT_BYTES)


def _dot(a, b):
    return jnp.dot(a, b, preferred_element_type=F32)


def _dot_nt(a, b):
    return lax.dot_general(a, b, (((1,), (1,)), ((), ())), preferred_element_type=F32)


def _dot_tn(a, b):
    return lax.dot_general(a, b, (((0,), (0,)), ((), ())), preferred_element_type=F32)


def _rms(x, g):
    return x * lax.rsqrt(jnp.mean(x * x, axis=-1, keepdims=True) + EPS) * g


def _head_rms(x):
    return x * lax.rsqrt(jnp.mean(x * x, axis=-1, keepdims=True) + EPS)


def _silu(x):
    return x * (1.0 / (1.0 + jnp.exp(-x)))


def _gelu_tanh(x):
    c0 = math.sqrt(2.0 / math.pi)
    return x * (0.5 + 0.5 * jnp.tanh(x * (c0 + (c0 * 0.044715) * (x * x))))


def _log_sigmoid(x):
    return jnp.minimum(x, 0.0) - jnp.log1p(jnp.exp(-jnp.abs(x)))


def _store_head_major(o_ref, val, tm, n_heads, head_dim):
    tiles = head_dim // LANES
    group = n_heads * tiles
    for hd in range(n_heads):
        for part in range(tiles):
            col = hd * head_dim + part * LANES
            o_ref[pl.ds(part * n_heads + hd, tm, stride=group), :] = val[:, col:col + LANES]


def _in_even_kernel(x_ref, g_ref, w_ref, z_ref, k_ref, v_ref, *, tm):
    h = _rms(x_ref[...], g_ref[...]).astype(BF16)
    width = A_HEADS * A_V_DIM
    for c in range(EVEN_IN // width):
        zc = _dot(h, w_ref[:, c * width:(c + 1) * width])
        z_ref[:, c * width:(c + 1) * width] = zc.astype(BF16)
        if c == 1:
            _store_head_major(k_ref, zc, tm, A_HEADS, A_V_DIM)
        if c == 2:
            _store_head_major(v_ref, zc, tm, A_HEADS, A_V_DIM)


def _in_even(x2d, gamma, w, tm):
    n = x2d.shape[0]
    return pl.pallas_call(
        functools.partial(_in_even_kernel, tm=tm),
        grid=(n // tm,),
        in_specs=[pl.BlockSpec((tm, D_MODEL), lambda i: (i, 0)),
                  pl.BlockSpec((1, D_MODEL), lambda i: (0, 0)),
                  pl.BlockSpec((D_MODEL, EVEN_IN), lambda i: (0, 0))],
        out_specs=[pl.BlockSpec((tm, EVEN_IN), lambda i: (i, 0)),
                   pl.BlockSpec((tm * A_HEADS, A_V_DIM), lambda i: (i, 0)),
                   pl.BlockSpec((tm * A_HEADS, A_V_DIM), lambda i: (i, 0))],
        out_shape=[jax.ShapeDtypeStruct((n, EVEN_IN), BF16),
                   jax.ShapeDtypeStruct((n * A_HEADS, A_V_DIM), F32),
                   jax.ShapeDtypeStruct((n * A_HEADS, A_V_DIM), F32)],
        compiler_params=_params(1),
    )(x2d, gamma.reshape(1, D_MODEL), w)


def _in_odd_kernel(x_ref, g_ref, w_ref, wlr_ref, bg_ref, z_ref, la_ref):
    h = _rms(x_ref[...], g_ref[...]).astype(BF16)
    width = 512
    ca = _dot(h, w_ref[:, ODD_Z:ODD_IN_PAD]).astype(BF16)
    pre = _dot(ca, wlr_ref[...]) + bg_ref[...]
    la_ref[...] = _log_sigmoid(pre) / C_GATE_TAU
    for c in range(ODD_Z // width):
        z_ref[:, c * width:(c + 1) * width] = _dot(h, w_ref[:, c * width:(c + 1) * width]).astype(BF16)


def _in_odd(x2d, gamma, w_pad, wlr_pad, b_gate, tm):
    n = x2d.shape[0]
    qk = C_HEADS * C_QK_DIM
    return pl.pallas_call(
        _in_odd_kernel,
        grid=(n // tm,),
        in_specs=[pl.BlockSpec((tm, D_MODEL), lambda i: (i, 0)),
                  pl.BlockSpec((1, D_MODEL), lambda i: (0, 0)),
                  pl.BlockSpec((D_MODEL, ODD_IN_PAD), lambda i: (0, 0)),
                  pl.BlockSpec((LANES, qk), lambda i: (0, 0)),
                  pl.BlockSpec((1, qk), lambda i: (0, 0))],
        out_specs=[pl.BlockSpec((tm, ODD_Z), lambda i: (i, 0)),
                   pl.BlockSpec((tm, qk), lambda i: (i, 0))],
        out_shape=[jax.ShapeDtypeStruct((n, ODD_Z), BF16),
                   jax.ShapeDtypeStruct((n, qk), F32)],
        compiler_params=_params(1),
    )(x2d, gamma.reshape(1, D_MODEL), w_pad, wlr_pad, b_gate.reshape(1, qk))


def _mem_kv_kernel(x_ref, g_ref, wk_ref, wv_ref, k5_ref, v5_ref, kb_ref, vb_ref, *, seqs, mem_len):
    h = _rms(x_ref[...], g_ref[0]).astype(BF16)
    for w_ref, o5_ref, ob_ref in ((wk_ref, k5_ref, kb_ref), (wv_ref, v5_ref, vb_ref)):
        y = _dot(h, w_ref[0])
        ob_ref[0] = y.astype(BF16)
        for s in range(seqs):
            for hd in range(M_HEADS):
                o5_ref[0, s, :, hd, :] = y[s * mem_len:(s + 1) * mem_len,
                                           hd * M_HEAD_DIM:(hd + 1) * M_HEAD_DIM]


def _mem_kv(mem, ln_mem, wk, wv, seqs):
    bsz, mem_len, _ = mem.shape
    depth = ln_mem.shape[0]
    width = M_HEADS * M_HEAD_DIM
    n = bsz * mem_len
    tm = seqs * mem_len
    out5 = jax.ShapeDtypeStruct((depth, bsz, mem_len, M_HEADS, M_HEAD_DIM), F32)
    outb = jax.ShapeDtypeStruct((depth, n, width), BF16)
    spec5 = pl.BlockSpec((1, seqs, mem_len, M_HEADS, M_HEAD_DIM), lambda l, i: (l, i, 0, 0, 0))
    specb = pl.BlockSpec((1, tm, width), lambda l, i: (l, i, 0))
    wspec = pl.BlockSpec((1, D_MODEL, width), lambda l, i: (l, 0, 0))
    return pl.pallas_call(
        functools.partial(_mem_kv_kernel, seqs=seqs, mem_len=mem_len),
        grid=(depth, n // tm),
        in_specs=[pl.BlockSpec((tm, D_MODEL), lambda l, i: (i, 0)),
                  pl.BlockSpec((1, 1, D_MODEL), lambda l, i: (l, 0, 0)),
                  wspec, wspec],
        out_specs=[spec5, spec5, specb, specb],
        out_shape=[out5, out5, outb, outb],
        compiler_params=_params(2),
    )(mem.reshape(n, D_MODEL), ln_mem.reshape(depth, 1, D_MODEL), wk, wv)


def _diff_attn_kernel(lam_ref, q_ref, k_ref, v_ref, bias_ref, subln_ref, o_ref,
                      vt_sc, s_sc, acc_sc, *, tq, nd, out_scale):
    i = pl.program_id(1)
    dv = A_V_DIM
    chains = [(h, c) for h in range(A_HEADS) for c in range(2)]
    n_ch = len(chains)

    @pl.when(i == 0)
    def _():
        for h in range(A_HEADS):
            for jj in range(nd):
                vt_sc[h, jj, :dv, :] = v_ref[0, jj * tq:(jj + 1) * tq, h * dv:(h + 1) * dv].T
                vt_sc[h, jj, dv:, :] = jnp.ones((ONES_ROWS, tq), BF16)

    row = lax.broadcasted_iota(jnp.int32, (dv, tq), 0)
    qzt = []
    for h in range(A_HEADS):
        qt = (q_ref[0, :, h * dv:(h + 1) * dv] * (A_QK_DIM ** -0.5)).T
        for c in range(2):
            qzt.append(jnp.where((row >= A_QK_DIM) == (c == 1), qt, jnp.zeros_like(qt)))

    def raw_scores(j, n):
        h = chains[n][0]
        start = pl.multiple_of(j * tq, tq)
        return _dot(k_ref[0, pl.ds(start, tq), h * dv:(h + 1) * dv], qzt[n])

    def scaled(j, n, raw):
        return (raw + bias_ref[chains[n][0], j - i + (nd - 1)]) * LOG2_E

    def pass1(j, ms):
        new = []
        raw_next = raw_scores(j, 0)
        for n in range(n_ch):
            raw = raw_next
            if n + 1 < n_ch:
                raw_next = raw_scores(j, n + 1)
            s = scaled(j, n, raw)
            s_sc[n, j] = s
            new.append(jnp.maximum(ms[n], jnp.max(s, axis=0, keepdims=True)))
        return tuple(new)

    ms = lax.fori_loop(0, i, pass1, tuple(jnp.full((1, tq), NEG_INF, F32) for _ in range(n_ch)))
    kk = lax.broadcasted_iota(jnp.int32, (tq, tq), 0)
    qq = lax.broadcasted_iota(jnp.int32, (tq, tq), 1)
    allowed = (kk // CHUNK) <= (qq // CHUNK)
    m_fin = []
    for n in range(n_ch):
        s = jnp.where(allowed, scaled(i, n, raw_scores(i, n)), NEG_INF)
        s_sc[n, i] = s
        m_fin.append(jnp.maximum(ms[n], jnp.max(s, axis=0, keepdims=True)))

    acc_sc[...] = jnp.zeros(acc_sc.shape, F32)

    def pass2(j, carry):
        p_next = jnp.exp2(s_sc[0, j] - m_fin[0])
        for n, (h, c) in enumerate(chains):
            p = p_next
            if n + 1 < n_ch:
                p_next = jnp.exp2(s_sc[n + 1, j] - m_fin[n + 1])
            acc_sc[n] += _dot(vt_sc[h, j], p.astype(BF16))
        return carry

    lax.fori_loop(0, i + 1, pass2, 0)
    for h in range(A_HEADS):
        acc0 = acc_sc[2 * h]
        acc1 = acc_sc[2 * h + 1]
        out0 = acc0[:dv] / acc0[dv:dv + 1]
        out1 = acc1[:dv] / acc1[dv:dv + 1]
        o = (out0 - lam_ref[0, 0] * out1).T
        o_ref[:, h * dv:(h + 1) * dv] = (_head_rms(o) * subln_ref[...] * out_scale).astype(BF16)


def _diff_attn(lam, z3, bias_tiles, subln, tq, out_scale):
    bsz, t, _ = z3.shape
    nd = t // tq
    width = A_HEADS * A_V_DIM
    return pl.pallas_call(
        functools.partial(_diff_attn_kernel, tq=tq, nd=nd, out_scale=out_scale),
        grid=(bsz, nd),
        in_specs=[pl.BlockSpec(memory_space=pltpu.SMEM),
                  pl.BlockSpec((1, tq, width), lambda b, i: (b, i, 0)),
                  pl.BlockSpec((1, t, width), lambda b, i: (b, 0, 1)),
                  pl.BlockSpec((1, t, width), lambda b, i: (b, 0, 2)),
                  pl.BlockSpec((A_HEADS, nd, tq, tq), lambda b, i: (0, 0, 0, 0),
                               pipeline_mode=pl.Buffered(1)),
                  pl.BlockSpec((1, A_V_DIM), lambda b, i: (0, 0))],
        out_specs=pl.BlockSpec((tq, width), lambda b, i: (b * nd + i, 0)),
        out_shape=jax.ShapeDtypeStruct((bsz * t, width), BF16),
        scratch_shapes=[pltpu.VMEM((A_HEADS, nd, A_V_DIM + ONES_ROWS, tq), BF16),
                        pltpu.VMEM((2 * A_HEADS, nd, tq, tq), F32),
                        pltpu.VMEM((2 * A_HEADS, A_V_DIM + ONES_ROWS, tq), F32)],
        compiler_params=_params(2),
    )(lam, z3, z3, z3, bias_tiles, subln.reshape(1, A_V_DIM))


def _diff_attn_cached_kernel(lam_ref, q_ref, kn_ref, vn_ref, kp_ref, vp_ref, bp_ref, bn_ref,
                             subln_ref, o_ref, *, t, past, out_scale):
    lane = lax.broadcasted_iota(jnp.int32, (t, A_V_DIM), 1)
    qpos_p = past + lax.broadcasted_iota(jnp.int32, (2 * t, past), 0) % t
    kpos_p = lax.broadcasted_iota(jnp.int32, (2 * t, past), 1)
    ok_p = (kpos_p // CHUNK) <= (qpos_p // CHUNK)
    qpos_n = past + lax.broadcasted_iota(jnp.int32, (2 * t, t), 0) % t
    kpos_n = past + lax.broadcasted_iota(jnp.int32, (2 * t, t), 1)
    ok_n = (kpos_n // CHUNK) <= (qpos_n // CHUNK)
    for h in range(A_HEADS):
        cs = slice(h * A_V_DIM, (h + 1) * A_V_DIM)
        q = q_ref[0, :, cs] * (A_QK_DIM ** -0.5)
        zero = jnp.zeros_like(q)
        qp = jnp.concatenate([jnp.where(lane < A_QK_DIM, q, zero),
                              jnp.where(lane >= A_QK_DIM, q, zero)], axis=0)
        kp = kp_ref[0, :, cs].astype(BF16)
        vp = vp_ref[0, :, cs].astype(BF16)
        kn = kn_ref[0, :, cs]
        vn = vn_ref[0, :, cs]
        bp = bp_ref[h]
        bn = bn_ref[h]
        sp = jnp.where(ok_p, _dot_nt(qp, kp) + jnp.concatenate([bp, bp], axis=0), NEG_INF)
        sn = jnp.where(ok_n, _dot_nt(qp, kn) + jnp.concatenate([bn, bn], axis=0), NEG_INF)
        m = jnp.maximum(jnp.max(sp, axis=-1, keepdims=True), jnp.max(sn, axis=-1, keepdims=True))
        pp = jnp.exp(sp - m)
        pn = jnp.exp(sn - m)
        l = jnp.sum(pp, axis=-1, keepdims=True) + jnp.sum(pn, axis=-1, keepdims=True)
        out = (_dot(pp.astype(BF16), vp) + _dot(pn.astype(BF16), vn)) / l
        o = out[:t] - lam_ref[0, 0] * out[t:]
        o_ref[:, cs] = (_head_rms(o) * subln_ref[...] * out_scale).astype(BF16)


def _diff_attn_cached(lam, z3, past_k, past_v, bias_past, bias_new, subln, out_scale):
    bsz, t, _ = z3.shape
    past = past_k.shape[1]
    width = A_HEADS * A_V_DIM
    return pl.pallas_call(
        functools.partial(_diff_attn_cached_kernel, t=t, past=past, out_scale=out_scale),
        grid=(bsz,),
        in_specs=[pl.BlockSpec(memory_space=pltpu.SMEM),
                  pl.BlockSpec((1, t, width), lambda b: (b, 0, 0)),
                  pl.BlockSpec((1, t, width), lambda b: (b, 0, 1)),
                  pl.BlockSpec((1, t, width), lambda b: (b, 0, 2)),
                  pl.BlockSpec((1, past, width), lambda b: (b, 0, 0)),
                  pl.BlockSpec((1, past, width), lambda b: (b, 0, 0)),
                  pl.BlockSpec((A_HEADS, t, past), lambda b: (0, 0, 0)),
                  pl.BlockSpec((A_HEADS, t, t), lambda b: (0, 0, 0)),
                  pl.BlockSpec((1, A_V_DIM), lambda b: (0, 0))],
        out_specs=pl.BlockSpec((t, width), lambda b: (b, 0)),
        out_shape=jax.ShapeDtypeStruct((bsz * t, width), BF16),
        compiler_params=_params(1),
    )(lam, z3, z3, z3, past_k, past_v, bias_past, bias_new, subln.reshape(1, A_V_DIM))


def _retention_kernel(q_ref, k_ref, v_ref, gt_ref, cos_ref, sin_ref, dec_ref, qd_ref, kd_ref,
                      cd_ref, s0_ref, o_ref, s_out_ref, s_sc):
    t = pl.program_id(1)

    @pl.when(t == 0)
    def _():
        s_sc[...] = s0_ref[0]

    d = B_QK_DIM
    half = d // 2
    cos = cos_ref[...]
    sin = sin_ref[...]
    for h in range(B_HEADS):
        cs = slice(h * d, (h + 1) * d)
        q = q_ref[0, :, cs].astype(F32)
        k = k_ref[0, :, cs].astype(F32)
        qr = (q * cos + pltpu.roll(q, half, 1) * sin) * (d ** -0.5)
        kr = k * cos + pltpu.roll(k, half, 1) * sin
        v = v_ref[0, :, cs]
        att = _dot_nt(qr.astype(BF16), kr.astype(BF16)) * dec_ref[h]
        s = s_sc[h]
        o = _dot(att.astype(BF16), v) + _dot((qr * qd_ref[h]).astype(BF16), s.astype(BF16))
        s_sc[h] = s * cd_ref[h] + _dot_tn((kr * kd_ref[h]).astype(BF16), v)
        gt = gt_ref[0, :, cs].astype(F32)
        o_ref[:, cs] = (_head_rms(o) * _silu(gt)).astype(BF16)

    @pl.when(t == pl.num_programs(1) - 1)
    def _():
        s_out_ref[0] = s_sc[...]


def _retention(z3, cos, sin, consts, s0, c):
    bsz, t, _ = z3.shape
    nt = t // c
    dec, qd, kd, cd = consts
    d = B_QK_DIM
    width = B_HEADS * d
    base = 3
    full3 = lambda b, i: (0, 0, 0)
    return pl.pallas_call(
        _retention_kernel,
        grid=(bsz, nt),
        in_specs=[pl.BlockSpec((1, c, width), lambda b, i: (b, i, base)),
                  pl.BlockSpec((1, c, width), lambda b, i: (b, i, base + 1)),
                  pl.BlockSpec((1, c, width), lambda b, i: (b, i, base + 2)),
                  pl.BlockSpec((1, c, width), lambda b, i: (b, i, base + 3)),
                  pl.BlockSpec((c, d), lambda b, i: (i, 0)),
                  pl.BlockSpec((c, d), lambda b, i: (i, 0)),
                  pl.BlockSpec((B_HEADS, c, c), full3),
                  pl.BlockSpec((B_HEADS, c, d), full3),
                  pl.BlockSpec((B_HEADS, c, d), full3),
                  pl.BlockSpec((B_HEADS, 1, d), full3),
                  pl.BlockSpec((1, B_HEADS, d, d), lambda b, i: (b, 0, 0, 0))],
        out_specs=[pl.BlockSpec((c, width), lambda b, i: (b * nt + i, 0)),
                   pl.BlockSpec((1, B_HEADS, d, d), lambda b, i: (b, 0, 0, 0))],
        out_shape=[jax.ShapeDtypeStruct((bsz * t, width), BF16),
                   jax.ShapeDtypeStruct((bsz, B_HEADS, d, d), F32)],
        scratch_shapes=[pltpu.VMEM((B_HEADS, d, d), F32)],
        compiler_params=_params(2),
    )(z3, z3, z3, z3, cos, sin, dec, qd, kd, cd, s0)


def _retention_consts(c):
    log_g = jnp.log1p(-jnp.exp2(-5.0 - jnp.arange(B_HEADS, dtype=F32)))
    idx = jnp.arange(c, dtype=F32)
    dist = idx[:, None] - idx[None, :]
    dec = jnp.where(dist >= 0, jnp.exp(jnp.maximum(dist, 0.0)[None] * log_g[:, None, None]), 0.0)
    qd = jnp.exp((idx[None, :] + 1.0) * log_g[:, None])
    kd = jnp.exp((c - 1.0 - idx)[None, :] * log_g[:, None])
    cd = jnp.exp(c * log_g)
    bc = lambda a: jnp.broadcast_to(a[..., None], a.shape + (B_QK_DIM,))
    return dec, bc(qd), bc(kd), bc(cd[:, None])


def _rotary_tables(pos):
    half = B_QK_DIM // 2
    inv = ROPE_BASE ** (-jnp.arange(half, dtype=F32) / half)
    ang = pos.astype(F32)[:, None] * inv[None, :]
    cos, sin = jnp.cos(ang), jnp.sin(ang)
    return jnp.concatenate([cos, cos], axis=-1), jnp.concatenate([-sin, sin], axis=-1)


def _gla_levels(c):
    return [c >> (l + 1) for l in range(int(math.log2(c)))]


def _gla_consts(c):
    levels = _gla_levels(c)
    rows = np.arange(c)
    mats = []
    for s in levels:
        ref = (rows // (2 * s)) * 2 * s + s - 1
        a = np.zeros((c, c), np.float32)
        for i in range(c):
            if i & s:
                a[i, ref[i] + 1:i + 1] = 1.0
            else:
                a[i, i + 1:ref[i] + 1] = 1.0
        mats.append(a)
    mats.append(np.tril(np.ones((c, c), np.float32)))
    mats.append(np.triu(np.ones((c, c), np.float32), 1))
    lv = np.full((c, c), -1, np.int32)
    for i in range(c):
        lv[i, i] = len(levels)
        for j in range(i):
            lv[i, j] = levels.index(1 << int(math.floor(math.log2(i ^ j))))
    return jnp.asarray(np.concatenate(mats, axis=0), BF16), jnp.asarray(lv)


def _gla_kernel(q_ref, k_ref, v_ref, r_ref, g_ref, s0_ref, a_ref, lv_ref, nw_ref,
                o_ref, s_out_ref, st_sc, *, c):
    t = pl.program_id(1)
    dk, dv = C_QK_DIM, C_V_DIM

    @pl.when(t == 0)
    def _():
        for h in range(C_HEADS):
            st_sc[h] = s0_ref[0, h].T

    levels = _gla_levels(c)
    n_lv = len(levels)
    lv = lv_ref[...]
    row = lax.broadcasted_iota(jnp.int32, (c, dk), 0)
    g = g_ref[0]
    xs = _dot(a_ref[...], g.astype(BF16))
    for h in range(C_HEADS):
        ks = slice(h * dk, (h + 1) * dk)
        vs = slice(h * dv, (h + 1) * dv)
        x = xs[:, ks]
        q = q_ref[0, :, ks].astype(F32) * (dk ** -0.5)
        k = k_ref[0, :, ks].astype(F32)
        v = v_ref[0, :, vs]
        att = jnp.zeros((c, c), F32)
        pending = (n_lv, _dot_nt(q.astype(BF16), k.astype(BF16)))
        for l, s in enumerate(levels):
            e = jnp.exp(x[l * c:(l + 1) * c])
            up = (row & s) != 0
            mix = jnp.where(up, q, k) * e
            qt = jnp.where(up, mix, 0.0).astype(BF16)
            kt = jnp.where(up, 0.0, mix).astype(BF16)
            prod = _dot_nt(qt, kt)
            att = jnp.where(lv == pending[0], pending[1], att)
            pending = (l, prod)
        att = jnp.where(lv == pending[0], pending[1], att)
        b = x[n_lv * c:(n_lv + 1) * c]
        rem = x[(n_lv + 1) * c:(n_lv + 2) * c]
        st = st_sc[h]
        o = _dot(att.astype(BF16), v) + _dot_nt((q * jnp.exp(b)).astype(BF16), st.astype(BF16))
        kd = (k * jnp.exp(rem)).astype(BF16)
        st_sc[h] = st * jnp.exp(b[c - 1:c, :]) + _dot_tn(v, kd)
        r = r_ref[0, :, vs].astype(F32)
        o_ref[:, vs] = (_head_rms(o) * nw_ref[...] * _silu(r)).astype(BF16)

    @pl.when(t == pl.num_programs(1) - 1)
    def _():
        for h in range(C_HEADS):
            s_out_ref[0, h] = st_sc[h].T


def _gla(z3, log_a3, s0, a_stack, lv, norm_w, c):
    bsz, t, _ = z3.shape
    nt = t // c
    dk, dv = C_QK_DIM, C_V_DIM
    qk_w, v_w = C_HEADS * dk, C_HEADS * dv
    const = lambda b, i: (0, 0)
    return pl.pallas_call(
        functools.partial(_gla_kernel, c=c),
        grid=(bsz, nt),
        in_specs=[pl.BlockSpec((1, c, qk_w), lambda b, i: (b, i, 0)),
                  pl.BlockSpec((1, c, qk_w), lambda b, i: (b, i, 1)),
                  pl.BlockSpec((1, c, v_w), lambda b, i: (b, i, 1)),
                  pl.BlockSpec((1, c, v_w), lambda b, i: (b, i, 2)),
                  pl.BlockSpec((1, c, qk_w), lambda b, i: (b, i, 0)),
                  pl.BlockSpec((1, C_HEADS, dk, dv), lambda b, i: (b, 0, 0, 0)),
                  pl.BlockSpec(a_stack.shape, const),
                  pl.BlockSpec((c, c), const),
                  pl.BlockSpec((1, dv), const)],
        out_specs=[pl.BlockSpec((c, v_w), lambda b, i: (b * nt + i, 0)),
                   pl.BlockSpec((1, C_HEADS, dk, dv), lambda b, i: (b, 0, 0, 0))],
        out_shape=[jax.ShapeDtypeStruct((bsz * t, v_w), BF16),
                   jax.ShapeDtypeStruct((bsz, C_HEADS, dk, dv), F32)],
        scratch_shapes=[pltpu.VMEM((C_HEADS, dv, dk), F32)],
        compiler_params=_params(2),
    )(z3, z3, z3, z3, log_a3, s0, a_stack, lv, norm_w.reshape(1, dv))


def _cross_kernel(*refs, n_mix):
    x_ref = refs[0]
    a_refs = refs[1:1 + n_mix]
    w_refs = refs[1 + n_mix:1 + 2 * n_mix]
    g_ref, wq_ref, mk_ref, mv_ref, wo_ref, o_ref = refs[1 + 2 * n_mix:]
    x = x_ref[...]
    for a_ref, w_ref in zip(a_refs, w_refs):
        x = x + _dot(a_ref[...], w_ref[...])
    h = _rms(x, g_ref[...]).astype(BF16)
    q = (_dot(h, wq_ref[...]) * (M_HEAD_DIM ** -0.5)).astype(BF16)
    mk = mk_ref[0].astype(BF16)
    mv = mv_ref[0].astype(BF16)
    outs = []
    for hd in range(M_HEADS):
        cs = slice(hd * M_HEAD_DIM, (hd + 1) * M_HEAD_DIM)
        s = _dot_nt(q[:, cs], mk[:, cs])
        p = jnp.exp(s - jnp.max(s, axis=-1, keepdims=True))
        l = jnp.sum(p, axis=-1, keepdims=True)
        outs.append((_dot(p.astype(BF16), mv[:, cs]) / l).astype(BF16))
    o_ref[...] = x + _dot(jnp.concatenate(outs, axis=1), wo_ref[...])


def _cross(x2d, mix_list, w_mix_list, gamma, wq, mk3, mv3, wo, t, tm):
    n = x2d.shape[0]
    nt = t // tm
    mem_len, width = mk3.shape[1], mk3.shape[2]
    rows = lambda b, i: (b * nt + i, 0)
    const = lambda b, i: (0, 0)
    in_specs = [pl.BlockSpec((tm, D_MODEL), rows)]
    in_specs += [pl.BlockSpec((tm, a.shape[1]), rows) for a in mix_list]
    in_specs += [pl.BlockSpec(w.shape, const) for w in w_mix_list]
    in_specs += [pl.BlockSpec((1, D_MODEL), const),
                 pl.BlockSpec((D_MODEL, width), const),
                 pl.BlockSpec((1, mem_len, width), lambda b, i: (b, 0, 0)),
                 pl.BlockSpec((1, mem_len, width), lambda b, i: (b, 0, 0)),
                 pl.BlockSpec((width, D_MODEL), const)]
    return pl.pallas_call(
        functools.partial(_cross_kernel, n_mix=len(mix_list)),
        grid=(n // t, nt),
        in_specs=in_specs,
        out_specs=pl.BlockSpec((tm, D_MODEL), rows),
        out_shape=jax.ShapeDtypeStruct((n, D_MODEL), F32),
        compiler_params=_params(2),
    )(x2d, *mix_list, *w_mix_list, gamma.reshape(1, D_MODEL), wq, mk3, mv3, wo)


def _ffn_kernel(x_ref, g_ref, wg_ref, wu_ref, cw_ref, cb_ref, prev_ref, wd_ref, *rest,
                tm, final_norm):
    if final_norm:
        gf_ref, o_ref, tail_ref, carry = rest
    else:
        o_ref, tail_ref, carry = rest
    t = pl.program_id(1)
    lo = SUBLANES - 2

    @pl.when(t == 0)
    def _():
        carry[lo:SUBLANES, :] = prev_ref[0]

    x = x_ref[...]
    h = _rms(x, g_ref[...]).astype(BF16)
    row = lax.broadcasted_iota(jnp.int32, (SUBLANES, FF_CHUNK), 0)
    acc = x
    n_chunks = D_FF // FF_CHUNK
    chunk = lambda c: slice(c * FF_CHUNK, (c + 1) * FF_CHUNK)
    nxt = (_dot(h, wg_ref[:, chunk(0)]), _dot(h, wu_ref[:, chunk(0)]))
    for c in range(n_chunks):
        cs = chunk(c)
        gate, up = nxt
        if c + 1 < n_chunks:
            nxt = (_dot(h, wg_ref[:, chunk(c + 1)]), _dot(h, wu_ref[:, chunk(c + 1)]))
        p1 = carry[SUBLANES - 1:SUBLANES, cs]
        p2 = carry[lo:lo + 1, cs]
        r1 = pltpu.roll(gate, 1, 0)
        r2 = pltpu.roll(gate, 2, 0)
        h1 = jnp.where(row == 0, p1, r1[:SUBLANES])
        h2 = jnp.where(row == 0, p2, jnp.where(row == 1, p1, r2[:SUBLANES]))
        g1 = jnp.concatenate([h1, r1[SUBLANES:]], axis=0)
        g2 = jnp.concatenate([h2, r2[SUBLANES:]], axis=0)
        carry[:, cs] = gate[tm - SUBLANES:tm]
        conv = cb_ref[:, cs] + cw_ref[0:1, cs] * g2
        conv = conv + cw_ref[1:2, cs] * g1
        conv = conv + cw_ref[2:3, cs] * gate
        act = (_gelu_tanh(conv) * up).astype(BF16)
        acc = acc + _dot(act, wd_ref[cs, :])
    if final_norm:
        o_ref[...] = _rms(acc, gf_ref[...])
    else:
        o_ref[...] = acc

    @pl.when(t == pl.num_programs(1) - 1)
    def _():
        tail_ref[0] = carry[lo:SUBLANES, :]


def _ffn(x2d, gamma, wg, wu, conv_w, conv_b, prev, wd, t, tm, final_gamma=None):
    n = x2d.shape[0]
    nt = t // tm
    bsz = n // t
    const = lambda b, i: (0, 0)
    resident = dict(pipeline_mode=pl.Buffered(1))
    in_specs = [pl.BlockSpec((tm, D_MODEL), lambda b, i: (b * nt + i, 0)),
                pl.BlockSpec((1, D_MODEL), const),
                pl.BlockSpec((D_MODEL, D_FF), const, **resident),
                pl.BlockSpec((D_MODEL, D_FF), const, **resident),
                pl.BlockSpec((3, D_FF), const),
                pl.BlockSpec((1, D_FF), const),
                pl.BlockSpec((1, 2, D_FF), lambda b, i: (b, 0, 0)),
                pl.BlockSpec((D_FF, D_MODEL), const, **resident)]
    args = [x2d, gamma.reshape(1, D_MODEL), wg, wu, conv_w, conv_b.reshape(1, D_FF), prev, wd]
    if final_gamma is not None:
        in_specs.append(pl.BlockSpec((1, D_MODEL), const))
        args.append(final_gamma.reshape(1, D_MODEL))
    return pl.pallas_call(
        functools.partial(_ffn_kernel, tm=tm, final_norm=final_gamma is not None),
        grid=(bsz, nt),
        in_specs=in_specs,
        out_specs=[pl.BlockSpec((tm, D_MODEL), lambda b, i: (b * nt + i, 0)),
                   pl.BlockSpec((1, 2, D_FF), lambda b, i: (b, 0, 0))],
        out_shape=[jax.ShapeDtypeStruct((n, D_MODEL), F32),
                   jax.ShapeDtypeStruct((bsz, 2, D_FF), F32)],
        scratch_shapes=[pltpu.VMEM((SUBLANES, D_FF), F32)],
        compiler_params=_params(2),
    )(*args)


def _rel_bucket(rel):
    nb = REL_BUCKETS // 2
    max_exact = nb // 2
    n = jnp.abs(rel)
    nf = jnp.maximum(n, 1).astype(F32)
    large = max_exact + (jnp.log(nf / max_exact) / math.log(REL_MAX_DIST / max_exact)
                         * (nb - max_exact)).astype(jnp.int32)
    large = jnp.minimum(large, nb - 1)
    return jnp.where(rel > 0, nb, 0) + jnp.where(n < max_exact, n, large)


def _bias_tiles(rel_bias, tq, nd):
    period = 2 * tq
    n = np.arange(period)
    rel = np.stack([np.where(n < tq, (d - (nd - 1)) * tq - n, (d - (nd - 1)) * tq + period - n)
                    for d in range(nd)]).astype(np.int32)
    w = jnp.transpose(rel_bias[_rel_bucket(jnp.asarray(rel))], (2, 0, 1)).astype(F32)
    flat = jnp.tile(w, (1, 1, tq))[:, :, :tq * (period - 1)]
    return flat.reshape(A_HEADS, nd, tq, period - 1)[:, :, :, :tq]


def _bias_rows(rel_bias, t, past):
    rel = (np.arange(past + t)[None, :] - (past + np.arange(t))[:, None]).astype(np.int32)
    bias = jnp.transpose(rel_bias[_rel_bucket(jnp.asarray(rel))], (2, 0, 1)).astype(F32)
    return bias[:, :, :past], bias[:, :, past:]


def _forward(x, past_k, past_v, ret_state, gla_state, conv_prev, mem_k, mem_v, p, w):
    bsz, t, _ = x.shape
    n = bsz * t
    past = 0 if past_k is None else past_k.shape[2]
    tm = min(512, n)
    tseq = min(512, t)
    tffn = min(256, t)
    tq = min(256, t)
    c_ret = min(256, t)
    c_gla = min(256, t)
    x2 = x.reshape(n, D_MODEL)
    pos = past + jnp.arange(t, dtype=jnp.int32)
    new_conv = []

    z, ak, av = _in_even(x2, p["ln_mix"][0], w["w_in_even"], tm)
    z3 = z.reshape(bsz, t, EVEN_IN)
    lam_init = 0.8 - 0.6 * math.exp(-0.3 * 0)
    lam = (jnp.exp(jnp.sum(p["diff_lq1"][0].astype(F32) * p["diff_lk1"][0].astype(F32)))
           - jnp.exp(jnp.sum(p["diff_lq2"][0].astype(F32) * p["diff_lk2"][0].astype(F32)))
           + lam_init).reshape(1, 1).astype(F32)
    if past == 0:
        o_a = _diff_attn(lam, z3, _bias_tiles(p["rel_bias"], tq, t // tq), p["diff_subln"][0],
                         tq, 1.0 - lam_init)
    else:
        width = A_HEADS * A_V_DIM
        bias_past, bias_new = _bias_rows(p["rel_bias"], t, past)
        o_a = _diff_attn_cached(lam, z3, past_k[0].reshape(bsz, past, width),
                                past_v[0].reshape(bsz, past, width), bias_past, bias_new,
                                p["diff_subln"][0], 1.0 - lam_init)
    cos, sin = _rotary_tables(pos)
    o_b, s_ret = _retention(z3, cos, sin, _retention_consts(c_ret), ret_state[0], c_ret)
    x2 = _cross(x2, [o_a, o_b], [w["w_out_even_a"], w["w_out_even_b"]], p["ln_cross"][0],
                w["w_cq"][0], mem_k[0], mem_v[0], w["w_co"][0], t, tseq)
    x2, tail = _ffn(x2, p["ln_ffn"][0], w["w_ffn_gate"][0], w["w_ffn_up"][0], p["ffn_conv_w"][0],
                    p["ffn_conv_b"][0], conv_prev[0], w["w_ffn_down"][0], t, tffn)
    new_conv.append(tail)

    zc, log_a = _in_odd(x2, p["ln_mix"][1], w["w_in_odd"], w["w_gate_lr"], p["b_gate"][0], tm)
    a_stack, lv = _gla_consts(c_gla)
    o_c, s_gla = _gla(zc.reshape(bsz, t, ODD_Z), log_a.reshape(bsz, t, C_HEADS * C_QK_DIM),
                      gla_state[0], a_stack, lv, p["gla_norm"][0], c_gla)
    x2 = _cross(x2, [o_c], [w["w_out_odd"]], p["ln_cross"][1],
                w["w_cq"][1], mem_k[1], mem_v[1], w["w_co"][1], t, tseq)
    y, tail = _ffn(x2, p["ln_ffn"][1], w["w_ffn_gate"][1], w["w_ffn_up"][1], p["ffn_conv_w"][1],
                   p["ffn_conv_b"][1], conv_prev[1], w["w_ffn_down"][1], t, tffn,
                   final_gamma=p["ln_final"])
    new_conv.append(tail)

    return (y.reshape(bsz, t, D_MODEL),
            ak.reshape(1, bsz, t, A_HEADS, A_V_DIM), av.reshape(1, bsz, t, A_HEADS, A_V_DIM),
            s_ret[None], s_gla[None], jnp.stack(new_conv))


def kernel(x_prompt, x_sample, cache_diff_k, cache_diff_v, state_retention, state_gla, cache_ffn_conv, cache_mem_k, cache_mem_v, mem_prompt, ln_mix, ln_cross, ln_ffn, ln_mem, ln_final, w_in_even, w_out_even, diff_lq1, diff_lk1, diff_lq2, diff_lk2, diff_subln, rel_bias, w_in_odd, w_gate_lr, b_gate, gla_norm, w_out_odd, w_cq, w_ck, w_cv, w_co, w_ffn_gate, w_ffn_up, ffn_conv_w, ffn_conv_b, w_ffn_down):
    p = dict(ln_mix=ln_mix, ln_cross=ln_cross, ln_ffn=ln_ffn, ln_final=ln_final,
             diff_lq1=diff_lq1, diff_lk1=diff_lk1, diff_lq2=diff_lq2, diff_lk2=diff_lk2,
             diff_subln=diff_subln, rel_bias=rel_bias, b_gate=b_gate, gla_norm=gla_norm,
             ffn_conv_w=ffn_conv_w, ffn_conv_b=ffn_conv_b)
    a_v = A_HEADS * A_V_DIM
    w = dict(
        w_in_even=w_in_even[0].astype(BF16),
        w_out_even_a=w_out_even[0, :a_v].astype(BF16),
        w_out_even_b=w_out_even[0, a_v:].astype(BF16),
        w_in_odd=jnp.pad(w_in_odd[0], ((0, 0), (0, ODD_IN_PAD - w_in_odd.shape[2]))).astype(BF16),
        w_gate_lr=jnp.pad(w_gate_lr[0], ((0, LANES - C_GATE_RANK), (0, 0))).astype(BF16),
        w_out_odd=w_out_odd[0].astype(BF16),
        w_cq=w_cq.astype(BF16), w_co=w_co.astype(BF16),
        w_ffn_gate=w_ffn_gate.astype(BF16), w_ffn_up=w_ffn_up.astype(BF16),
        w_ffn_down=w_ffn_down.astype(BF16))

    bp, mem_len, _ = mem_prompt.shape
    bs = x_sample.shape[0]
    m_width = M_HEADS * M_HEAD_DIM
    depth = ln_mem.shape[0]
    mem_k_p, mem_v_p, mk_b, mv_b = _mem_kv(mem_prompt, ln_mem, w_ck.astype(BF16),
                                           w_cv.astype(BF16), seqs=2)

    dt = x_prompt.dtype
    zero_ret = jnp.zeros((1, bp, B_HEADS, B_QK_DIM, B_QK_DIM), dt)
    zero_gla = jnp.zeros((1, bp, C_HEADS, C_QK_DIM, C_V_DIM), dt)
    zero_conv = jnp.zeros((depth, bp, 2, D_FF), dt)
    y_p, dk_p, dv_p, ret_p, gla_p, conv_p = _forward(
        x_prompt, None, None, zero_ret, zero_gla, zero_conv,
        mk_b.reshape(depth, bp, mem_len, m_width), mv_b.reshape(depth, bp, mem_len, m_width), p, w)
    y_s, dk_s, dv_s, ret_s, gla_s, conv_s = _forward(
        x_sample, cache_diff_k, cache_diff_v, state_retention, state_gla, cache_ffn_conv,
        cache_mem_k.reshape(depth, bs, mem_len, m_width),
        cache_mem_v.reshape(depth, bs, mem_len, m_width), p, w)
    return (y_p, y_s, dk_p, dv_p, ret_p, gla_p, conv_p, mem_k_p, mem_v_p,
            dk_s, dv_s, ret_s, gla_s, conv_s)
```

```python
import functools
import math

import numpy as np
import jax
import jax.numpy as jnp
from jax import lax
from jax.experimental import pallas as pl
from jax.experimental.pallas import tpu as pltpu

F32 = jnp.float32
BF16 = jnp.bfloat16

D_MODEL = 1024
CHUNK = 64
A_HEADS = 4
A_QK_DIM = 64
A_V_DIM = 128
B_HEADS = 4
B_QK_DIM = 128
C_HEADS = 4
C_QK_DIM = 128
C_V_DIM = 256
C_GATE_RANK = 16
C_GATE_TAU = 16.0
M_HEADS = 4
M_HEAD_DIM = 256
REL_BUCKETS = 32
REL_MAX_DIST = 128
D_FF = 2816
ROPE_BASE = 10000.0
EPS = 1e-6
NEG_INF = -1e30

EVEN_IN = 3584
ODD_Z = 3072
LANES = 128
SUBLANES = 8
ODD_IN_PAD = ODD_Z + LANES
FF_CHUNK = 256
DOWN_GROUP = 6
ONES_ROWS = 16
LOG2_E = math.log2(math.e)
V7X_VMEM_LIMIT_BYTES = 56 * 1024 * 1024


def _params(n_axes):
    return pltpu.CompilerParams(dimension_semantics=("arbitrary",) * n_axes,
                                vmem_limit_bytes=V7X_VMEM_LIMIT_BYTES)


def _dot(a, b):
    return jnp.dot(a, b, preferred_element_type=F32)


def _dot_nt(a, b):
    return lax.dot_general(a, b, (((1,), (1,)), ((), ())), preferred_element_type=F32)


def _dot_tn(a, b):
    return lax.dot_general(a, b, (((0,), (0,)), ((), ())), preferred_element_type=F32)


def _rms(x, g):
    return x * lax.rsqrt(jnp.mean(x * x, axis=-1, keepdims=True) + EPS) * g


def _head_rms(x):
    return x * lax.rsqrt(jnp.mean(x * x, axis=-1, keepdims=True) + EPS)


def _silu(x):
    return x * (1.0 / (1.0 + jnp.exp(-x)))


def _gelu_tanh(x):
    c0 = math.sqrt(2.0 / math.pi)
    return x * (0.5 + 0.5 * jnp.tanh(x * (c0 + (c0 * 0.044715) * (x * x))))


def _log_sigmoid(x):
    return jnp.minimum(x, 0.0) - jnp.log1p(jnp.exp(-jnp.abs(x)))


def _store_head_major(o_ref, val, tm, n_heads, head_dim):
    tiles = head_dim // LANES
    group = n_heads * tiles
    for hd in range(n_heads):
        for part in range(tiles):
            col = hd * head_dim + part * LANES
            o_ref[pl.ds(part * n_heads + hd, tm, stride=group), :] = val[:, col:col + LANES]


def _in_even_kernel(x_ref, g_ref, w_ref, z_ref, k_ref, v_ref, *, tm):
    h = _rms(x_ref[...], g_ref[...]).astype(BF16)
    width = A_HEADS * A_V_DIM
    for c in range(EVEN_IN // width):
        zc = _dot(h, w_ref[:, c * width:(c + 1) * width])
        z_ref[:, c * width:(c + 1) * width] = zc.astype(BF16)
        if c == 1:
            _store_head_major(k_ref, zc, tm, A_HEADS, A_V_DIM)
        if c == 2:
            _store_head_major(v_ref, zc, tm, A_HEADS, A_V_DIM)


def _in_even(x2d, gamma, w, tm):
    n = x2d.shape[0]
    return pl.pallas_call(
        functools.partial(_in_even_kernel, tm=tm),
        grid=(n // tm,),
        in_specs=[pl.BlockSpec((tm, D_MODEL), lambda i: (i, 0)),
                  pl.BlockSpec((1, D_MODEL), lambda i: (0, 0)),
                  pl.BlockSpec((D_MODEL, EVEN_IN), lambda i: (0, 0))],
        out_specs=[pl.BlockSpec((tm, EVEN_IN), lambda i: (i, 0)),
                   pl.BlockSpec((tm * A_HEADS, A_V_DIM), lambda i: (i, 0)),
                   pl.BlockSpec((tm * A_HEADS, A_V_DIM), lambda i: (i, 0))],
        out_shape=[jax.ShapeDtypeStruct((n, EVEN_IN), BF16),
                   jax.ShapeDtypeStruct((n * A_HEADS, A_V_DIM), F32),
                   jax.ShapeDtypeStruct((n * A_HEADS, A_V_DIM), F32)],
        compiler_params=_params(1),
    )(x2d, gamma.reshape(1, D_MODEL), w)


def _in_odd_kernel(x_ref, g_ref, w_ref, wlr_ref, bg_ref, z_ref, la_ref):
    h = _rms(x_ref[...], g_ref[...]).astype(BF16)
    width = 512
    ca = _dot(h, w_ref[:, ODD_Z:ODD_IN_PAD]).astype(BF16)
    pre = _dot(ca, wlr_ref[...]) + bg_ref[...]
    la_ref[...] = _log_sigmoid(pre) / C_GATE_TAU
    for c in range(ODD_Z // width):
        z_ref[:, c * width:(c + 1) * width] = _dot(h, w_ref[:, c * width:(c + 1) * width]).astype(BF16)


def _in_odd(x2d, gamma, w_pad, wlr_pad, b_gate, tm):
    n = x2d.shape[0]
    qk = C_HEADS * C_QK_DIM
    return pl.pallas_call(
        _in_odd_kernel,
        grid=(n // tm,),
        in_specs=[pl.BlockSpec((tm, D_MODEL), lambda i: (i, 0)),
                  pl.BlockSpec((1, D_MODEL), lambda i: (0, 0)),
                  pl.BlockSpec((D_MODEL, ODD_IN_PAD), lambda i: (0, 0)),
                  pl.BlockSpec((LANES, qk), lambda i: (0, 0)),
                  pl.BlockSpec((1, qk), lambda i: (0, 0))],
        out_specs=[pl.BlockSpec((tm, ODD_Z), lambda i: (i, 0)),
                   pl.BlockSpec((tm, qk), lambda i: (i, 0))],
        out_shape=[jax.ShapeDtypeStruct((n, ODD_Z), BF16),
                   jax.ShapeDtypeStruct((n, qk), F32)],
        compiler_params=_params(1),
    )(x2d, gamma.reshape(1, D_MODEL), w_pad, wlr_pad, b_gate.reshape(1, qk))


def _mem_kv_kernel(x_ref, g_ref, wk_ref, wv_ref, k5_ref, v5_ref, kb_ref, vb_ref, *, seqs, mem_len):
    h = _rms(x_ref[...], g_ref[0]).astype(BF16)
    for w_ref, o5_ref, ob_ref in ((wk_ref, k5_ref, kb_ref), (wv_ref, v5_ref, vb_ref)):
        y = _dot(h, w_ref[0])
        ob_ref[0] = y.astype(BF16)
        for s in range(seqs):
            for hd in range(M_HEADS):
                o5_ref[0, s, :, hd, :] = y[s * mem_len:(s + 1) * mem_len,
                                           hd * M_HEAD_DIM:(hd + 1) * M_HEAD_DIM]


def _mem_kv(mem, ln_mem, wk, wv, seqs):
    bsz, mem_len, _ = mem.shape
    depth = ln_mem.shape[0]
    width = M_HEADS * M_HEAD_DIM
    n = bsz * mem_len
    tm = seqs * mem_len
    out5 = jax.ShapeDtypeStruct((depth, bsz, mem_len, M_HEADS, M_HEAD_DIM), F32)
    outb = jax.ShapeDtypeStruct((depth, n, width), BF16)
    spec5 = pl.BlockSpec((1, seqs, mem_len, M_HEADS, M_HEAD_DIM), lambda l, i: (l, i, 0, 0, 0))
    specb = pl.BlockSpec((1, tm, width), lambda l, i: (l, i, 0))
    wspec = pl.BlockSpec((1, D_MODEL, width), lambda l, i: (l, 0, 0))
    return pl.pallas_call(
        functools.partial(_mem_kv_kernel, seqs=seqs, mem_len=mem_len),
        grid=(depth, n // tm),
        in_specs=[pl.BlockSpec((tm, D_MODEL), lambda l, i: (i, 0)),
                  pl.BlockSpec((1, 1, D_MODEL), lambda l, i: (l, 0, 0)),
                  wspec, wspec],
        out_specs=[spec5, spec5, specb, specb],
        out_shape=[out5, out5, outb, outb],
        compiler_params=_params(2),
    )(mem.reshape(n, D_MODEL), ln_mem.reshape(depth, 1, D_MODEL), wk, wv)


def _diff_attn_kernel(lam_ref, q_ref, k_ref, v_ref, bias_ref, subln_ref, o_ref,
                      vt_sc, s_sc, acc_sc, *, tq, nd, hp, out_scale):
    i = pl.program_id(2)
    dv = A_V_DIM
    chains = [(h, c) for h in range(hp) for c in range(2)]
    n_ch = len(chains)

    @pl.when(i == 0)
    def _():
        for h in range(hp):
            for jj in range(nd):
                vt_sc[h, jj, :dv, :] = v_ref[0, jj * tq:(jj + 1) * tq, h * dv:(h + 1) * dv].T
                vt_sc[h, jj, dv:, :] = jnp.ones((ONES_ROWS, tq), BF16)

    row = lax.broadcasted_iota(jnp.int32, (dv, tq), 0)
    qzt = []
    for h in range(hp):
        qt = (q_ref[0, :, h * dv:(h + 1) * dv] * (A_QK_DIM ** -0.5)).T
        for c in range(2):
            qzt.append(jnp.where((row >= A_QK_DIM) == (c == 1), qt, jnp.zeros_like(qt)))

    def raw_scores(j, n):
        h = chains[n][0]
        start = pl.multiple_of(j * tq, tq)
        return _dot(k_ref[0, pl.ds(start, tq), h * dv:(h + 1) * dv], qzt[n])

    def scaled(j, n, raw):
        return (raw + bias_ref[chains[n][0], j - i + (nd - 1)]) * LOG2_E

    def pass1(j, ms):
        new = []
        raw_next = raw_scores(j, 0)
        for n in range(n_ch):
            raw = raw_next
            if n + 1 < n_ch:
                raw_next = raw_scores(j, n + 1)
            s = scaled(j, n, raw)
            s_sc[n, j] = s
            new.append(jnp.maximum(ms[n], jnp.max(s, axis=0, keepdims=True)))
        return tuple(new)

    ms = lax.fori_loop(0, i, pass1, tuple(jnp.full((1, tq), NEG_INF, F32) for _ in range(n_ch)))
    kk = lax.broadcasted_iota(jnp.int32, (tq, tq), 0)
    qq = lax.broadcasted_iota(jnp.int32, (tq, tq), 1)
    allowed = (kk // CHUNK) <= (qq // CHUNK)
    m_fin = []
    for n in range(n_ch):
        s = jnp.where(allowed, scaled(i, n, raw_scores(i, n)), NEG_INF)
        s_sc[n, i] = s
        m_fin.append(jnp.maximum(ms[n], jnp.max(s, axis=0, keepdims=True)))

    acc_sc[...] = jnp.zeros(acc_sc.shape, F32)

    def pass2(j, carry):
        p_next = jnp.exp2(s_sc[0, j] - m_fin[0])
        for n, (h, c) in enumerate(chains):
            p = p_next
            if n + 1 < n_ch:
                p_next = jnp.exp2(s_sc[n + 1, j] - m_fin[n + 1])
            acc_sc[n] += _dot(vt_sc[h, j], p.astype(BF16))
        return carry

    lax.fori_loop(0, i + 1, pass2, 0)
    for h in range(hp):
        acc0 = acc_sc[2 * h]
        acc1 = acc_sc[2 * h + 1]
        out0 = acc0[:dv] / acc0[dv:dv + 1]
        out1 = acc1[:dv] / acc1[dv:dv + 1]
        o = (out0 - lam_ref[0, 0] * out1).T
        o_ref[:, h * dv:(h + 1) * dv] = (_head_rms(o) * subln_ref[...] * out_scale).astype(BF16)


def _diff_attn(lam, z3, bias_tiles, subln, tq, hp, out_scale):
    bsz, t, _ = z3.shape
    nd = t // tq
    width = hp * A_V_DIM
    groups = A_HEADS // hp
    return pl.pallas_call(
        functools.partial(_diff_attn_kernel, tq=tq, nd=nd, hp=hp, out_scale=out_scale),
        grid=(bsz, groups, nd),
        in_specs=[pl.BlockSpec(memory_space=pltpu.SMEM),
                  pl.BlockSpec((1, tq, width), lambda b, g, i: (b, i, g)),
                  pl.BlockSpec((1, t, width), lambda b, g, i: (b, 0, groups + g)),
                  pl.BlockSpec((1, t, width), lambda b, g, i: (b, 0, 2 * groups + g)),
                  pl.BlockSpec((hp, nd, tq, tq), lambda b, g, i: (g, 0, 0, 0),
                               pipeline_mode=pl.Buffered(1)),
                  pl.BlockSpec((1, A_V_DIM), lambda b, g, i: (0, 0))],
        out_specs=pl.BlockSpec((tq, width), lambda b, g, i: (b * nd + i, g)),
        out_shape=jax.ShapeDtypeStruct((bsz * t, A_HEADS * A_V_DIM), BF16),
        scratch_shapes=[pltpu.VMEM((hp, nd, A_V_DIM + ONES_ROWS, tq), BF16),
                        pltpu.VMEM((2 * hp, nd, tq, tq), F32),
                        pltpu.VMEM((2 * hp, A_V_DIM + ONES_ROWS, tq), F32)],
        compiler_params=_params(3),
    )(lam, z3, z3, z3, bias_tiles, subln.reshape(1, A_V_DIM))


def _diff_attn_cached_kernel(lam_ref, q_ref, kn_ref, vn_ref, kp_ref, vp_ref, bp_ref, bn_ref,
                             subln_ref, o_ref, *, t, past, out_scale):
    lane = lax.broadcasted_iota(jnp.int32, (t, A_V_DIM), 1)
    qpos_p = past + lax.broadcasted_iota(jnp.int32, (2 * t, past), 0) % t
    kpos_p = lax.broadcasted_iota(jnp.int32, (2 * t, past), 1)
    ok_p = (kpos_p // CHUNK) <= (qpos_p // CHUNK)
    qpos_n = past + lax.broadcasted_iota(jnp.int32, (2 * t, t), 0) % t
    kpos_n = past + lax.broadcasted_iota(jnp.int32, (2 * t, t), 1)
    ok_n = (kpos_n // CHUNK) <= (qpos_n // CHUNK)
    for h in range(A_HEADS):
        cs = slice(h * A_V_DIM, (h + 1) * A_V_DIM)
        q = q_ref[0, :, cs] * (A_QK_DIM ** -0.5)
        zero = jnp.zeros_like(q)
        qp = jnp.concatenate([jnp.where(lane < A_QK_DIM, q, zero),
                              jnp.where(lane >= A_QK_DIM, q, zero)], axis=0)
        kp = kp_ref[0, :, cs].astype(BF16)
        vp = vp_ref[0, :, cs].astype(BF16)
        kn = kn_ref[0, :, cs]
        vn = vn_ref[0, :, cs]
        bp = bp_ref[h]
        bn = bn_ref[h]
        sp = jnp.where(ok_p, _dot_nt(qp, kp) + jnp.concatenate([bp, bp], axis=0), NEG_INF)
        sn = jnp.where(ok_n, _dot_nt(qp, kn) + jnp.concatenate([bn, bn], axis=0), NEG_INF)
        m = jnp.maximum(jnp.max(sp, axis=-1, keepdims=True), jnp.max(sn, axis=-1, keepdims=True))
        pp = jnp.exp(sp - m)
        pn = jnp.exp(sn - m)
        l = jnp.sum(pp, axis=-1, keepdims=True) + jnp.sum(pn, axis=-1, keepdims=True)
        out = (_dot(pp.astype(BF16), vp) + _dot(pn.astype(BF16), vn)) / l
        o = out[:t] - lam_ref[0, 0] * out[t:]
        o_ref[:, cs] = (_head_rms(o) * subln_ref[...] * out_scale).astype(BF16)


def _diff_attn_cached(lam, z3, past_k, past_v, bias_past, bias_new, subln, out_scale):
    bsz, t, _ = z3.shape
    past = past_k.shape[1]
    width = A_HEADS * A_V_DIM
    return pl.pallas_call(
        functools.partial(_diff_attn_cached_kernel, t=t, past=past, out_scale=out_scale),
        grid=(bsz,),
        in_specs=[pl.BlockSpec(memory_space=pltpu.SMEM),
                  pl.BlockSpec((1, t, width), lambda b: (b, 0, 0)),
                  pl.BlockSpec((1, t, width), lambda b: (b, 0, 1)),
                  pl.BlockSpec((1, t, width), lambda b: (b, 0, 2)),
                  pl.BlockSpec((1, past, width), lambda b: (b, 0, 0)),
                  pl.BlockSpec((1, past, width), lambda b: (b, 0, 0)),
                  pl.BlockSpec((A_HEADS, t, past), lambda b: (0, 0, 0)),
                  pl.BlockSpec((A_HEADS, t, t), lambda b: (0, 0, 0)),
                  pl.BlockSpec((1, A_V_DIM), lambda b: (0, 0))],
        out_specs=pl.BlockSpec((t, width), lambda b: (b, 0)),
        out_shape=jax.ShapeDtypeStruct((bsz * t, width), BF16),
        compiler_params=_params(1),
    )(lam, z3, z3, z3, past_k, past_v, bias_past, bias_new, subln.reshape(1, A_V_DIM))


def _retention_kernel(q_ref, k_ref, v_ref, gt_ref, cos_ref, sin_ref, dec_ref, qd_ref, kd_ref,
                      cd_ref, s0_ref, o_ref, s_out_ref, s_sc):
    t = pl.program_id(1)

    @pl.when(t == 0)
    def _():
        s_sc[...] = s0_ref[0]

    d = B_QK_DIM
    half = d // 2
    cos = cos_ref[...]
    sin = sin_ref[...]
    for h in range(B_HEADS):
        cs = slice(h * d, (h + 1) * d)
        q = q_ref[0, :, cs].astype(F32)
        k = k_ref[0, :, cs].astype(F32)
        qr = (q * cos + pltpu.roll(q, half, 1) * sin) * (d ** -0.5)
        kr = k * cos + pltpu.roll(k, half, 1) * sin
        v = v_ref[0, :, cs]
        att = _dot_nt(qr.astype(BF16), kr.astype(BF16)) * dec_ref[h]
        s = s_sc[h]
        o = _dot(att.astype(BF16), v) + _dot((qr * qd_ref[h]).astype(BF16), s.astype(BF16))
        s_sc[h] = s * cd_ref[h] + _dot_tn((kr * kd_ref[h]).astype(BF16), v)
        gt = gt_ref[0, :, cs].astype(F32)
        o_ref[:, cs] = (_head_rms(o) * _silu(gt)).astype(BF16)

    @pl.when(t == pl.num_programs(1) - 1)
    def _():
        s_out_ref[0] = s_sc[...]


def _retention(z3, cos, sin, consts, s0, c):
    bsz, t, _ = z3.shape
    nt = t // c
    dec, qd, kd, cd = consts
    d = B_QK_DIM
    width = B_HEADS * d
    base = 3
    full3 = lambda b, i: (0, 0, 0)
    return pl.pallas_call(
        _retention_kernel,
        grid=(bsz, nt),
        in_specs=[pl.BlockSpec((1, c, width), lambda b, i: (b, i, base)),
                  pl.BlockSpec((1, c, width), lambda b, i: (b, i, base + 1)),
                  pl.BlockSpec((1, c, width), lambda b, i: (b, i, base + 2)),
                  pl.BlockSpec((1, c, width), lambda b, i: (b, i, base + 3)),
                  pl.BlockSpec((c, d), lambda b, i: (i, 0)),
                  pl.BlockSpec((c, d), lambda b, i: (i, 0)),
                  pl.BlockSpec((B_HEADS, c, c), full3),
                  pl.BlockSpec((B_HEADS, c, d), full3),
                  pl.BlockSpec((B_HEADS, c, d), full3),
                  pl.BlockSpec((B_HEADS, 1, d), full3),
                  pl.BlockSpec((1, B_HEADS, d, d), lambda b, i: (b, 0, 0, 0))],
        out_specs=[pl.BlockSpec((c, width), lambda b, i: (b * nt + i, 0)),
                   pl.BlockSpec((1, B_HEADS, d, d), lambda b, i: (b, 0, 0, 0))],
        out_shape=[jax.ShapeDtypeStruct((bsz * t, width), BF16),
                   jax.ShapeDtypeStruct((bsz, B_HEADS, d, d), F32)],
        scratch_shapes=[pltpu.VMEM((B_HEADS, d, d), F32)],
        compiler_params=_params(2),
    )(z3, z3, z3, z3, cos, sin, dec, qd, kd, cd, s0)


def _retention_consts(c):
    log_g = jnp.log1p(-jnp.exp2(-5.0 - jnp.arange(B_HEADS, dtype=F32)))
    idx = jnp.arange(c, dtype=F32)
    dist = idx[:, None] - idx[None, :]
    dec = jnp.where(dist >= 0, jnp.exp(jnp.maximum(dist, 0.0)[None] * log_g[:, None, None]), 0.0)
    qd = jnp.exp((idx[None, :] + 1.0) * log_g[:, None])
    kd = jnp.exp((c - 1.0 - idx)[None, :] * log_g[:, None])
    cd = jnp.exp(c * log_g)
    bc = lambda a: jnp.broadcast_to(a[..., None], a.shape + (B_QK_DIM,))
    return dec, bc(qd), bc(kd), bc(cd[:, None])


def _rotary_tables(pos):
    half = B_QK_DIM // 2
    inv = ROPE_BASE ** (-jnp.arange(half, dtype=F32) / half)
    ang = pos.astype(F32)[:, None] * inv[None, :]
    cos, sin = jnp.cos(ang), jnp.sin(ang)
    return jnp.concatenate([cos, cos], axis=-1), jnp.concatenate([-sin, sin], axis=-1)


def _gla_levels(c):
    return [c >> (l + 1) for l in range(int(math.log2(c)))]


def _gla_consts(c):
    levels = _gla_levels(c)
    rows = np.arange(c)
    mats = []
    for s in levels:
        ref = (rows // (2 * s)) * 2 * s + s - 1
        a = np.zeros((c, c), np.float32)
        for i in range(c):
            if i & s:
                a[i, ref[i] + 1:i + 1] = 1.0
            else:
                a[i, i + 1:ref[i] + 1] = 1.0
        mats.append(a)
    mats.append(np.tril(np.ones((c, c), np.float32)))
    mats.append(np.triu(np.ones((c, c), np.float32), 1))
    lv = np.full((c, c), -1, np.int32)
    for i in range(c):
        lv[i, i] = len(levels)
        for j in range(i):
            lv[i, j] = levels.index(1 << int(math.floor(math.log2(i ^ j))))
    return jnp.asarray(np.concatenate(mats, axis=0), BF16), jnp.asarray(lv)


def _gla_kernel(q_ref, k_ref, v_ref, r_ref, g_ref, s0_ref, a_ref, lv_ref, nw_ref,
                o_ref, s_out_ref, st_sc, *, c):
    t = pl.program_id(1)
    dk, dv = C_QK_DIM, C_V_DIM

    @pl.when(t == 0)
    def _():
        for h in range(C_HEADS):
            st_sc[h] = s0_ref[0, h].T

    levels = _gla_levels(c)
    n_lv = len(levels)
    lv = lv_ref[...]
    row = lax.broadcasted_iota(jnp.int32, (c, dk), 0)
    g = g_ref[0]
    xs = _dot(a_ref[...], g.astype(BF16))
    for h in range(C_HEADS):
        ks = slice(h * dk, (h + 1) * dk)
        vs = slice(h * dv, (h + 1) * dv)
        x = xs[:, ks]
        q = q_ref[0, :, ks].astype(F32) * (dk ** -0.5)
        k = k_ref[0, :, ks].astype(F32)
        v = v_ref[0, :, vs]
        att = jnp.zeros((c, c), F32)
        pending = (n_lv, _dot_nt(q.astype(BF16), k.astype(BF16)))
        for l, s in enumerate(levels):
            e = jnp.exp(x[l * c:(l + 1) * c])
            up = (row & s) != 0
            mix = jnp.where(up, q, k) * e
            qt = jnp.where(up, mix, 0.0).astype(BF16)
            kt = jnp.where(up, 0.0, mix).astype(BF16)
            prod = _dot_nt(qt, kt)
            att = jnp.where(lv == pending[0], pending[1], att)
            pending = (l, prod)
        att = jnp.where(lv == pending[0], pending[1], att)
        b = x[n_lv * c:(n_lv + 1) * c]
        rem = x[(n_lv + 1) * c:(n_lv + 2) * c]
        st = st_sc[h]
        o = _dot(att.astype(BF16), v) + _dot_nt((q * jnp.exp(b)).astype(BF16), st.astype(BF16))
        kd = (k * jnp.exp(rem)).astype(BF16)
        st_sc[h] = st * jnp.exp(b[c - 1:c, :]) + _dot_tn(v, kd)
        r = r_ref[0, :, vs].astype(F32)
        o_ref[:, vs] = (_head_rms(o) * nw_ref[...] * _silu(r)).astype(BF16)

    @pl.when(t == pl.num_programs(1) - 1)
    def _():
        for h in range(C_HEADS):
            s_out_ref[0, h] = st_sc[h].T


def _gla(z3, log_a3, s0, a_stack, lv, norm_w, c):
    bsz, t, _ = z3.shape
    nt = t // c
    dk, dv = C_QK_DIM, C_V_DIM
    qk_w, v_w = C_HEADS * dk, C_HEADS * dv
    const = lambda b, i: (0, 0)
    return pl.pallas_call(
        functools.partial(_gla_kernel, c=c),
        grid=(bsz, nt),
        in_specs=[pl.BlockSpec((1, c, qk_w), lambda b, i: (b, i, 0)),
                  pl.BlockSpec((1, c, qk_w), lambda b, i: (b, i, 1)),
                  pl.BlockSpec((1, c, v_w), lambda b, i: (b, i, 1)),
                  pl.BlockSpec((1, c, v_w), lambda b, i: (b, i, 2)),
                  pl.BlockSpec((1, c, qk_w), lambda b, i: (b, i, 0)),
                  pl.BlockSpec((1, C_HEADS, dk, dv), lambda b, i: (b, 0, 0, 0)),
                  pl.BlockSpec(a_stack.shape, const),
                  pl.BlockSpec((c, c), const),
                  pl.BlockSpec((1, dv), const)],
        out_specs=[pl.BlockSpec((c, v_w), lambda b, i: (b * nt + i, 0)),
                   pl.BlockSpec((1, C_HEADS, dk, dv), lambda b, i: (b, 0, 0, 0))],
        out_shape=[jax.ShapeDtypeStruct((bsz * t, v_w), BF16),
                   jax.ShapeDtypeStruct((bsz, C_HEADS, dk, dv), F32)],
        scratch_shapes=[pltpu.VMEM((C_HEADS, dv, dk), F32)],
        compiler_params=_params(2),
    )(z3, z3, z3, z3, log_a3, s0, a_stack, lv, norm_w.reshape(1, dv))


def _cross_kernel(*refs, n_mix):
    x_ref = refs[0]
    a_refs = refs[1:1 + n_mix]
    w_refs = refs[1 + n_mix:1 + 2 * n_mix]
    g_ref, wq_ref, mk_ref, mv_ref, wo_ref, o_ref = refs[1 + 2 * n_mix:]
    x = x_ref[...]
    for a_ref, w_ref in zip(a_refs, w_refs):
        x = x + _dot(a_ref[...], w_ref[...])
    h = _rms(x, g_ref[...]).astype(BF16)
    q = (_dot(h, wq_ref[...]) * (M_HEAD_DIM ** -0.5)).astype(BF16)
    mk = mk_ref[0].astype(BF16)
    mv = mv_ref[0].astype(BF16)
    outs = []
    for hd in range(M_HEADS):
        cs = slice(hd * M_HEAD_DIM, (hd + 1) * M_HEAD_DIM)
        s = _dot_nt(q[:, cs], mk[:, cs])
        p = jnp.exp(s - jnp.max(s, axis=-1, keepdims=True))
        l = jnp.sum(p, axis=-1, keepdims=True)
        outs.append((_dot(p.astype(BF16), mv[:, cs]) / l).astype(BF16))
    o_ref[...] = x + _dot(jnp.concatenate(outs, axis=1), wo_ref[...])


def _cross(x2d, mix_list, w_mix_list, gamma, wq, mk3, mv3, wo, t, tm):
    n = x2d.shape[0]
    nt = t // tm
    mem_len, width = mk3.shape[1], mk3.shape[2]
    rows = lambda b, i: (b * nt + i, 0)
    const = lambda b, i: (0, 0)
    in_specs = [pl.BlockSpec((tm, D_MODEL), rows)]
    in_specs += [pl.BlockSpec((tm, a.shape[1]), rows) for a in mix_list]
    in_specs += [pl.BlockSpec(w.shape, const) for w in w_mix_list]
    in_specs += [pl.BlockSpec((1, D_MODEL), const),
                 pl.BlockSpec((D_MODEL, width), const),
                 pl.BlockSpec((1, mem_len, width), lambda b, i: (b, 0, 0)),
                 pl.BlockSpec((1, mem_len, width), lambda b, i: (b, 0, 0)),
                 pl.BlockSpec((width, D_MODEL), const)]
    return pl.pallas_call(
        functools.partial(_cross_kernel, n_mix=len(mix_list)),
        grid=(n // t, nt),
        in_specs=in_specs,
        out_specs=pl.BlockSpec((tm, D_MODEL), rows),
        out_shape=jax.ShapeDtypeStruct((n, D_MODEL), F32),
        compiler_params=_params(2),
    )(x2d, *mix_list, *w_mix_list, gamma.reshape(1, D_MODEL), wq, mk3, mv3, wo)


def _ffn_kernel(x_ref, g_ref, wg_ref, wu_ref, cw_ref, cb_ref, prev_ref, wd_ref, *rest,
                tm, final_norm):
    if final_norm:
        gf_ref, o_ref, tail_ref, carry = rest
    else:
        o_ref, tail_ref, carry = rest
    t = pl.program_id(1)
    lo = SUBLANES - 2

    @pl.when(t == 0)
    def _():
        carry[lo:SUBLANES, :] = prev_ref[0]

    x = x_ref[...]
    h = _rms(x, g_ref[...]).astype(BF16)
    row = lax.broadcasted_iota(jnp.int32, (SUBLANES, FF_CHUNK), 0)
    acc = x
    n_chunks = D_FF // FF_CHUNK
    chunk = lambda c: slice(c * FF_CHUNK, (c + 1) * FF_CHUNK)
    nxt = (_dot(h, wg_ref[:, chunk(0)]), _dot(h, wu_ref[:, chunk(0)]))
    pending = []
    for c in range(n_chunks):
        cs = chunk(c)
        gate, up = nxt
        if c + 1 < n_chunks:
            nxt = (_dot(h, wg_ref[:, chunk(c + 1)]), _dot(h, wu_ref[:, chunk(c + 1)]))
        p1 = carry[SUBLANES - 1:SUBLANES, cs]
        p2 = carry[lo:lo + 1, cs]
        r1 = pltpu.roll(gate, 1, 0)
        r2 = pltpu.roll(gate, 2, 0)
        h1 = jnp.where(row == 0, p1, r1[:SUBLANES])
        h2 = jnp.where(row == 0, p2, jnp.where(row == 1, p1, r2[:SUBLANES]))
        g1 = jnp.concatenate([h1, r1[SUBLANES:]], axis=0)
        g2 = jnp.concatenate([h2, r2[SUBLANES:]], axis=0)
        carry[:, cs] = gate[tm - SUBLANES:tm]
        conv = cb_ref[:, cs] + cw_ref[0:1, cs] * g2
        conv = conv + cw_ref[1:2, cs] * g1
        conv = conv + cw_ref[2:3, cs] * gate
        pending.append((_gelu_tanh(conv) * up).astype(BF16))
        if len(pending) == DOWN_GROUP or c == n_chunks - 1:
            lo_col = (c + 1 - len(pending)) * FF_CHUNK
            acc = acc + _dot(jnp.concatenate(pending, axis=1), wd_ref[lo_col:(c + 1) * FF_CHUNK, :])
            pending = []
    if final_norm:
        o_ref[...] = _rms(acc, gf_ref[...])
    else:
        o_ref[...] = acc

    @pl.when(t == pl.num_programs(1) - 1)
    def _():
        tail_ref[0] = carry[lo:SUBLANES, :]


def _ffn(x2d, gamma, wg, wu, conv_w, conv_b, prev, wd, t, tm, final_gamma=None):
    n = x2d.shape[0]
    nt = t // tm
    bsz = n // t
    const = lambda b, i: (0, 0)
    resident = dict(pipeline_mode=pl.Buffered(1))
    in_specs = [pl.BlockSpec((tm, D_MODEL), lambda b, i: (b * nt + i, 0)),
                pl.BlockSpec((1, D_MODEL), const),
                pl.BlockSpec((D_MODEL, D_FF), const, **resident),
                pl.BlockSpec((D_MODEL, D_FF), const, **resident),
                pl.BlockSpec((3, D_FF), const),
                pl.BlockSpec((1, D_FF), const),
                pl.BlockSpec((1, 2, D_FF), lambda b, i: (b, 0, 0)),
                pl.BlockSpec((D_FF, D_MODEL), const, **resident)]
    args = [x2d, gamma.reshape(1, D_MODEL), wg, wu, conv_w, conv_b.reshape(1, D_FF), prev, wd]
    if final_gamma is not None:
        in_specs.append(pl.BlockSpec((1, D_MODEL), const))
        args.append(final_gamma.reshape(1, D_MODEL))
    return pl.pallas_call(
        functools.partial(_ffn_kernel, tm=tm, final_norm=final_gamma is not None),
        grid=(bsz, nt),
        in_specs=in_specs,
        out_specs=[pl.BlockSpec((tm, D_MODEL), lambda b, i: (b * nt + i, 0)),
                   pl.BlockSpec((1, 2, D_FF), lambda b, i: (b, 0, 0))],
        out_shape=[jax.ShapeDtypeStruct((n, D_MODEL), F32),
                   jax.ShapeDtypeStruct((bsz, 2, D_FF), F32)],
        scratch_shapes=[pltpu.VMEM((SUBLANES, D_FF), F32)],
        compiler_params=_params(2),
    )(*args)


def _rel_bucket(rel):
    nb = REL_BUCKETS // 2
    max_exact = nb // 2
    n = jnp.abs(rel)
    nf = jnp.maximum(n, 1).astype(F32)
    large = max_exact + (jnp.log(nf / max_exact) / math.log(REL_MAX_DIST / max_exact)
                         * (nb - max_exact)).astype(jnp.int32)
    large = jnp.minimum(large, nb - 1)
    return jnp.where(rel > 0, nb, 0) + jnp.where(n < max_exact, n, large)


def _bias_tiles(rel_bias, tq, nd):
    period = 2 * tq
    n = np.arange(period)
    rel = np.stack([np.where(n < tq, (d - (nd - 1)) * tq - n, (d - (nd - 1)) * tq + period - n)
                    for d in range(nd)]).astype(np.int32)
    w = jnp.transpose(rel_bias[_rel_bucket(jnp.asarray(rel))], (2, 0, 1)).astype(F32)
    flat = jnp.tile(w, (1, 1, tq))[:, :, :tq * (period - 1)]
    return flat.reshape(A_HEADS, nd, tq, period - 1)[:, :, :, :tq]


def _bias_rows(rel_bias, t, past):
    rel = (np.arange(past + t)[None, :] - (past + np.arange(t))[:, None]).astype(np.int32)
    bias = jnp.transpose(rel_bias[_rel_bucket(jnp.asarray(rel))], (2, 0, 1)).astype(F32)
    return bias[:, :, :past], bias[:, :, past:]


def _forward(x, past_k, past_v, ret_state, gla_state, conv_prev, mem_k, mem_v, p, w):
    bsz, t, _ = x.shape
    n = bsz * t
    past = 0 if past_k is None else past_k.shape[2]
    tm = min(512, n)
    tseq = min(512, t)
    tffn = min(256, t)
    tq = min(512, t)
    c_ret = min(256, t)
    c_gla = min(256, t)
    x2 = x.reshape(n, D_MODEL)
    pos = past + jnp.arange(t, dtype=jnp.int32)
    new_conv = []

    z, ak, av = _in_even(x2, p["ln_mix"][0], w["w_in_even"], tm)
    z3 = z.reshape(bsz, t, EVEN_IN)
    lam_init = 0.8 - 0.6 * math.exp(-0.3 * 0)
    lam = (jnp.exp(jnp.sum(p["diff_lq1"][0].astype(F32) * p["diff_lk1"][0].astype(F32)))
           - jnp.exp(jnp.sum(p["diff_lq2"][0].astype(F32) * p["diff_lk2"][0].astype(F32)))
           + lam_init).reshape(1, 1).astype(F32)
    if past == 0:
        o_a = _diff_attn(lam, z3, _bias_tiles(p["rel_bias"], tq, t // tq), p["diff_subln"][0],
                         tq, 2, 1.0 - lam_init)
    else:
        width = A_HEADS * A_V_DIM
        bias_past, bias_new = _bias_rows(p["rel_bias"], t, past)
        o_a = _diff_attn_cached(lam, z3, past_k[0].reshape(bsz, past, width),
                                past_v[0].reshape(bsz, past, width), bias_past, bias_new,
                                p["diff_subln"][0], 1.0 - lam_init)
    cos, sin = _rotary_tables(pos)
    o_b, s_ret = _retention(z3, cos, sin, _retention_consts(c_ret), ret_state[0], c_ret)
    x2 = _cross(x2, [o_a, o_b], [w["w_out_even_a"], w["w_out_even_b"]], p["ln_cross"][0],
                w["w_cq"][0], mem_k[0], mem_v[0], w["w_co"][0], t, tseq)
    x2, tail = _ffn(x2, p["ln_ffn"][0], w["w_ffn_gate"][0], w["w_ffn_up"][0], p["ffn_conv_w"][0],
                    p["ffn_conv_b"][0], conv_prev[0], w["w_ffn_down"][0], t, tffn)
    new_conv.append(tail)

    zc, log_a = _in_odd(x2, p["ln_mix"][1], w["w_in_odd"], w["w_gate_lr"], p["b_gate"][0], tm)
    a_stack, lv = _gla_consts(c_gla)
    o_c, s_gla = _gla(zc.reshape(bsz, t, ODD_Z), log_a.reshape(bsz, t, C_HEADS * C_QK_DIM),
                      gla_state[0], a_stack, lv, p["gla_norm"][0], c_gla)
    x2 = _cross(x2, [o_c], [w["w_out_odd"]], p["ln_cross"][1],
                w["w_cq"][1], mem_k[1], mem_v[1], w["w_co"][1], t, tseq)
    y, tail = _ffn(x2, p["ln_ffn"][1], w["w_ffn_gate"][1], w["w_ffn_up"][1], p["ffn_conv_w"][1],
                   p["ffn_conv_b"][1], conv_prev[1], w["w_ffn_down"][1], t, tffn,
                   final_gamma=p["ln_final"])
    new_conv.append(tail)

    return (y.reshape(bsz, t, D_MODEL),
            ak.reshape(1, bsz, t, A_HEADS, A_V_DIM), av.reshape(1, bsz, t, A_HEADS, A_V_DIM),
            s_ret[None], s_gla[None], jnp.stack(new_conv))


def kernel(x_prompt, x_sample, cache_diff_k, cache_diff_v, state_retention, state_gla, cache_ffn_conv, cache_mem_k, cache_mem_v, mem_prompt, ln_mix, ln_cross, ln_ffn, ln_mem, ln_final, w_in_even, w_out_even, diff_lq1, diff_lk1, diff_lq2, diff_lk2, diff_subln, rel_bias, w_in_odd, w_gate_lr, b_gate, gla_norm, w_out_odd, w_cq, w_ck, w_cv, w_co, w_ffn_gate, w_ffn_up, ffn_conv_w, ffn_conv_b, w_ffn_down):
    p = dict(ln_mix=ln_mix, ln_cross=ln_cross, ln_ffn=ln_ffn, ln_final=ln_final,
             diff_lq1=diff_lq1, diff_lk1=diff_lk1, diff_lq2=diff_lq2, diff_lk2=diff_lk2,
             diff_subln=diff_subln, rel_bias=rel_bias, b_gate=b_gate, gla_norm=gla_norm,
             ffn_conv_w=ffn_conv_w, ffn_conv_b=ffn_conv_b)
    a_v = A_HEADS * A_V_DIM
    w = dict(
        w_in_even=w_in_even[0].astype(BF16),
        w_out_even_a=w_out_even[0, :a_v].astype(BF16),
        w_out_even_b=w_out_even[0, a_v:].astype(BF16),
        w_in_odd=jnp.pad(w_in_odd[0], ((0, 0), (0, ODD_IN_PAD - w_in_odd.shape[2]))).astype(BF16),
        w_gate_lr=jnp.pad(w_gate_lr[0], ((0, LANES - C_GATE_RANK), (0, 0))).astype(BF16),
        w_out_odd=w_out_odd[0].astype(BF16),
        w_cq=w_cq.astype(BF16), w_co=w_co.astype(BF16),
        w_ffn_gate=w_ffn_gate.astype(BF16), w_ffn_up=w_ffn_up.astype(BF16),
        w_ffn_down=w_ffn_down.astype(BF16))

    bp, mem_len, _ = mem_prompt.shape
    bs = x_sample.shape[0]
    m_width = M_HEADS * M_HEAD_DIM
    depth = ln_mem.shape[0]
    mem_k_p, mem_v_p, mk_b, mv_b = _mem_kv(mem_prompt, ln_mem, w_ck.astype(BF16),
                                           w_cv.astype(BF16), seqs=2)

    dt = x_prompt.dtype
    zero_ret = jnp.zeros((1, bp, B_HEADS, B_QK_DIM, B_QK_DIM), dt)
    zero_gla = jnp.zeros((1, bp, C_HEADS, C_QK_DIM, C_V_DIM), dt)
    zero_conv = jnp.zeros((depth, bp, 2, D_FF), dt)
    y_p, dk_p, dv_p, ret_p, gla_p, conv_p = _forward(
        x_prompt, None, None, zero_ret, zero_gla, zero_conv,
        mk_b.reshape(depth, bp, mem_len, m_width), mv_b.reshape(depth, bp, mem_len, m_width), p, w)
    y_s, dk_s, dv_s, ret_s, gla_s, conv_s = _forward(
        x_sample, cache_diff_k, cache_diff_v, state_retention, state_gla, cache_ffn_conv,
        cache_mem_k.reshape(depth, bs, mem_len, m_width),
        cache_mem_v.reshape(depth, bs, mem_len, m_width), p, w)
    return (y_p, y_s, dk_p, dv_p, ret_p, gla_p, conv_p, mem_k_p, mem_v_p,
            dk_s, dv_s, ret_s, gla_s, conv_s)
```

```python
import functools
import math

import numpy as np
import jax
import jax.numpy as jnp
from jax import lax
from jax.experimental import pallas as pl
from jax.experimental.pallas import tpu as pltpu

F32 = jnp.float32
BF16 = jnp.bfloat16

D_MODEL = 1024
CHUNK = 64
A_HEADS = 4
A_QK_DIM = 64
A_V_DIM = 128
B_HEADS = 4
B_QK_DIM = 128
C_HEADS = 4
C_QK_DIM = 128
C_V_DIM = 256
C_GATE_RANK = 16
C_GATE_TAU = 16.0
M_HEADS = 4
M_HEAD_DIM = 256
REL_BUCKETS = 32
REL_MAX_DIST = 128
D_FF = 2816
ROPE_BASE = 10000.0
EPS = 1e-6
NEG_INF = -1e30

EVEN_IN = 3584
ODD_Z = 3072
LANES = 128
SUBLANES = 8
ODD_IN_PAD = ODD_Z + LANES
FF_CHUNK = 256
DOWN_GROUP = 6
ONES_ROWS = 16
LOG2_E = math.log2(math.e)
V7X_VMEM_LIMIT_BYTES = 56 * 1024 * 1024


def _params(n_axes):
    return pltpu.CompilerParams(dimension_semantics=("arbitrary",) * n_axes,
                                vmem_limit_bytes=V7X_VMEM_LIMIT_BYTES)


def _dot(a, b):
    return jnp.dot(a, b, preferred_element_type=F32)


def _dot_nt(a, b):
    return lax.dot_general(a, b, (((1,), (1,)), ((), ())), preferred_element_type=F32)


def _dot_tn(a, b):
    return lax.dot_general(a, b, (((0,), (0,)), ((), ())), preferred_element_type=F32)


def _rms(x, g):
    return x * lax.rsqrt(jnp.mean(x * x, axis=-1, keepdims=True) + EPS) * g


def _head_rms(x):
    return x * lax.rsqrt(jnp.mean(x * x, axis=-1, keepdims=True) + EPS)


def _silu(x):
    return x * (1.0 / (1.0 + jnp.exp(-x)))


def _gelu_tanh(x):
    c0 = math.sqrt(2.0 / math.pi)
    return x * (0.5 + 0.5 * jnp.tanh(x * (c0 + (c0 * 0.044715) * (x * x))))


def _log_sigmoid(x):
    return jnp.minimum(x, 0.0) - jnp.log1p(jnp.exp(-jnp.abs(x)))


def _store_head_major(o_ref, val, tm, n_heads, head_dim):
    tiles = head_dim // LANES
    group = n_heads * tiles
    for hd in range(n_heads):
        for part in range(tiles):
            col = hd * head_dim + part * LANES
            o_ref[pl.ds(part * n_heads + hd, tm, stride=group), :] = val[:, col:col + LANES]


def _in_even_kernel(x_ref, g_ref, w_ref, z_ref, k_ref, v_ref, *, tm):
    h = _rms(x_ref[...], g_ref[...]).astype(BF16)
    width = A_HEADS * A_V_DIM
    for c in range(EVEN_IN // width):
        zc = _dot(h, w_ref[:, c * width:(c + 1) * width])
        z_ref[:, c * width:(c + 1) * width] = zc.astype(BF16)
        if c == 1:
            _store_head_major(k_ref, zc, tm, A_HEADS, A_V_DIM)
        if c == 2:
            _store_head_major(v_ref, zc, tm, A_HEADS, A_V_DIM)


def _in_even(x2d, gamma, w, tm):
    n = x2d.shape[0]
    return pl.pallas_call(
        functools.partial(_in_even_kernel, tm=tm),
        grid=(n // tm,),
        in_specs=[pl.BlockSpec((tm, D_MODEL), lambda i: (i, 0)),
                  pl.BlockSpec((1, D_MODEL), lambda i: (0, 0)),
                  pl.BlockSpec((D_MODEL, EVEN_IN), lambda i: (0, 0))],
        out_specs=[pl.BlockSpec((tm, EVEN_IN), lambda i: (i, 0)),
                   pl.BlockSpec((tm * A_HEADS, A_V_DIM), lambda i: (i, 0)),
                   pl.BlockSpec((tm * A_HEADS, A_V_DIM), lambda i: (i, 0))],
        out_shape=[jax.ShapeDtypeStruct((n, EVEN_IN), BF16),
                   jax.ShapeDtypeStruct((n * A_HEADS, A_V_DIM), F32),
                   jax.ShapeDtypeStruct((n * A_HEADS, A_V_DIM), F32)],
        compiler_params=_params(1),
    )(x2d, gamma.reshape(1, D_MODEL), w)


def _in_odd_kernel(x_ref, g_ref, w_ref, wlr_ref, bg_ref, z_ref, la_ref):
    h = _rms(x_ref[...], g_ref[...]).astype(BF16)
    width = 512
    ca = _dot(h, w_ref[:, ODD_Z:ODD_IN_PAD]).astype(BF16)
    pre = _dot(ca, wlr_ref[...]) + bg_ref[...]
    la_ref[...] = _log_sigmoid(pre) / C_GATE_TAU
    for c in range(ODD_Z // width):
        z_ref[:, c * width:(c + 1) * width] = _dot(h, w_ref[:, c * width:(c + 1) * width]).astype(BF16)


def _in_odd(x2d, gamma, w_pad, wlr_pad, b_gate, tm):
    n = x2d.shape[0]
    qk = C_HEADS * C_QK_DIM
    return pl.pallas_call(
        _in_odd_kernel,
        grid=(n // tm,),
        in_specs=[pl.BlockSpec((tm, D_MODEL), lambda i: (i, 0)),
                  pl.BlockSpec((1, D_MODEL), lambda i: (0, 0)),
                  pl.BlockSpec((D_MODEL, ODD_IN_PAD), lambda i: (0, 0)),
                  pl.BlockSpec((LANES, qk), lambda i: (0, 0)),
                  pl.BlockSpec((1, qk), lambda i: (0, 0))],
        out_specs=[pl.BlockSpec((tm, ODD_Z), lambda i: (i, 0)),
                   pl.BlockSpec((tm, qk), lambda i: (i, 0))],
        out_shape=[jax.ShapeDtypeStruct((n, ODD_Z), BF16),
                   jax.ShapeDtypeStruct((n, qk), F32)],
        compiler_params=_params(1),
    )(x2d, gamma.reshape(1, D_MODEL), w_pad, wlr_pad, b_gate.reshape(1, qk))


def _mem_kv_kernel(x_ref, g_ref, wk_ref, wv_ref, k5_ref, v5_ref, kb_ref, vb_ref, *, seqs, mem_len):
    h = _rms(x_ref[...], g_ref[0]).astype(BF16)
    for w_ref, o5_ref, ob_ref in ((wk_ref, k5_ref, kb_ref), (wv_ref, v5_ref, vb_ref)):
        y = _dot(h, w_ref[0])
        ob_ref[0] = y.astype(BF16)
        for s in range(seqs):
            for hd in range(M_HEADS):
                o5_ref[0, s, :, hd, :] = y[s * mem_len:(s + 1) * mem_len,
                                           hd * M_HEAD_DIM:(hd + 1) * M_HEAD_DIM]


def _mem_kv(mem, ln_mem, wk, wv, seqs):
    bsz, mem_len, _ = mem.shape
    depth = ln_mem.shape[0]
    width = M_HEADS * M_HEAD_DIM
    n = bsz * mem_len
    tm = seqs * mem_len
    out5 = jax.ShapeDtypeStruct((depth, bsz, mem_len, M_HEADS, M_HEAD_DIM), F32)
    outb = jax.ShapeDtypeStruct((depth, n, width), BF16)
    spec5 = pl.BlockSpec((1, seqs, mem_len, M_HEADS, M_HEAD_DIM), lambda l, i: (l, i, 0, 0, 0))
    specb = pl.BlockSpec((1, tm, width), lambda l, i: (l, i, 0))
    wspec = pl.BlockSpec((1, D_MODEL, width), lambda l, i: (l, 0, 0))
    return pl.pallas_call(
        functools.partial(_mem_kv_kernel, seqs=seqs, mem_len=mem_len),
        grid=(depth, n // tm),
        in_specs=[pl.BlockSpec((tm, D_MODEL), lambda l, i: (i, 0)),
                  pl.BlockSpec((1, 1, D_MODEL), lambda l, i: (l, 0, 0)),
                  wspec, wspec],
        out_specs=[spec5, spec5, specb, specb],
        out_shape=[out5, out5, outb, outb],
        compiler_params=_params(2),
    )(mem.reshape(n, D_MODEL), ln_mem.reshape(depth, 1, D_MODEL), wk, wv)


def _diff_attn_kernel(lam_ref, q_ref, k_ref, v_ref, bias_ref, subln_ref, o_ref,
                      vt_sc, s_sc, acc_sc, *, tq, nd, hp, out_scale):
    i = pl.program_id(2)
    dv = A_V_DIM
    chains = [(h, c) for h in range(hp) for c in range(2)]
    n_ch = len(chains)

    @pl.when(i == 0)
    def _():
        for h in range(hp):
            for jj in range(nd):
                vt_sc[h, jj, :dv, :] = v_ref[0, jj * tq:(jj + 1) * tq, h * dv:(h + 1) * dv].T
                vt_sc[h, jj, dv:, :] = jnp.ones((ONES_ROWS, tq), BF16)

    row = lax.broadcasted_iota(jnp.int32, (dv, tq), 0)
    qzt = []
    for h in range(hp):
        qt = (q_ref[0, :, h * dv:(h + 1) * dv] * (A_QK_DIM ** -0.5)).T
        for c in range(2):
            qzt.append(jnp.where((row >= A_QK_DIM) == (c == 1), qt, jnp.zeros_like(qt)))

    def raw_scores(j, n):
        h = chains[n][0]
        start = pl.multiple_of(j * tq, tq)
        return _dot(k_ref[0, pl.ds(start, tq), h * dv:(h + 1) * dv], qzt[n])

    kk = lax.broadcasted_iota(jnp.int32, (tq, tq), 0)
    qq = lax.broadcasted_iota(jnp.int32, (tq, tq), 1)
    diag_penalty = jnp.where((kk // CHUNK) <= (qq // CHUNK), 0.0, NEG_INF)

    def pass1(js, ms):
        ms = list(ms)
        items = [(j, n) for j in js for n in range(n_ch)]
        raw_next = raw_scores(*items[0])
        for idx, (j, n) in enumerate(items):
            raw = raw_next
            if idx + 1 < len(items):
                raw_next = raw_scores(*items[idx + 1])
            h, c = chains[n]
            if c == 0:
                bias = bias_ref[h, j - i + (nd - 1)] + diag_penalty * jnp.where(j == i, 1.0, 0.0)
            s = (raw + bias) * LOG2_E
            s_sc[n, j] = s
            ms[n] = jnp.maximum(ms[n], jnp.max(s, axis=0, keepdims=True))
        return tuple(ms)

    def pass2(js):
        items = [(j, n) for j in js for n in range(n_ch)]
        p_next = jnp.exp2(s_sc[items[0][1], items[0][0]] - m_fin[items[0][1]])
        for idx, (j, n) in enumerate(items):
            p = p_next
            if idx + 1 < len(items):
                jn, nn = items[idx + 1]
                p_next = jnp.exp2(s_sc[nn, jn] - m_fin[nn])
            acc_sc[n] += _dot(vt_sc[chains[n][0], j], p.astype(BF16))

    pairs = (i + 1) // 2
    odd = (i + 1) - 2 * pairs
    ms = tuple(jnp.full((1, tq), NEG_INF, F32) for _ in range(n_ch))
    ms = lax.fori_loop(0, pairs, lambda jj, m: pass1([2 * jj, 2 * jj + 1], m), ms)
    m_fin = lax.fori_loop(0, odd, lambda _, m: pass1([2 * pairs], m), ms)

    acc_sc[...] = jnp.zeros(acc_sc.shape, F32)

    def pair2(jj, carry):
        pass2([2 * jj, 2 * jj + 1])
        return carry

    def single2(_, carry):
        pass2([2 * pairs])
        return carry

    lax.fori_loop(0, pairs, pair2, 0)
    lax.fori_loop(0, odd, single2, 0)
    for h in range(hp):
        acc0 = acc_sc[2 * h]
        acc1 = acc_sc[2 * h + 1]
        out0 = acc0[:dv] / acc0[dv:dv + 1]
        out1 = acc1[:dv] / acc1[dv:dv + 1]
        o = (out0 - lam_ref[0, 0] * out1).T
        o_ref[:, h * dv:(h + 1) * dv] = (_head_rms(o) * subln_ref[...] * out_scale).astype(BF16)


def _diff_attn(lam, z3, bias_tiles, subln, tq, hp, out_scale):
    bsz, t, _ = z3.shape
    nd = t // tq
    width = hp * A_V_DIM
    groups = A_HEADS // hp
    return pl.pallas_call(
        functools.partial(_diff_attn_kernel, tq=tq, nd=nd, hp=hp, out_scale=out_scale),
        grid=(bsz, groups, nd),
        in_specs=[pl.BlockSpec(memory_space=pltpu.SMEM),
                  pl.BlockSpec((1, tq, width), lambda b, g, i: (b, i, g)),
                  pl.BlockSpec((1, t, width), lambda b, g, i: (b, 0, groups + g)),
                  pl.BlockSpec((1, t, width), lambda b, g, i: (b, 0, 2 * groups + g)),
                  pl.BlockSpec((hp, nd, tq, tq), lambda b, g, i: (g, 0, 0, 0),
                               pipeline_mode=pl.Buffered(1)),
                  pl.BlockSpec((1, A_V_DIM), lambda b, g, i: (0, 0))],
        out_specs=pl.BlockSpec((tq, width), lambda b, g, i: (b * nd + i, g)),
        out_shape=jax.ShapeDtypeStruct((bsz * t, A_HEADS * A_V_DIM), BF16),
        scratch_shapes=[pltpu.VMEM((hp, nd, A_V_DIM + ONES_ROWS, tq), BF16),
                        pltpu.VMEM((2 * hp, nd, tq, tq), F32),
                        pltpu.VMEM((2 * hp, A_V_DIM + ONES_ROWS, tq), F32)],
        compiler_params=_params(3),
    )(lam, z3, z3, z3, bias_tiles, subln.reshape(1, A_V_DIM))


def _diff_attn_cached_kernel(lam_ref, q_ref, kn_ref, vn_ref, kp_ref, vp_ref, bp_ref, bn_ref,
                             subln_ref, o_ref, *, t, past, out_scale):
    lane = lax.broadcasted_iota(jnp.int32, (t, A_V_DIM), 1)
    qpos_p = past + lax.broadcasted_iota(jnp.int32, (2 * t, past), 0) % t
    kpos_p = lax.broadcasted_iota(jnp.int32, (2 * t, past), 1)
    ok_p = (kpos_p // CHUNK) <= (qpos_p // CHUNK)
    qpos_n = past + lax.broadcasted_iota(jnp.int32, (2 * t, t), 0) % t
    kpos_n = past + lax.broadcasted_iota(jnp.int32, (2 * t, t), 1)
    ok_n = (kpos_n // CHUNK) <= (qpos_n // CHUNK)
    for h in range(A_HEADS):
        cs = slice(h * A_V_DIM, (h + 1) * A_V_DIM)
        q = q_ref[0, :, cs] * (A_QK_DIM ** -0.5)
        zero = jnp.zeros_like(q)
        qp = jnp.concatenate([jnp.where(lane < A_QK_DIM, q, zero),
                              jnp.where(lane >= A_QK_DIM, q, zero)], axis=0)
        kp = kp_ref[0, :, cs].astype(BF16)
        vp = vp_ref[0, :, cs].astype(BF16)
        kn = kn_ref[0, :, cs]
        vn = vn_ref[0, :, cs]
        bp = bp_ref[h]
        bn = bn_ref[h]
        sp = jnp.where(ok_p, _dot_nt(qp, kp) + jnp.concatenate([bp, bp], axis=0), NEG_INF)
        sn = jnp.where(ok_n, _dot_nt(qp, kn) + jnp.concatenate([bn, bn], axis=0), NEG_INF)
        m = jnp.maximum(jnp.max(sp, axis=-1, keepdims=True), jnp.max(sn, axis=-1, keepdims=True))
        pp = jnp.exp(sp - m)
        pn = jnp.exp(sn - m)
        l = jnp.sum(pp, axis=-1, keepdims=True) + jnp.sum(pn, axis=-1, keepdims=True)
        out = (_dot(pp.astype(BF16), vp) + _dot(pn.astype(BF16), vn)) / l
        o = out[:t] - lam_ref[0, 0] * out[t:]
        o_ref[:, cs] = (_head_rms(o) * subln_ref[...] * out_scale).astype(BF16)


def _diff_attn_cached(lam, z3, past_k, past_v, bias_past, bias_new, subln, out_scale):
    bsz, t, _ = z3.shape
    past = past_k.shape[1]
    width = A_HEADS * A_V_DIM
    return pl.pallas_call(
        functools.partial(_diff_attn_cached_kernel, t=t, past=past, out_scale=out_scale),
        grid=(bsz,),
        in_specs=[pl.BlockSpec(memory_space=pltpu.SMEM),
                  pl.BlockSpec((1, t, width), lambda b: (b, 0, 0)),
                  pl.BlockSpec((1, t, width), lambda b: (b, 0, 1)),
                  pl.BlockSpec((1, t, width), lambda b: (b, 0, 2)),
                  pl.BlockSpec((1, past, width), lambda b: (b, 0, 0)),
                  pl.BlockSpec((1, past, width), lambda b: (b, 0, 0)),
                  pl.BlockSpec((A_HEADS, t, past), lambda b: (0, 0, 0)),
                  pl.BlockSpec((A_HEADS, t, t), lambda b: (0, 0, 0)),
                  pl.BlockSpec((1, A_V_DIM), lambda b: (0, 0))],
        out_specs=pl.BlockSpec((t, width), lambda b: (b, 0)),
        out_shape=jax.ShapeDtypeStruct((bsz * t, width), BF16),
        compiler_params=_params(1),
    )(lam, z3, z3, z3, past_k, past_v, bias_past, bias_new, subln.reshape(1, A_V_DIM))


def _retention_kernel(q_ref, k_ref, v_ref, gt_ref, cos_ref, sin_ref, dec_ref, qd_ref, kd_ref,
                      cd_ref, s0_ref, o_ref, s_out_ref, s_sc):
    t = pl.program_id(1)

    @pl.when(t == 0)
    def _():
        s_sc[...] = s0_ref[0]

    d = B_QK_DIM
    half = d // 2
    cos = cos_ref[...]
    sin = sin_ref[...]
    for h in range(B_HEADS):
        cs = slice(h * d, (h + 1) * d)
        q = q_ref[0, :, cs].astype(F32)
        k = k_ref[0, :, cs].astype(F32)
        qr = (q * cos + pltpu.roll(q, half, 1) * sin) * (d ** -0.5)
        kr = k * cos + pltpu.roll(k, half, 1) * sin
        v = v_ref[0, :, cs]
        att = _dot_nt(qr.astype(BF16), kr.astype(BF16)) * dec_ref[h]
        s = s_sc[h]
        o = _dot(att.astype(BF16), v) + _dot((qr * qd_ref[h]).astype(BF16), s.astype(BF16))
        s_sc[h] = s * cd_ref[h] + _dot_tn((kr * kd_ref[h]).astype(BF16), v)
        gt = gt_ref[0, :, cs].astype(F32)
        o_ref[:, cs] = (_head_rms(o) * _silu(gt)).astype(BF16)

    @pl.when(t == pl.num_programs(1) - 1)
    def _():
        s_out_ref[0] = s_sc[...]


def _retention(z3, cos, sin, consts, s0, c):
    bsz, t, _ = z3.shape
    nt = t // c
    dec, qd, kd, cd = consts
    d = B_QK_DIM
    width = B_HEADS * d
    base = 3
    full3 = lambda b, i: (0, 0, 0)
    return pl.pallas_call(
        _retention_kernel,
        grid=(bsz, nt),
        in_specs=[pl.BlockSpec((1, c, width), lambda b, i: (b, i, base)),
                  pl.BlockSpec((1, c, width), lambda b, i: (b, i, base + 1)),
                  pl.BlockSpec((1, c, width), lambda b, i: (b, i, base + 2)),
                  pl.BlockSpec((1, c, width), lambda b, i: (b, i, base + 3)),
                  pl.BlockSpec((c, d), lambda b, i: (i, 0)),
                  pl.BlockSpec((c, d), lambda b, i: (i, 0)),
                  pl.BlockSpec((B_HEADS, c, c), full3),
                  pl.BlockSpec((B_HEADS, c, d), full3),
                  pl.BlockSpec((B_HEADS, c, d), full3),
                  pl.BlockSpec((B_HEADS, 1, d), full3),
                  pl.BlockSpec((1, B_HEADS, d, d), lambda b, i: (b, 0, 0, 0))],
        out_specs=[pl.BlockSpec((c, width), lambda b, i: (b * nt + i, 0)),
                   pl.BlockSpec((1, B_HEADS, d, d), lambda b, i: (b, 0, 0, 0))],
        out_shape=[jax.ShapeDtypeStruct((bsz * t, width), BF16),
                   jax.ShapeDtypeStruct((bsz, B_HEADS, d, d), F32)],
        scratch_shapes=[pltpu.VMEM((B_HEADS, d, d), F32)],
        compiler_params=_params(2),
    )(z3, z3, z3, z3, cos, sin, dec, qd, kd, cd, s0)


def _retention_consts(c):
    log_g = jnp.log1p(-jnp.exp2(-5.0 - jnp.arange(B_HEADS, dtype=F32)))
    idx = jnp.arange(c, dtype=F32)
    dist = idx[:, None] - idx[None, :]
    dec = jnp.where(dist >= 0, jnp.exp(jnp.maximum(dist, 0.0)[None] * log_g[:, None, None]), 0.0)
    qd = jnp.exp((idx[None, :] + 1.0) * log_g[:, None])
    kd = jnp.exp((c - 1.0 - idx)[None, :] * log_g[:, None])
    cd = jnp.exp(c * log_g)
    bc = lambda a: jnp.broadcast_to(a[..., None], a.shape + (B_QK_DIM,))
    return dec, bc(qd), bc(kd), bc(cd[:, None])


def _rotary_tables(pos):
    half = B_QK_DIM // 2
    inv = ROPE_BASE ** (-jnp.arange(half, dtype=F32) / half)
    ang = pos.astype(F32)[:, None] * inv[None, :]
    cos, sin = jnp.cos(ang), jnp.sin(ang)
    return jnp.concatenate([cos, cos], axis=-1), jnp.concatenate([-sin, sin], axis=-1)


def _gla_levels(c):
    return [c >> (l + 1) for l in range(int(math.log2(c)))]


def _gla_consts(c):
    levels = _gla_levels(c)
    rows = np.arange(c)
    mats = []
    for s in levels:
        ref = (rows // (2 * s)) * 2 * s + s - 1
        a = np.zeros((c, c), np.float32)
        for i in range(c):
            if i & s:
                a[i, ref[i] + 1:i + 1] = 1.0
            else:
                a[i, i + 1:ref[i] + 1] = 1.0
        mats.append(a)
    mats.append(np.tril(np.ones((c, c), np.float32)))
    mats.append(np.triu(np.ones((c, c), np.float32), 1))
    lv = np.full((c, c), -1, np.int32)
    for i in range(c):
        lv[i, i] = len(levels)
        for j in range(i):
            lv[i, j] = levels.index(1 << int(math.floor(math.log2(i ^ j))))
    return jnp.asarray(np.concatenate(mats, axis=0), BF16), jnp.asarray(lv)


def _gla_kernel(q_ref, k_ref, v_ref, r_ref, g_ref, s0_ref, a_ref, lv_ref, nw_ref,
                o_ref, s_out_ref, st_sc, *, c):
    t = pl.program_id(1)
    dk, dv = C_QK_DIM, C_V_DIM

    @pl.when(t == 0)
    def _():
        for h in range(C_HEADS):
            st_sc[h] = s0_ref[0, h].T

    levels = _gla_levels(c)
    n_lv = len(levels)
    lv = lv_ref[...]
    row = lax.broadcasted_iota(jnp.int32, (c, dk), 0)
    g = g_ref[0]
    xs = _dot(a_ref[...], g.astype(BF16))
    for h in range(C_HEADS):
        ks = slice(h * dk, (h + 1) * dk)
        vs = slice(h * dv, (h + 1) * dv)
        x = xs[:, ks]
        q = q_ref[0, :, ks].astype(F32) * (dk ** -0.5)
        k = k_ref[0, :, ks].astype(F32)
        v = v_ref[0, :, vs]
        att = jnp.zeros((c, c), F32)
        pending = (n_lv, _dot_nt(q.astype(BF16), k.astype(BF16)))
        for l, s in enumerate(levels):
            e = jnp.exp(x[l * c:(l + 1) * c])
            up = (row & s) != 0
            mix = jnp.where(up, q, k) * e
            qt = jnp.where(up, mix, 0.0).astype(BF16)
            kt = jnp.where(up, 0.0, mix).astype(BF16)
            prod = _dot_nt(qt, kt)
            att = jnp.where(lv == pending[0], pending[1], att)
            pending = (l, prod)
        att = jnp.where(lv == pending[0], pending[1], att)
        b = x[n_lv * c:(n_lv + 1) * c]
        rem = x[(n_lv + 1) * c:(n_lv + 2) * c]
        st = st_sc[h]
        o = _dot(att.astype(BF16), v) + _dot_nt((q * jnp.exp(b)).astype(BF16), st.astype(BF16))
        kd = (k * jnp.exp(rem)).astype(BF16)
        st_sc[h] = st * jnp.exp(b[c - 1:c, :]) + _dot_tn(v, kd)
        r = r_ref[0, :, vs].astype(F32)
        o_ref[:, vs] = (_head_rms(o) * nw_ref[...] * _silu(r)).astype(BF16)

    @pl.when(t == pl.num_programs(1) - 1)
    def _():
        for h in range(C_HEADS):
            s_out_ref[0, h] = st_sc[h].T


def _gla(z3, log_a3, s0, a_stack, lv, norm_w, c):
    bsz, t, _ = z3.shape
    nt = t // c
    dk, dv = C_QK_DIM, C_V_DIM
    qk_w, v_w = C_HEADS * dk, C_HEADS * dv
    const = lambda b, i: (0, 0)
    return pl.pallas_call(
        functools.partial(_gla_kernel, c=c),
        grid=(bsz, nt),
        in_specs=[pl.BlockSpec((1, c, qk_w), lambda b, i: (b, i, 0)),
                  pl.BlockSpec((1, c, qk_w), lambda b, i: (b, i, 1)),
                  pl.BlockSpec((1, c, v_w), lambda b, i: (b, i, 1)),
                  pl.BlockSpec((1, c, v_w), lambda b, i: (b, i, 2)),
                  pl.BlockSpec((1, c, qk_w), lambda b, i: (b, i, 0)),
                  pl.BlockSpec((1, C_HEADS, dk, dv), lambda b, i: (b, 0, 0, 0)),
                  pl.BlockSpec(a_stack.shape, const),
                  pl.BlockSpec((c, c), const),
                  pl.BlockSpec((1, dv), const)],
        out_specs=[pl.BlockSpec((c, v_w), lambda b, i: (b * nt + i, 0)),
                   pl.BlockSpec((1, C_HEADS, dk, dv), lambda b, i: (b, 0, 0, 0))],
        out_shape=[jax.ShapeDtypeStruct((bsz * t, v_w), BF16),
                   jax.ShapeDtypeStruct((bsz, C_HEADS, dk, dv), F32)],
        scratch_shapes=[pltpu.VMEM((C_HEADS, dv, dk), F32)],
        compiler_params=_params(2),
    )(z3, z3, z3, z3, log_a3, s0, a_stack, lv, norm_w.reshape(1, dv))


def _cross_kernel(*refs, n_mix):
    x_ref = refs[0]
    a_refs = refs[1:1 + n_mix]
    w_refs = refs[1 + n_mix:1 + 2 * n_mix]
    g_ref, wq_ref, mk_ref, mv_ref, wo_ref, o_ref = refs[1 + 2 * n_mix:]
    x = x_ref[...]
    for a_ref, w_ref in zip(a_refs, w_refs):
        x = x + _dot(a_ref[...], w_ref[...])
    h = _rms(x, g_ref[...]).astype(BF16)
    q = (_dot(h, wq_ref[...]) * (M_HEAD_DIM ** -0.5)).astype(BF16)
    mk = mk_ref[0].astype(BF16)
    mv = mv_ref[0].astype(BF16)
    outs = []
    for hd in range(M_HEADS):
        cs = slice(hd * M_HEAD_DIM, (hd + 1) * M_HEAD_DIM)
        s = _dot_nt(q[:, cs], mk[:, cs])
        p = jnp.exp(s - jnp.max(s, axis=-1, keepdims=True))
        l = jnp.sum(p, axis=-1, keepdims=True)
        outs.append((_dot(p.astype(BF16), mv[:, cs]) / l).astype(BF16))
    o_ref[...] = x + _dot(jnp.concatenate(outs, axis=1), wo_ref[...])


def _cross(x2d, mix_list, w_mix_list, gamma, wq, mk3, mv3, wo, t, tm):
    n = x2d.shape[0]
    nt = t // tm
    mem_len, width = mk3.shape[1], mk3.shape[2]
    rows = lambda b, i: (b * nt + i, 0)
    const = lambda b, i: (0, 0)
    in_specs = [pl.BlockSpec((tm, D_MODEL), rows)]
    in_specs += [pl.BlockSpec((tm, a.shape[1]), rows) for a in mix_list]
    in_specs += [pl.BlockSpec(w.shape, const) for w in w_mix_list]
    in_specs += [pl.BlockSpec((1, D_MODEL), const),
                 pl.BlockSpec((D_MODEL, width), const),
                 pl.BlockSpec((1, mem_len, width), lambda b, i: (b, 0, 0)),
                 pl.BlockSpec((1, mem_len, width), lambda b, i: (b, 0, 0)),
                 pl.BlockSpec((width, D_MODEL), const)]
    return pl.pallas_call(
        functools.partial(_cross_kernel, n_mix=len(mix_list)),
        grid=(n // t, nt),
        in_specs=in_specs,
        out_specs=pl.BlockSpec((tm, D_MODEL), rows),
        out_shape=jax.ShapeDtypeStruct((n, D_MODEL), F32),
        compiler_params=_params(2),
    )(x2d, *mix_list, *w_mix_list, gamma.reshape(1, D_MODEL), wq, mk3, mv3, wo)


def _ffn_kernel(x_ref, g_ref, wg_ref, wu_ref, cw_ref, cb_ref, prev_ref, wd_ref, *rest,
                tm, final_norm):
    if final_norm:
        gf_ref, o_ref, tail_ref, carry = rest
    else:
        o_ref, tail_ref, carry = rest
    t = pl.program_id(1)
    lo = SUBLANES - 2

    @pl.when(t == 0)
    def _():
        carry[lo:SUBLANES, :] = prev_ref[0]

    x = x_ref[...]
    h = _rms(x, g_ref[...]).astype(BF16)
    row = lax.broadcasted_iota(jnp.int32, (SUBLANES, FF_CHUNK), 0)
    acc = x
    n_chunks = D_FF // FF_CHUNK
    chunk = lambda c: slice(c * FF_CHUNK, (c + 1) * FF_CHUNK)
    nxt = (_dot(h, wg_ref[:, chunk(0)]), _dot(h, wu_ref[:, chunk(0)]))
    pending = []
    for c in range(n_chunks):
        cs = chunk(c)
        gate, up = nxt
        if c + 1 < n_chunks:
            nxt = (_dot(h, wg_ref[:, chunk(c + 1)]), _dot(h, wu_ref[:, chunk(c + 1)]))
        p1 = carry[SUBLANES - 1:SUBLANES, cs]
        p2 = carry[lo:lo + 1, cs]
        r1 = pltpu.roll(gate, 1, 0)
        r2 = pltpu.roll(gate, 2, 0)
        h1 = jnp.where(row == 0, p1, r1[:SUBLANES])
        h2 = jnp.where(row == 0, p2, jnp.where(row == 1, p1, r2[:SUBLANES]))
        g1 = jnp.concatenate([h1, r1[SUBLANES:]], axis=0)
        g2 = jnp.concatenate([h2, r2[SUBLANES:]], axis=0)
        carry[:, cs] = gate[tm - SUBLANES:tm]
        conv = cb_ref[:, cs] + cw_ref[0:1, cs] * g2
        conv = conv + cw_ref[1:2, cs] * g1
        conv = conv + cw_ref[2:3, cs] * gate
        pending.append((_gelu_tanh(conv) * up).astype(BF16))
        if len(pending) == DOWN_GROUP or c == n_chunks - 1:
            lo_col = (c + 1 - len(pending)) * FF_CHUNK
            acc = acc + _dot(jnp.concatenate(pending, axis=1), wd_ref[lo_col:(c + 1) * FF_CHUNK, :])
            pending = []
    if final_norm:
        o_ref[...] = _rms(acc, gf_ref[...])
    else:
        o_ref[...] = acc

    @pl.when(t == pl.num_programs(1) - 1)
    def _():
        tail_ref[0] = carry[lo:SUBLANES, :]


def _ffn(x2d, gamma, wg, wu, conv_w, conv_b, prev, wd, t, tm, final_gamma=None):
    n = x2d.shape[0]
    nt = t // tm
    bsz = n // t
    const = lambda b, i: (0, 0)
    resident = dict(pipeline_mode=pl.Buffered(1))
    in_specs = [pl.BlockSpec((tm, D_MODEL), lambda b, i: (b * nt + i, 0)),
                pl.BlockSpec((1, D_MODEL), const),
                pl.BlockSpec((D_MODEL, D_FF), const, **resident),
                pl.BlockSpec((D_MODEL, D_FF), const, **resident),
                pl.BlockSpec((3, D_FF), const),
                pl.BlockSpec((1, D_FF), const),
                pl.BlockSpec((1, 2, D_FF), lambda b, i: (b, 0, 0)),
                pl.BlockSpec((D_FF, D_MODEL), const, **resident)]
    args = [x2d, gamma.reshape(1, D_MODEL), wg, wu, conv_w, conv_b.reshape(1, D_FF), prev, wd]
    if final_gamma is not None:
        in_specs.append(pl.BlockSpec((1, D_MODEL), const))
        args.append(final_gamma.reshape(1, D_MODEL))
    return pl.pallas_call(
        functools.partial(_ffn_kernel, tm=tm, final_norm=final_gamma is not None),
        grid=(bsz, nt),
        in_specs=in_specs,
        out_specs=[pl.BlockSpec((tm, D_MODEL), lambda b, i: (b * nt + i, 0)),
                   pl.BlockSpec((1, 2, D_FF), lambda b, i: (b, 0, 0))],
        out_shape=[jax.ShapeDtypeStruct((n, D_MODEL), F32),
                   jax.ShapeDtypeStruct((bsz, 2, D_FF), F32)],
        scratch_shapes=[pltpu.VMEM((SUBLANES, D_FF), F32)],
        compiler_params=_params(2),
    )(*args)


def _rel_bucket(rel):
    nb = REL_BUCKETS // 2
    max_exact = nb // 2
    n = jnp.abs(rel)
    nf = jnp.maximum(n, 1).astype(F32)
    large = max_exact + (jnp.log(nf / max_exact) / math.log(REL_MAX_DIST / max_exact)
                         * (nb - max_exact)).astype(jnp.int32)
    large = jnp.minimum(large, nb - 1)
    return jnp.where(rel > 0, nb, 0) + jnp.where(n < max_exact, n, large)


def _bias_tiles(rel_bias, tq, nd):
    period = 2 * tq
    n = np.arange(period)
    rel = np.stack([np.where(n < tq, (d - (nd - 1)) * tq - n, (d - (nd - 1)) * tq + period - n)
                    for d in range(nd)]).astype(np.int32)
    w = jnp.transpose(rel_bias[_rel_bucket(jnp.asarray(rel))], (2, 0, 1)).astype(F32)
    flat = jnp.tile(w, (1, 1, tq))[:, :, :tq * (period - 1)]
    return flat.reshape(A_HEADS, nd, tq, period - 1)[:, :, :, :tq]


def _bias_rows(rel_bias, t, past):
    rel = (np.arange(past + t)[None, :] - (past + np.arange(t))[:, None]).astype(np.int32)
    bias = jnp.transpose(rel_bias[_rel_bucket(jnp.asarray(rel))], (2, 0, 1)).astype(F32)
    return bias[:, :, :past], bias[:, :, past:]


def _forward(x, past_k, past_v, ret_state, gla_state, conv_prev, mem_k, mem_v, p, w):
    bsz, t, _ = x.shape
    n = bsz * t
    past = 0 if past_k is None else past_k.shape[2]
    tm = min(512, n)
    tseq = min(512, t)
    tffn = min(256, t)
    tq = min(256, t)
    c_ret = min(256, t)
    c_gla = min(256, t)
    x2 = x.reshape(n, D_MODEL)
    pos = past + jnp.arange(t, dtype=jnp.int32)
    new_conv = []

    z, ak, av = _in_even(x2, p["ln_mix"][0], w["w_in_even"], tm)
    z3 = z.reshape(bsz, t, EVEN_IN)
    lam_init = 0.8 - 0.6 * math.exp(-0.3 * 0)
    lam = (jnp.exp(jnp.sum(p["diff_lq1"][0].astype(F32) * p["diff_lk1"][0].astype(F32)))
           - jnp.exp(jnp.sum(p["diff_lq2"][0].astype(F32) * p["diff_lk2"][0].astype(F32)))
           + lam_init).reshape(1, 1).astype(F32)
    if past == 0:
        o_a = _diff_attn(lam, z3, _bias_tiles(p["rel_bias"], tq, t // tq), p["diff_subln"][0],
                         tq, A_HEADS, 1.0 - lam_init)
    else:
        width = A_HEADS * A_V_DIM
        bias_past, bias_new = _bias_rows(p["rel_bias"], t, past)
        o_a = _diff_attn_cached(lam, z3, past_k[0].reshape(bsz, past, width),
                                past_v[0].reshape(bsz, past, width), bias_past, bias_new,
                                p["diff_subln"][0], 1.0 - lam_init)
    cos, sin = _rotary_tables(pos)
    o_b, s_ret = _retention(z3, cos, sin, _retention_consts(c_ret), ret_state[0], c_ret)
    x2 = _cross(x2, [o_a, o_b], [w["w_out_even_a"], w["w_out_even_b"]], p["ln_cross"][0],
                w["w_cq"][0], mem_k[0], mem_v[0], w["w_co"][0], t, tseq)
    x2, tail = _ffn(x2, p["ln_ffn"][0], w["w_ffn_gate"][0], w["w_ffn_up"][0], p["ffn_conv_w"][0],
                    p["ffn_conv_b"][0], conv_prev[0], w["w_ffn_down"][0], t, tffn)
    new_conv.append(tail)

    zc, log_a = _in_odd(x2, p["ln_mix"][1], w["w_in_odd"], w["w_gate_lr"], p["b_gate"][0], tm)
    a_stack, lv = _gla_consts(c_gla)
    o_c, s_gla = _gla(zc.reshape(bsz, t, ODD_Z), log_a.reshape(bsz, t, C_HEADS * C_QK_DIM),
                      gla_state[0], a_stack, lv, p["gla_norm"][0], c_gla)
    x2 = _cross(x2, [o_c], [w["w_out_odd"]], p["ln_cross"][1],
                w["w_cq"][1], mem_k[1], mem_v[1], w["w_co"][1], t, tseq)
    y, tail = _ffn(x2, p["ln_ffn"][1], w["w_ffn_gate"][1], w["w_ffn_up"][1], p["ffn_conv_w"][1],
                   p["ffn_conv_b"][1], conv_prev[1], w["w_ffn_down"][1], t, tffn,
                   final_gamma=p["ln_final"])
    new_conv.append(tail)

    return (y.reshape(bsz, t, D_MODEL),
            ak.reshape(1, bsz, t, A_HEADS, A_V_DIM), av.reshape(1, bsz, t, A_HEADS, A_V_DIM),
            s_ret[None], s_gla[None], jnp.stack(new_conv))


def kernel(x_prompt, x_sample, cache_diff_k, cache_diff_v, state_retention, state_gla, cache_ffn_conv, cache_mem_k, cache_mem_v, mem_prompt, ln_mix, ln_cross, ln_ffn, ln_mem, ln_final, w_in_even, w_out_even, diff_lq1, diff_lk1, diff_lq2, diff_lk2, diff_subln, rel_bias, w_in_odd, w_gate_lr, b_gate, gla_norm, w_out_odd, w_cq, w_ck, w_cv, w_co, w_ffn_gate, w_ffn_up, ffn_conv_w, ffn_conv_b, w_ffn_down):
    p = dict(ln_mix=ln_mix, ln_cross=ln_cross, ln_ffn=ln_ffn, ln_final=ln_final,
             diff_lq1=diff_lq1, diff_lk1=diff_lk1, diff_lq2=diff_lq2, diff_lk2=diff_lk2,
             diff_subln=diff_subln, rel_bias=rel_bias, b_gate=b_gate, gla_norm=gla_norm,
             ffn_conv_w=ffn_conv_w, ffn_conv_b=ffn_conv_b)
    a_v = A_HEADS * A_V_DIM
    w = dict(
        w_in_even=w_in_even[0].astype(BF16),
        w_out_even_a=w_out_even[0, :a_v].astype(BF16),
        w_out_even_b=w_out_even[0, a_v:].astype(BF16),
        w_in_odd=jnp.pad(w_in_odd[0], ((0, 0), (0, ODD_IN_PAD - w_in_odd.shape[2]))).astype(BF16),
        w_gate_lr=jnp.pad(w_gate_lr[0], ((0, LANES - C_GATE_RANK), (0, 0))).astype(BF16),
        w_out_odd=w_out_odd[0].astype(BF16),
        w_cq=w_cq.astype(BF16), w_co=w_co.astype(BF16),
        w_ffn_gate=w_ffn_gate.astype(BF16), w_ffn_up=w_ffn_up.astype(BF16),
        w_ffn_down=w_ffn_down.astype(BF16))

    bp, mem_len, _ = mem_prompt.shape
    bs = x_sample.shape[0]
    m_width = M_HEADS * M_HEAD_DIM
    depth = ln_mem.shape[0]
    mem_k_p, mem_v_p, mk_b, mv_b = _mem_kv(mem_prompt, ln_mem, w_ck.astype(BF16),
                                           w_cv.astype(BF16), seqs=2)

    dt = x_prompt.dtype
    zero_ret = jnp.zeros((1, bp, B_HEADS, B_QK_DIM, B_QK_DIM), dt)
    zero_gla = jnp.zeros((1, bp, C_HEADS, C_QK_DIM, C_V_DIM), dt)
    zero_conv = jnp.zeros((depth, bp, 2, D_FF), dt)
    y_p, dk_p, dv_p, ret_p, gla_p, conv_p = _forward(
        x_prompt, None, None, zero_ret, zero_gla, zero_conv,
        mk_b.reshape(depth, bp, mem_len, m_width), mv_b.reshape(depth, bp, mem_len, m_width), p, w)
    y_s, dk_s, dv_s, ret_s, gla_s, conv_s = _forward(
        x_sample, cache_diff_k, cache_diff_v, state_retention, state_gla, cache_ffn_conv,
        cache_mem_k.reshape(depth, bs, mem_len, m_width),
        cache_mem_v.reshape(depth, bs, mem_len, m_width), p, w)
    return (y_p, y_s, dk_p, dv_p, ret_p, gla_p, conv_p, mem_k_p, mem_v_p,
            dk_s, dv_s, ret_s, gla_s, conv_s)
```

```python
import functools
import math

import numpy as np
import jax
import jax.numpy as jnp
from jax import lax
from jax.experimental import pallas as pl
from jax.experimental.pallas import tpu as pltpu

F32 = jnp.float32
BF16 = jnp.bfloat16

D_MODEL = 1024
CHUNK = 64
A_HEADS = 4
A_QK_DIM = 64
A_V_DIM = 128
B_HEADS = 4
B_QK_DIM = 128
C_HEADS = 4
C_QK_DIM = 128
C_V_DIM = 256
C_GATE_RANK = 16
C_GATE_TAU = 16.0
M_HEADS = 4
M_HEAD_DIM = 256
REL_BUCKETS = 32
REL_MAX_DIST = 128
D_FF = 2816
ROPE_BASE = 10000.0
EPS = 1e-6
NEG_INF = -1e30

EVEN_IN = 3584
ODD_Z = 3072
LANES = 128
SUBLANES = 8
ODD_IN_PAD = ODD_Z + LANES
FF_CHUNK = 256
DOWN_GROUP = 6
ONES_ROWS = 16
LOG2_E = math.log2(math.e)
V7X_VMEM_LIMIT_BYTES = 56 * 1024 * 1024


def _params(n_axes):
    return pltpu.CompilerParams(dimension_semantics=("arbitrary",) * n_axes,
                                vmem_limit_bytes=V7X_VMEM_LIMIT_BYTES)


def _dot(a, b):
    return jnp.dot(a, b, preferred_element_type=F32)


def _dot_nt(a, b):
    return lax.dot_general(a, b, (((1,), (1,)), ((), ())), preferred_element_type=F32)


def _dot_tn(a, b):
    return lax.dot_general(a, b, (((0,), (0,)), ((), ())), preferred_element_type=F32)


def _rms(x, g):
    return x * lax.rsqrt(jnp.mean(x * x, axis=-1, keepdims=True) + EPS) * g


def _head_rms(x):
    return x * lax.rsqrt(jnp.mean(x * x, axis=-1, keepdims=True) + EPS)


def _silu(x):
    return x * (1.0 / (1.0 + jnp.exp(-x)))


def _gelu_tanh(x):
    c0 = math.sqrt(2.0 / math.pi)
    return x * (0.5 + 0.5 * jnp.tanh(x * (c0 + (c0 * 0.044715) * (x * x))))


def _log_sigmoid(x):
    return jnp.minimum(x, 0.0) - jnp.log1p(jnp.exp(-jnp.abs(x)))


def _store_head_major(o_ref, val, tm, n_heads, head_dim):
    tiles = head_dim // LANES
    group = n_heads * tiles
    for hd in range(n_heads):
        for part in range(tiles):
            col = hd * head_dim + part * LANES
            o_ref[pl.ds(part * n_heads + hd, tm, stride=group), :] = val[:, col:col + LANES]


def _in_even_kernel(x_ref, g_ref, w_ref, z_ref, k_ref, v_ref, *, tm):
    h = _rms(x_ref[...], g_ref[...]).astype(BF16)
    width = A_HEADS * A_V_DIM
    for c in range(EVEN_IN // width):
        zc = _dot(h, w_ref[:, c * width:(c + 1) * width])
        z_ref[:, c * width:(c + 1) * width] = zc.astype(BF16)
        if c == 1:
            _store_head_major(k_ref, zc, tm, A_HEADS, A_V_DIM)
        if c == 2:
            _store_head_major(v_ref, zc, tm, A_HEADS, A_V_DIM)


def _in_even(x2d, gamma, w, tm):
    n = x2d.shape[0]
    return pl.pallas_call(
        functools.partial(_in_even_kernel, tm=tm),
        grid=(n // tm,),
        in_specs=[pl.BlockSpec((tm, D_MODEL), lambda i: (i, 0)),
                  pl.BlockSpec((1, D_MODEL), lambda i: (0, 0)),
                  pl.BlockSpec((D_MODEL, EVEN_IN), lambda i: (0, 0))],
        out_specs=[pl.BlockSpec((tm, EVEN_IN), lambda i: (i, 0)),
                   pl.BlockSpec((tm * A_HEADS, A_V_DIM), lambda i: (i, 0)),
                   pl.BlockSpec((tm * A_HEADS, A_V_DIM), lambda i: (i, 0))],
        out_shape=[jax.ShapeDtypeStruct((n, EVEN_IN), BF16),
                   jax.ShapeDtypeStruct((n * A_HEADS, A_V_DIM), F32),
                   jax.ShapeDtypeStruct((n * A_HEADS, A_V_DIM), F32)],
        compiler_params=_params(1),
    )(x2d, gamma.reshape(1, D_MODEL), w)


def _in_odd_kernel(x_ref, g_ref, w_ref, wlr_ref, bg_ref, z_ref, la_ref):
    h = _rms(x_ref[...], g_ref[...]).astype(BF16)
    width = 512
    ca = _dot(h, w_ref[:, ODD_Z:ODD_IN_PAD]).astype(BF16)
    pre = _dot(ca, wlr_ref[...]) + bg_ref[...]
    la_ref[...] = _log_sigmoid(pre) / C_GATE_TAU
    for c in range(ODD_Z // width):
        z_ref[:, c * width:(c + 1) * width] = _dot(h, w_ref[:, c * width:(c + 1) * width]).astype(BF16)


def _in_odd(x2d, gamma, w_pad, wlr_pad, b_gate, tm):
    n = x2d.shape[0]
    qk = C_HEADS * C_QK_DIM
    return pl.pallas_call(
        _in_odd_kernel,
        grid=(n // tm,),
        in_specs=[pl.BlockSpec((tm, D_MODEL), lambda i: (i, 0)),
                  pl.BlockSpec((1, D_MODEL), lambda i: (0, 0)),
                  pl.BlockSpec((D_MODEL, ODD_IN_PAD), lambda i: (0, 0)),
                  pl.BlockSpec((LANES, qk), lambda i: (0, 0)),
                  pl.BlockSpec((1, qk), lambda i: (0, 0))],
        out_specs=[pl.BlockSpec((tm, ODD_Z), lambda i: (i, 0)),
                   pl.BlockSpec((tm, qk), lambda i: (i, 0))],
        out_shape=[jax.ShapeDtypeStruct((n, ODD_Z), BF16),
                   jax.ShapeDtypeStruct((n, qk), F32)],
        compiler_params=_params(1),
    )(x2d, gamma.reshape(1, D_MODEL), w_pad, wlr_pad, b_gate.reshape(1, qk))


def _mem_kv_kernel(x_ref, g_ref, wk_ref, wv_ref, k5_ref, v5_ref, kb_ref, vb_ref, *, seqs, mem_len):
    h = _rms(x_ref[...], g_ref[0]).astype(BF16)
    for w_ref, o5_ref, ob_ref in ((wk_ref, k5_ref, kb_ref), (wv_ref, v5_ref, vb_ref)):
        y = _dot(h, w_ref[0])
        ob_ref[0] = y.astype(BF16)
        for s in range(seqs):
            for hd in range(M_HEADS):
                o5_ref[0, s, :, hd, :] = y[s * mem_len:(s + 1) * mem_len,
                                           hd * M_HEAD_DIM:(hd + 1) * M_HEAD_DIM]


def _mem_kv(mem, ln_mem, wk, wv, seqs):
    bsz, mem_len, _ = mem.shape
    depth = ln_mem.shape[0]
    width = M_HEADS * M_HEAD_DIM
    n = bsz * mem_len
    tm = seqs * mem_len
    out5 = jax.ShapeDtypeStruct((depth, bsz, mem_len, M_HEADS, M_HEAD_DIM), F32)
    outb = jax.ShapeDtypeStruct((depth, n, width), BF16)
    spec5 = pl.BlockSpec((1, seqs, mem_len, M_HEADS, M_HEAD_DIM), lambda l, i: (l, i, 0, 0, 0))
    specb = pl.BlockSpec((1, tm, width), lambda l, i: (l, i, 0))
    wspec = pl.BlockSpec((1, D_MODEL, width), lambda l, i: (l, 0, 0))
    return pl.pallas_call(
        functools.partial(_mem_kv_kernel, seqs=seqs, mem_len=mem_len),
        grid=(depth, n // tm),
        in_specs=[pl.BlockSpec((tm, D_MODEL), lambda l, i: (i, 0)),
                  pl.BlockSpec((1, 1, D_MODEL), lambda l, i: (l, 0, 0)),
                  wspec, wspec],
        out_specs=[spec5, spec5, specb, specb],
        out_shape=[out5, out5, outb, outb],
        compiler_params=_params(2),
    )(mem.reshape(n, D_MODEL), ln_mem.reshape(depth, 1, D_MODEL), wk, wv)


def _diff_attn_kernel(lam_ref, q_ref, k_ref, v_ref, bias_ref, subln_ref, o_ref,
                      vt_sc, s_sc, acc_sc, *, tq, nd, hp, out_scale):
    i = pl.program_id(2)
    dv = A_V_DIM
    chains = [(h, c) for h in range(hp) for c in range(2)]
    n_ch = len(chains)

    @pl.when(i == 0)
    def _():
        for h in range(hp):
            for jj in range(nd):
                vt_sc[h, jj, :dv, :] = v_ref[0, jj * tq:(jj + 1) * tq, h * dv:(h + 1) * dv].T
                vt_sc[h, jj, dv:, :] = jnp.ones((ONES_ROWS, tq), BF16)

    row = lax.broadcasted_iota(jnp.int32, (dv, tq), 0)
    qzt = []
    for h in range(hp):
        qt = (q_ref[0, :, h * dv:(h + 1) * dv] * (A_QK_DIM ** -0.5)).T
        for c in range(2):
            qzt.append(jnp.where((row >= A_QK_DIM) == (c == 1), qt, jnp.zeros_like(qt)))

    def raw_scores(j, n):
        h = chains[n][0]
        start = pl.multiple_of(j * tq, tq)
        return _dot(k_ref[0, pl.ds(start, tq), h * dv:(h + 1) * dv], qzt[n])

    kk = lax.broadcasted_iota(jnp.int32, (tq, tq), 0)
    qq = lax.broadcasted_iota(jnp.int32, (tq, tq), 1)
    diag_penalty = jnp.where((kk // CHUNK) <= (qq // CHUNK), 0.0, NEG_INF)

    def pass1(js, ms):
        ms = list(ms)
        items = [(j, n) for j in js for n in range(n_ch)]
        raw_next = raw_scores(*items[0])
        for idx, (j, n) in enumerate(items):
            raw = raw_next
            if idx + 1 < len(items):
                raw_next = raw_scores(*items[idx + 1])
            h, c = chains[n]
            if c == 0:
                bias = bias_ref[h, j - i + (nd - 1)] + diag_penalty * jnp.where(j == i, 1.0, 0.0)
            s = (raw + bias) * LOG2_E
            s_sc[n, j] = s
            ms[n] = jnp.maximum(ms[n], jnp.max(s, axis=0, keepdims=True))
        return tuple(ms)

    def pass2(js):
        items = [(j, n) for j in js for n in range(n_ch)]
        p_next = jnp.exp2(s_sc[items[0][1], items[0][0]] - m_fin[items[0][1]])
        for idx, (j, n) in enumerate(items):
            p = p_next
            if idx + 1 < len(items):
                jn, nn = items[idx + 1]
                p_next = jnp.exp2(s_sc[nn, jn] - m_fin[nn])
            acc_sc[n] += _dot(vt_sc[chains[n][0], j], p.astype(BF16))

    n_tiles = i + 1
    plan = []
    first = 0
    for width in (4, 2, 1):
        trips = (n_tiles - first) // width
        plan.append((width, trips, first))
        first = first + trips * width

    ms = tuple(jnp.full((1, tq), NEG_INF, F32) for _ in range(n_ch))
    for width, trips, start in plan:
        ms = lax.fori_loop(
            0, trips,
            lambda tt, m, width=width, start=start:
                pass1([start + tt * width + u for u in range(width)], m), ms)
    m_fin = ms

    acc_sc[...] = jnp.zeros(acc_sc.shape, F32)
    for width, trips, start in plan:
        def trip2(tt, carry, width=width, start=start):
            pass2([start + tt * width + u for u in range(width)])
            return carry

        lax.fori_loop(0, trips, trip2, 0)
    for h in range(hp):
        acc0 = acc_sc[2 * h]
        acc1 = acc_sc[2 * h + 1]
        out0 = acc0[:dv] / acc0[dv:dv + 1]
        out1 = acc1[:dv] / acc1[dv:dv + 1]
        o = (out0 - lam_ref[0, 0] * out1).T
        o_ref[:, h * dv:(h + 1) * dv] = (_head_rms(o) * subln_ref[...] * out_scale).astype(BF16)


def _diff_attn(lam, z3, bias_tiles, subln, tq, hp, out_scale):
    bsz, t, _ = z3.shape
    nd = t // tq
    width = hp * A_V_DIM
    groups = A_HEADS // hp
    return pl.pallas_call(
        functools.partial(_diff_attn_kernel, tq=tq, nd=nd, hp=hp, out_scale=out_scale),
        grid=(bsz, groups, nd),
        in_specs=[pl.BlockSpec(memory_space=pltpu.SMEM),
                  pl.BlockSpec((1, tq, width), lambda b, g, i: (b, i, g)),
                  pl.BlockSpec((1, t, width), lambda b, g, i: (b, 0, groups + g)),
                  pl.BlockSpec((1, t, width), lambda b, g, i: (b, 0, 2 * groups + g)),
                  pl.BlockSpec((hp, nd, tq, tq), lambda b, g, i: (g, 0, 0, 0),
                               pipeline_mode=pl.Buffered(1)),
                  pl.BlockSpec((1, A_V_DIM), lambda b, g, i: (0, 0))],
        out_specs=pl.BlockSpec((tq, width), lambda b, g, i: (b * nd + i, g)),
        out_shape=jax.ShapeDtypeStruct((bsz * t, A_HEADS * A_V_DIM), BF16),
        scratch_shapes=[pltpu.VMEM((hp, nd, A_V_DIM + ONES_ROWS, tq), BF16),
                        pltpu.VMEM((2 * hp, nd, tq, tq), F32),
                        pltpu.VMEM((2 * hp, A_V_DIM + ONES_ROWS, tq), F32)],
        compiler_params=_params(3),
    )(lam, z3, z3, z3, bias_tiles, subln.reshape(1, A_V_DIM))


def _diff_attn_cached_kernel(lam_ref, q_ref, kn_ref, vn_ref, kp_ref, vp_ref, bp_ref, bn_ref,
                             subln_ref, o_ref, *, t, past, out_scale):
    lane = lax.broadcasted_iota(jnp.int32, (t, A_V_DIM), 1)
    qpos_p = past + lax.broadcasted_iota(jnp.int32, (2 * t, past), 0) % t
    kpos_p = lax.broadcasted_iota(jnp.int32, (2 * t, past), 1)
    ok_p = (kpos_p // CHUNK) <= (qpos_p // CHUNK)
    qpos_n = past + lax.broadcasted_iota(jnp.int32, (2 * t, t), 0) % t
    kpos_n = past + lax.broadcasted_iota(jnp.int32, (2 * t, t), 1)
    ok_n = (kpos_n // CHUNK) <= (qpos_n // CHUNK)
    for h in range(A_HEADS):
        cs = slice(h * A_V_DIM, (h + 1) * A_V_DIM)
        q = q_ref[0, :, cs] * (A_QK_DIM ** -0.5)
        zero = jnp.zeros_like(q)
        qp = jnp.concatenate([jnp.where(lane < A_QK_DIM, q, zero),
                              jnp.where(lane >= A_QK_DIM, q, zero)], axis=0)
        kp = kp_ref[0, :, h, :].astype(BF16)
        vp = vp_ref[0, :, h, :].astype(BF16)
        kn = kn_ref[0, :, cs]
        vn = vn_ref[0, :, cs]
        bp = bp_ref[h]
        bn = bn_ref[h]
        sp = jnp.where(ok_p, _dot_nt(qp, kp) + jnp.concatenate([bp, bp], axis=0), NEG_INF)
        sn = jnp.where(ok_n, _dot_nt(qp, kn) + jnp.concatenate([bn, bn], axis=0), NEG_INF)
        m = jnp.maximum(jnp.max(sp, axis=-1, keepdims=True), jnp.max(sn, axis=-1, keepdims=True))
        pp = jnp.exp(sp - m)
        pn = jnp.exp(sn - m)
        l = jnp.sum(pp, axis=-1, keepdims=True) + jnp.sum(pn, axis=-1, keepdims=True)
        out = (_dot(pp.astype(BF16), vp) + _dot(pn.astype(BF16), vn)) / l
        o = out[:t] - lam_ref[0, 0] * out[t:]
        o_ref[:, cs] = (_head_rms(o) * subln_ref[...] * out_scale).astype(BF16)


def _diff_attn_cached(lam, z3, past_k, past_v, bias_past, bias_new, subln, out_scale):
    bsz, t, _ = z3.shape
    past = past_k.shape[1]
    width = A_HEADS * A_V_DIM
    cache_spec = pl.BlockSpec((1, past, A_HEADS, A_V_DIM), lambda b: (b, 0, 0, 0))
    return pl.pallas_call(
        functools.partial(_diff_attn_cached_kernel, t=t, past=past, out_scale=out_scale),
        grid=(bsz,),
        in_specs=[pl.BlockSpec(memory_space=pltpu.SMEM),
                  pl.BlockSpec((1, t, width), lambda b: (b, 0, 0)),
                  pl.BlockSpec((1, t, width), lambda b: (b, 0, 1)),
                  pl.BlockSpec((1, t, width), lambda b: (b, 0, 2)),
                  cache_spec, cache_spec,
                  pl.BlockSpec((A_HEADS, t, past), lambda b: (0, 0, 0)),
                  pl.BlockSpec((A_HEADS, t, t), lambda b: (0, 0, 0)),
                  pl.BlockSpec((1, A_V_DIM), lambda b: (0, 0))],
        out_specs=pl.BlockSpec((t, width), lambda b: (b, 0)),
        out_shape=jax.ShapeDtypeStruct((bsz * t, width), BF16),
        compiler_params=_params(1),
    )(lam, z3, z3, z3, past_k, past_v, bias_past, bias_new, subln.reshape(1, A_V_DIM))


def _retention_kernel(q_ref, k_ref, v_ref, gt_ref, cos_ref, sin_ref, dec_ref, qd_ref, kd_ref,
                      cd_ref, s0_ref, o_ref, s_out_ref, s_sc):
    t = pl.program_id(1)

    @pl.when(t == 0)
    def _():
        s_sc[...] = s0_ref[0]

    d = B_QK_DIM
    half = d // 2
    cos = cos_ref[...]
    sin = sin_ref[...]
    for h in range(B_HEADS):
        cs = slice(h * d, (h + 1) * d)
        q = q_ref[0, :, cs].astype(F32)
        k = k_ref[0, :, cs].astype(F32)
        qr = (q * cos + pltpu.roll(q, half, 1) * sin) * (d ** -0.5)
        kr = k * cos + pltpu.roll(k, half, 1) * sin
        v = v_ref[0, :, cs]
        att = _dot_nt(qr.astype(BF16), kr.astype(BF16)) * dec_ref[h]
        s = s_sc[h]
        o = _dot(att.astype(BF16), v) + _dot((qr * qd_ref[h]).astype(BF16), s.astype(BF16))
        s_sc[h] = s * cd_ref[h] + _dot_tn((kr * kd_ref[h]).astype(BF16), v)
        gt = gt_ref[0, :, cs].astype(F32)
        o_ref[:, cs] = (_head_rms(o) * _silu(gt)).astype(BF16)

    @pl.when(t == pl.num_programs(1) - 1)
    def _():
        s_out_ref[0] = s_sc[...]


def _retention(z3, cos, sin, consts, s0, c):
    bsz, t, _ = z3.shape
    nt = t // c
    dec, qd, kd, cd = consts
    d = B_QK_DIM
    width = B_HEADS * d
    base = 3
    full3 = lambda b, i: (0, 0, 0)
    return pl.pallas_call(
        _retention_kernel,
        grid=(bsz, nt),
        in_specs=[pl.BlockSpec((1, c, width), lambda b, i: (b, i, base)),
                  pl.BlockSpec((1, c, width), lambda b, i: (b, i, base + 1)),
                  pl.BlockSpec((1, c, width), lambda b, i: (b, i, base + 2)),
                  pl.BlockSpec((1, c, width), lambda b, i: (b, i, base + 3)),
                  pl.BlockSpec((c, d), lambda b, i: (i, 0)),
                  pl.BlockSpec((c, d), lambda b, i: (i, 0)),
                  pl.BlockSpec((B_HEADS, c, c), full3),
                  pl.BlockSpec((B_HEADS, c, d), full3),
                  pl.BlockSpec((B_HEADS, c, d), full3),
                  pl.BlockSpec((B_HEADS, 1, d), full3),
                  pl.BlockSpec((1, B_HEADS, d, d), lambda b, i: (b, 0, 0, 0))],
        out_specs=[pl.BlockSpec((c, width), lambda b, i: (b * nt + i, 0)),
                   pl.BlockSpec((1, B_HEADS, d, d), lambda b, i: (b, 0, 0, 0))],
        out_shape=[jax.ShapeDtypeStruct((bsz * t, width), BF16),
                   jax.ShapeDtypeStruct((bsz, B_HEADS, d, d), F32)],
        scratch_shapes=[pltpu.VMEM((B_HEADS, d, d), F32)],
        compiler_params=_params(2),
    )(z3, z3, z3, z3, cos, sin, dec, qd, kd, cd, s0)


def _retention_consts(c):
    log_g = jnp.log1p(-jnp.exp2(-5.0 - jnp.arange(B_HEADS, dtype=F32)))
    idx = jnp.arange(c, dtype=F32)
    dist = idx[:, None] - idx[None, :]
    dec = jnp.where(dist >= 0, jnp.exp(jnp.maximum(dist, 0.0)[None] * log_g[:, None, None]), 0.0)
    qd = jnp.exp((idx[None, :] + 1.0) * log_g[:, None])
    kd = jnp.exp((c - 1.0 - idx)[None, :] * log_g[:, None])
    cd = jnp.exp(c * log_g)
    bc = lambda a: jnp.broadcast_to(a[..., None], a.shape + (B_QK_DIM,))
    return dec, bc(qd), bc(kd), bc(cd[:, None])


def _rotary_tables(pos):
    half = B_QK_DIM // 2
    inv = ROPE_BASE ** (-jnp.arange(half, dtype=F32) / half)
    ang = pos.astype(F32)[:, None] * inv[None, :]
    cos, sin = jnp.cos(ang), jnp.sin(ang)
    return jnp.concatenate([cos, cos], axis=-1), jnp.concatenate([-sin, sin], axis=-1)


def _gla_levels(c):
    return [c >> (l + 1) for l in range(int(math.log2(c)))]


def _gla_consts(c):
    levels = _gla_levels(c)
    rows = np.arange(c)
    mats = []
    for s in levels:
        ref = (rows // (2 * s)) * 2 * s + s - 1
        a = np.zeros((c, c), np.float32)
        for i in range(c):
            if i & s:
                a[i, ref[i] + 1:i + 1] = 1.0
            else:
                a[i, i + 1:ref[i] + 1] = 1.0
        mats.append(a)
    mats.append(np.tril(np.ones((c, c), np.float32)))
    mats.append(np.triu(np.ones((c, c), np.float32), 1))
    lv = np.full((c, c), -1, np.int32)
    for i in range(c):
        lv[i, i] = len(levels)
        for j in range(i):
            lv[i, j] = levels.index(1 << int(math.floor(math.log2(i ^ j))))
    return jnp.asarray(np.concatenate(mats, axis=0), BF16), jnp.asarray(lv)


def _gla_kernel(q_ref, k_ref, v_ref, r_ref, g_ref, s0_ref, a_ref, lv_ref, nw_ref,
                o_ref, s_out_ref, st_sc, *, c):
    t = pl.program_id(1)
    dk, dv = C_QK_DIM, C_V_DIM

    @pl.when(t == 0)
    def _():
        for h in range(C_HEADS):
            st_sc[h] = s0_ref[0, h].T

    levels = _gla_levels(c)
    n_lv = len(levels)
    lv = lv_ref[...]
    row = lax.broadcasted_iota(jnp.int32, (c, dk), 0)
    g = g_ref[0]
    xs = _dot(a_ref[...], g.astype(BF16))
    for h in range(C_HEADS):
        ks = slice(h * dk, (h + 1) * dk)
        vs = slice(h * dv, (h + 1) * dv)
        x = xs[:, ks]
        q = q_ref[0, :, ks].astype(F32) * (dk ** -0.5)
        k = k_ref[0, :, ks].astype(F32)
        v = v_ref[0, :, vs]
        att = jnp.zeros((c, c), F32)
        pending = (n_lv, _dot_nt(q.astype(BF16), k.astype(BF16)))
        for l, s in enumerate(levels):
            e = jnp.exp(x[l * c:(l + 1) * c])
            up = (row & s) != 0
            mix = jnp.where(up, q, k) * e
            qt = jnp.where(up, mix, 0.0).astype(BF16)
            kt = jnp.where(up, 0.0, mix).astype(BF16)
            prod = _dot_nt(qt, kt)
            att = jnp.where(lv == pending[0], pending[1], att)
            pending = (l, prod)
        att = jnp.where(lv == pending[0], pending[1], att)
        b = x[n_lv * c:(n_lv + 1) * c]
        rem = x[(n_lv + 1) * c:(n_lv + 2) * c]
        st = st_sc[h]
        o = _dot(att.astype(BF16), v) + _dot_nt((q * jnp.exp(b)).astype(BF16), st.astype(BF16))
        kd = (k * jnp.exp(rem)).astype(BF16)
        st_sc[h] = st * jnp.exp(b[c - 1:c, :]) + _dot_tn(v, kd)
        r = r_ref[0, :, vs].astype(F32)
        o_ref[:, vs] = (_head_rms(o) * nw_ref[...] * _silu(r)).astype(BF16)

    @pl.when(t == pl.num_programs(1) - 1)
    def _():
        for h in range(C_HEADS):
            s_out_ref[0, h] = st_sc[h].T


def _gla(z3, log_a3, s0, a_stack, lv, norm_w, c):
    bsz, t, _ = z3.shape
    nt = t // c
    dk, dv = C_QK_DIM, C_V_DIM
    qk_w, v_w = C_HEADS * dk, C_HEADS * dv
    const = lambda b, i: (0, 0)
    return pl.pallas_call(
        functools.partial(_gla_kernel, c=c),
        grid=(bsz, nt),
        in_specs=[pl.BlockSpec((1, c, qk_w), lambda b, i: (b, i, 0)),
                  pl.BlockSpec((1, c, qk_w), lambda b, i: (b, i, 1)),
                  pl.BlockSpec((1, c, v_w), lambda b, i: (b, i, 1)),
                  pl.BlockSpec((1, c, v_w), lambda b, i: (b, i, 2)),
                  pl.BlockSpec((1, c, qk_w), lambda b, i: (b, i, 0)),
                  pl.BlockSpec((1, C_HEADS, dk, dv), lambda b, i: (b, 0, 0, 0)),
                  pl.BlockSpec(a_stack.shape, const),
                  pl.BlockSpec((c, c), const),
                  pl.BlockSpec((1, dv), const)],
        out_specs=[pl.BlockSpec((c, v_w), lambda b, i: (b * nt + i, 0)),
                   pl.BlockSpec((1, C_HEADS, dk, dv), lambda b, i: (b, 0, 0, 0))],
        out_shape=[jax.ShapeDtypeStruct((bsz * t, v_w), BF16),
                   jax.ShapeDtypeStruct((bsz, C_HEADS, dk, dv), F32)],
        scratch_shapes=[pltpu.VMEM((C_HEADS, dv, dk), F32)],
        compiler_params=_params(2),
    )(z3, z3, z3, z3, log_a3, s0, a_stack, lv, norm_w.reshape(1, dv))


def _cross_kernel(*refs, n_mix):
    x_ref = refs[0]
    a_refs = refs[1:1 + n_mix]
    w_refs = refs[1 + n_mix:1 + 2 * n_mix]
    g_ref, wq_ref, mk_ref, mv_ref, wo_ref, o_ref = refs[1 + 2 * n_mix:]
    x = x_ref[...]
    for a_ref, w_ref in zip(a_refs, w_refs):
        x = x + _dot(a_ref[...], w_ref[...])
    h = _rms(x, g_ref[...]).astype(BF16)
    q = (_dot(h, wq_ref[...]) * (M_HEAD_DIM ** -0.5)).astype(BF16)
    def head(ref, hd):
        if len(ref.shape) == 4:
            return ref[0, :, hd, :].astype(BF16)
        return ref[0, :, hd * M_HEAD_DIM:(hd + 1) * M_HEAD_DIM].astype(BF16)

    outs = []
    for hd in range(M_HEADS):
        cs = slice(hd * M_HEAD_DIM, (hd + 1) * M_HEAD_DIM)
        s = _dot_nt(q[:, cs], head(mk_ref, hd))
        p = jnp.exp(s - jnp.max(s, axis=-1, keepdims=True))
        l = jnp.sum(p, axis=-1, keepdims=True)
        outs.append((_dot(p.astype(BF16), head(mv_ref, hd)) / l).astype(BF16))
    o_ref[...] = x + _dot(jnp.concatenate(outs, axis=1), wo_ref[...])


def _cross(x2d, mix_list, w_mix_list, gamma, wq, mk3, mv3, wo, t, tm):
    n = x2d.shape[0]
    nt = t // tm
    width = M_HEADS * M_HEAD_DIM
    rows = lambda b, i: (b * nt + i, 0)
    const = lambda b, i: (0, 0)
    mem_spec = pl.BlockSpec((1,) + mk3.shape[1:], lambda b, i: (b,) + (0,) * (mk3.ndim - 1))
    in_specs = [pl.BlockSpec((tm, D_MODEL), rows)]
    in_specs += [pl.BlockSpec((tm, a.shape[1]), rows) for a in mix_list]
    in_specs += [pl.BlockSpec(w.shape, const) for w in w_mix_list]
    in_specs += [pl.BlockSpec((1, D_MODEL), const),
                 pl.BlockSpec((D_MODEL, width), const),
                 mem_spec, mem_spec,
                 pl.BlockSpec((width, D_MODEL), const)]
    return pl.pallas_call(
        functools.partial(_cross_kernel, n_mix=len(mix_list)),
        grid=(n // t, nt),
        in_specs=in_specs,
        out_specs=pl.BlockSpec((tm, D_MODEL), rows),
        out_shape=jax.ShapeDtypeStruct((n, D_MODEL), F32),
        compiler_params=_params(2),
    )(x2d, *mix_list, *w_mix_list, gamma.reshape(1, D_MODEL), wq, mk3, mv3, wo)


def _ffn_kernel(x_ref, g_ref, wg_ref, wu_ref, cw_ref, cb_ref, prev_ref, wd_ref, *rest,
                tm, final_norm):
    if final_norm:
        gf_ref, o_ref, tail_ref, carry = rest
    else:
        o_ref, tail_ref, carry = rest
    t = pl.program_id(1)
    lo = SUBLANES - 2

    @pl.when(t == 0)
    def _():
        carry[lo:SUBLANES, :] = prev_ref[0]

    x = x_ref[...]
    h = _rms(x, g_ref[...]).astype(BF16)
    row = lax.broadcasted_iota(jnp.int32, (SUBLANES, FF_CHUNK), 0)
    acc = x
    n_chunks = D_FF // FF_CHUNK
    chunk = lambda c: slice(c * FF_CHUNK, (c + 1) * FF_CHUNK)
    nxt = (_dot(h, wg_ref[:, chunk(0)]), _dot(h, wu_ref[:, chunk(0)]))
    pending = []
    for c in range(n_chunks):
        cs = chunk(c)
        gate, up = nxt
        if c + 1 < n_chunks:
            nxt = (_dot(h, wg_ref[:, chunk(c + 1)]), _dot(h, wu_ref[:, chunk(c + 1)]))
        p1 = carry[SUBLANES - 1:SUBLANES, cs]
        p2 = carry[lo:lo + 1, cs]
        r1 = pltpu.roll(gate, 1, 0)
        r2 = pltpu.roll(gate, 2, 0)
        h1 = jnp.where(row == 0, p1, r1[:SUBLANES])
        h2 = jnp.where(row == 0, p2, jnp.where(row == 1, p1, r2[:SUBLANES]))
        g1 = jnp.concatenate([h1, r1[SUBLANES:]], axis=0)
        g2 = jnp.concatenate([h2, r2[SUBLANES:]], axis=0)
        carry[:, cs] = gate[tm - SUBLANES:tm]
        conv = cb_ref[:, cs] + cw_ref[0:1, cs] * g2
        conv = conv + cw_ref[1:2, cs] * g1
        conv = conv + cw_ref[2:3, cs] * gate
        pending.append((_gelu_tanh(conv) * up).astype(BF16))
        if len(pending) == DOWN_GROUP or c == n_chunks - 1:
            lo_col = (c + 1 - len(pending)) * FF_CHUNK
            acc = acc + _dot(jnp.concatenate(pending, axis=1), wd_ref[lo_col:(c + 1) * FF_CHUNK, :])
            pending = []
    if final_norm:
        o_ref[...] = _rms(acc, gf_ref[...])
    else:
        o_ref[...] = acc

    @pl.when(t == pl.num_programs(1) - 1)
    def _():
        tail_ref[0] = carry[lo:SUBLANES, :]


def _ffn(x2d, gamma, wg, wu, conv_w, conv_b, prev, wd, t, tm, final_gamma=None):
    n = x2d.shape[0]
    nt = t // tm
    bsz = n // t
    const = lambda b, i: (0, 0)
    resident = dict(pipeline_mode=pl.Buffered(1))
    in_specs = [pl.BlockSpec((tm, D_MODEL), lambda b, i: (b * nt + i, 0)),
                pl.BlockSpec((1, D_MODEL), const),
                pl.BlockSpec((D_MODEL, D_FF), const, **resident),
                pl.BlockSpec((D_MODEL, D_FF), const, **resident),
                pl.BlockSpec((3, D_FF), const),
                pl.BlockSpec((1, D_FF), const),
                pl.BlockSpec((1, 2, D_FF), lambda b, i: (b, 0, 0)),
                pl.BlockSpec((D_FF, D_MODEL), const, **resident)]
    args = [x2d, gamma.reshape(1, D_MODEL), wg, wu, conv_w, conv_b.reshape(1, D_FF), prev, wd]
    if final_gamma is not None:
        in_specs.append(pl.BlockSpec((1, D_MODEL), const))
        args.append(final_gamma.reshape(1, D_MODEL))
    return pl.pallas_call(
        functools.partial(_ffn_kernel, tm=tm, final_norm=final_gamma is not None),
        grid=(bsz, nt),
        in_specs=in_specs,
        out_specs=[pl.BlockSpec((tm, D_MODEL), lambda b, i: (b * nt + i, 0)),
                   pl.BlockSpec((1, 2, D_FF), lambda b, i: (b, 0, 0))],
        out_shape=[jax.ShapeDtypeStruct((n, D_MODEL), F32),
                   jax.ShapeDtypeStruct((bsz, 2, D_FF), F32)],
        scratch_shapes=[pltpu.VMEM((SUBLANES, D_FF), F32)],
        compiler_params=_params(2),
    )(*args)


def _rel_bucket(rel):
    nb = REL_BUCKETS // 2
    max_exact = nb // 2
    n = jnp.abs(rel)
    nf = jnp.maximum(n, 1).astype(F32)
    large = max_exact + (jnp.log(nf / max_exact) / math.log(REL_MAX_DIST / max_exact)
                         * (nb - max_exact)).astype(jnp.int32)
    large = jnp.minimum(large, nb - 1)
    return jnp.where(rel > 0, nb, 0) + jnp.where(n < max_exact, n, large)


def _toeplitz_kernel(w_ref, o_ref, *, rows, cols):
    w = w_ref[0]
    x = jnp.broadcast_to(w, (rows, w.shape[1]))
    o_ref[0] = pltpu.roll(x, 0, 1, stride=1, stride_axis=0)[:, :cols]


def _toeplitz(w, rows, cols):
    groups, period = w.shape
    return pl.pallas_call(
        functools.partial(_toeplitz_kernel, rows=rows, cols=cols),
        grid=(groups,),
        in_specs=[pl.BlockSpec((1, 1, period), lambda g: (g, 0, 0))],
        out_specs=pl.BlockSpec((1, rows, cols), lambda g: (g, 0, 0)),
        out_shape=jax.ShapeDtypeStruct((groups, rows, cols), F32),
        compiler_params=_params(1),
    )(w.reshape(groups, 1, period))


def _bias_tiles(rel_bias, tq, nd):
    period = 2 * tq
    n = np.arange(period)
    rel = np.stack([np.where(n < tq, (d - (nd - 1)) * tq - n, (d - (nd - 1)) * tq + period - n)
                    for d in range(nd)]).astype(np.int32)
    w = jnp.transpose(rel_bias[_rel_bucket(jnp.asarray(rel))], (2, 0, 1)).astype(F32)
    return _toeplitz(w.reshape(A_HEADS * nd, period), tq, tq).reshape(A_HEADS, nd, tq, tq)


def _bias_rows(rel_bias, t, past):
    n_keys = past + t
    period = -(-(n_keys + t) // LANES) * LANES
    m = np.arange(period)
    rel = np.where(m < n_keys, m - past, m - period - past).astype(np.int32)
    w = jnp.transpose(rel_bias[_rel_bucket(jnp.asarray(rel))], (1, 0)).astype(F32)
    bias = _toeplitz(w, t, n_keys)
    return bias[:, :, :past], bias[:, :, past:]


def _forward(x, past_k, past_v, ret_state, gla_state, conv_prev, mem_k, mem_v, p, w):
    bsz, t, _ = x.shape
    n = bsz * t
    past = 0 if past_k is None else past_k.shape[2]
    tm = min(512, n)
    tseq = min(512, t)
    tffn = min(256, t)
    tq = min(256, t)
    c_ret = min(256, t)
    c_gla = min(256, t)
    x2 = x.reshape(n, D_MODEL)
    pos = past + jnp.arange(t, dtype=jnp.int32)
    new_conv = []

    z, ak, av = _in_even(x2, p["ln_mix"][0], w["w_in_even"], tm)
    z3 = z.reshape(bsz, t, EVEN_IN)
    lam_init = 0.8 - 0.6 * math.exp(-0.3 * 0)
    lam = (jnp.exp(jnp.sum(p["diff_lq1"][0].astype(F32) * p["diff_lk1"][0].astype(F32)))
           - jnp.exp(jnp.sum(p["diff_lq2"][0].astype(F32) * p["diff_lk2"][0].astype(F32)))
           + lam_init).reshape(1, 1).astype(F32)
    if past == 0:
        o_a = _diff_attn(lam, z3, _bias_tiles(p["rel_bias"], tq, t // tq), p["diff_subln"][0],
                         tq, A_HEADS, 1.0 - lam_init)
    else:
        bias_past, bias_new = _bias_rows(p["rel_bias"], t, past)
        o_a = _diff_attn_cached(lam, z3, past_k[0], past_v[0], bias_past, bias_new,
                                p["diff_subln"][0], 1.0 - lam_init)
    cos, sin = _rotary_tables(pos)
    o_b, s_ret = _retention(z3, cos, sin, _retention_consts(c_ret), ret_state[0], c_ret)
    x2 = _cross(x2, [o_a, o_b], [w["w_out_even_a"], w["w_out_even_b"]], p["ln_cross"][0],
                w["w_cq"][0], mem_k[0], mem_v[0], w["w_co"][0], t, tseq)
    x2, tail = _ffn(x2, p["ln_ffn"][0], w["w_ffn_gate"][0], w["w_ffn_up"][0], p["ffn_conv_w"][0],
                    p["ffn_conv_b"][0], conv_prev[0], w["w_ffn_down"][0], t, tffn)
    new_conv.append(tail)

    zc, log_a = _in_odd(x2, p["ln_mix"][1], w["w_in_odd"], w["w_gate_lr"], p["b_gate"][0], tm)
    a_stack, lv = _gla_consts(c_gla)
    o_c, s_gla = _gla(zc.reshape(bsz, t, ODD_Z), log_a.reshape(bsz, t, C_HEADS * C_QK_DIM),
                      gla_state[0], a_stack, lv, p["gla_norm"][0], c_gla)
    x2 = _cross(x2, [o_c], [w["w_out_odd"]], p["ln_cross"][1],
                w["w_cq"][1], mem_k[1], mem_v[1], w["w_co"][1], t, tseq)
    y, tail = _ffn(x2, p["ln_ffn"][1], w["w_ffn_gate"][1], w["w_ffn_up"][1], p["ffn_conv_w"][1],
                   p["ffn_conv_b"][1], conv_prev[1], w["w_ffn_down"][1], t, tffn,
                   final_gamma=p["ln_final"])
    new_conv.append(tail)

    return (y.reshape(bsz, t, D_MODEL),
            ak.reshape(1, bsz, t, A_HEADS, A_V_DIM), av.reshape(1, bsz, t, A_HEADS, A_V_DIM),
            s_ret[None], s_gla[None], jnp.stack(new_conv))


def kernel(x_prompt, x_sample, cache_diff_k, cache_diff_v, state_retention, state_gla, cache_ffn_conv, cache_mem_k, cache_mem_v, mem_prompt, ln_mix, ln_cross, ln_ffn, ln_mem, ln_final, w_in_even, w_out_even, diff_lq1, diff_lk1, diff_lq2, diff_lk2, diff_subln, rel_bias, w_in_odd, w_gate_lr, b_gate, gla_norm, w_out_odd, w_cq, w_ck, w_cv, w_co, w_ffn_gate, w_ffn_up, ffn_conv_w, ffn_conv_b, w_ffn_down):
    p = dict(ln_mix=ln_mix, ln_cross=ln_cross, ln_ffn=ln_ffn, ln_final=ln_final,
             diff_lq1=diff_lq1, diff_lk1=diff_lk1, diff_lq2=diff_lq2, diff_lk2=diff_lk2,
             diff_subln=diff_subln, rel_bias=rel_bias, b_gate=b_gate, gla_norm=gla_norm,
             ffn_conv_w=ffn_conv_w, ffn_conv_b=ffn_conv_b)
    a_v = A_HEADS * A_V_DIM
    w = dict(
        w_in_even=w_in_even[0].astype(BF16),
        w_out_even_a=w_out_even[0, :a_v].astype(BF16),
        w_out_even_b=w_out_even[0, a_v:].astype(BF16),
        w_in_odd=jnp.pad(w_in_odd[0], ((0, 0), (0, ODD_IN_PAD - w_in_odd.shape[2]))).astype(BF16),
        w_gate_lr=jnp.pad(w_gate_lr[0], ((0, LANES - C_GATE_RANK), (0, 0))).astype(BF16),
        w_out_odd=w_out_odd[0].astype(BF16),
        w_cq=w_cq.astype(BF16), w_co=w_co.astype(BF16),
        w_ffn_gate=w_ffn_gate.astype(BF16), w_ffn_up=w_ffn_up.astype(BF16),
        w_ffn_down=w_ffn_down.astype(BF16))

    bp, mem_len, _ = mem_prompt.shape
    m_width = M_HEADS * M_HEAD_DIM
    depth = ln_mem.shape[0]
    mem_k_p, mem_v_p, mk_b, mv_b = _mem_kv(mem_prompt, ln_mem, w_ck.astype(BF16),
                                           w_cv.astype(BF16), seqs=2)

    dt = x_prompt.dtype
    zero_ret = jnp.zeros((1, bp, B_HEADS, B_QK_DIM, B_QK_DIM), dt)
    zero_gla = jnp.zeros((1, bp, C_HEADS, C_QK_DIM, C_V_DIM), dt)
    zero_conv = jnp.zeros((depth, bp, 2, D_FF), dt)
    y_p, dk_p, dv_p, ret_p, gla_p, conv_p = _forward(
        x_prompt, None, None, zero_ret, zero_gla, zero_conv,
        mk_b.reshape(depth, bp, mem_len, m_width), mv_b.reshape(depth, bp, mem_len, m_width), p, w)
    y_s, dk_s, dv_s, ret_s, gla_s, conv_s = _forward(
        x_sample, cache_diff_k, cache_diff_v, state_retention, state_gla, cache_ffn_conv,
        cache_mem_k, cache_mem_v, p, w)
    return (y_p, y_s, dk_p, dv_p, ret_p, gla_p, conv_p, mem_k_p, mem_v_p,
            dk_s, dv_s, ret_s, gla_s, conv_s)
```

```python
import functools
import math

import numpy as np
import jax
import jax.numpy as jnp
from jax import lax
from jax.experimental import pallas as pl
from jax.experimental.pallas import tpu as pltpu

F32 = jnp.float32
BF16 = jnp.bfloat16

D_MODEL = 1024
CHUNK = 64
A_HEADS = 4
A_QK_DIM = 64
A_V_DIM = 128
B_HEADS = 4
B_QK_DIM = 128
C_HEADS = 4
C_QK_DIM = 128
C_V_DIM = 256
C_GATE_RANK = 16
C_GATE_TAU = 16.0
M_HEADS = 4
M_HEAD_DIM = 256
REL_BUCKETS = 32
REL_MAX_DIST = 128
D_FF = 2816
ROPE_BASE = 10000.0
EPS = 1e-6
NEG_INF = -1e30

EVEN_IN = 3584
ODD_Z = 3072
LANES = 128
SUBLANES = 8
ODD_IN_PAD = ODD_Z + LANES
FF_CHUNK = 256
DOWN_GROUP = 6
ONES_ROWS = 16
LOG2_E = math.log2(math.e)
V7X_VMEM_LIMIT_BYTES = 56 * 1024 * 1024


def _params(n_axes):
    return pltpu.CompilerParams(dimension_semantics=("arbitrary",) * n_axes,
                                vmem_limit_bytes=V7X_VMEM_LIMIT_BYTES)


def _dot(a, b):
    return jnp.dot(a, b, preferred_element_type=F32)


def _dot_nt(a, b):
    return lax.dot_general(a, b, (((1,), (1,)), ((), ())), preferred_element_type=F32)


def _dot_tn(a, b):
    return lax.dot_general(a, b, (((0,), (0,)), ((), ())), preferred_element_type=F32)


def _rms(x, g):
    return x * lax.rsqrt(jnp.mean(x * x, axis=-1, keepdims=True) + EPS) * g


def _head_rms(x):
    return x * lax.rsqrt(jnp.mean(x * x, axis=-1, keepdims=True) + EPS)


def _silu(x):
    return x * (1.0 / (1.0 + jnp.exp(-x)))


def _gelu_tanh(x):
    c0 = math.sqrt(2.0 / math.pi)
    return x * (0.5 + 0.5 * jnp.tanh(x * (c0 + (c0 * 0.044715) * (x * x))))


def _log_sigmoid(x):
    return jnp.minimum(x, 0.0) - jnp.log(1.0 + jnp.exp(-jnp.abs(x)))


def _store_head_major(o_ref, val, tm, n_heads, head_dim):
    tiles = head_dim // LANES
    group = n_heads * tiles
    for hd in range(n_heads):
        for part in range(tiles):
            col = hd * head_dim + part * LANES
            o_ref[pl.ds(part * n_heads + hd, tm, stride=group), :] = val[:, col:col + LANES]


def _in_even_kernel(x_ref, g_ref, w_ref, z_ref, k_ref, v_ref, *, tm):
    h = _rms(x_ref[...], g_ref[...]).astype(BF16)
    width = A_HEADS * A_V_DIM
    for c in range(EVEN_IN // width):
        zc = _dot(h, w_ref[:, c * width:(c + 1) * width])
        z_ref[:, c * width:(c + 1) * width] = zc.astype(BF16)
        if c == 1:
            _store_head_major(k_ref, zc, tm, A_HEADS, A_V_DIM)
        if c == 2:
            _store_head_major(v_ref, zc, tm, A_HEADS, A_V_DIM)


def _in_even(x2d, gamma, w, tm):
    n = x2d.shape[0]
    return pl.pallas_call(
        functools.partial(_in_even_kernel, tm=tm),
        grid=(n // tm,),
        in_specs=[pl.BlockSpec((tm, D_MODEL), lambda i: (i, 0)),
                  pl.BlockSpec((1, D_MODEL), lambda i: (0, 0)),
                  pl.BlockSpec((D_MODEL, EVEN_IN), lambda i: (0, 0))],
        out_specs=[pl.BlockSpec((tm, EVEN_IN), lambda i: (i, 0)),
                   pl.BlockSpec((tm * A_HEADS, A_V_DIM), lambda i: (i, 0)),
                   pl.BlockSpec((tm * A_HEADS, A_V_DIM), lambda i: (i, 0))],
        out_shape=[jax.ShapeDtypeStruct((n, EVEN_IN), BF16),
                   jax.ShapeDtypeStruct((n * A_HEADS, A_V_DIM), F32),
                   jax.ShapeDtypeStruct((n * A_HEADS, A_V_DIM), F32)],
        compiler_params=_params(1),
    )(x2d, gamma.reshape(1, D_MODEL), w)


def _in_odd_kernel(x_ref, g_ref, w_ref, wlr_ref, bg_ref, z_ref, la_ref):
    h = _rms(x_ref[...], g_ref[...]).astype(BF16)
    width = 512
    ca = _dot(h, w_ref[:, ODD_Z:ODD_IN_PAD]).astype(BF16)
    pre = _dot(ca, wlr_ref[...]) + bg_ref[...]
    la_ref[...] = _log_sigmoid(pre) / C_GATE_TAU
    for c in range(ODD_Z // width):
        z_ref[:, c * width:(c + 1) * width] = _dot(h, w_ref[:, c * width:(c + 1) * width]).astype(BF16)


def _in_odd(x2d, gamma, w_pad, wlr_pad, b_gate, tm):
    n = x2d.shape[0]
    qk = C_HEADS * C_QK_DIM
    return pl.pallas_call(
        _in_odd_kernel,
        grid=(n // tm,),
        in_specs=[pl.BlockSpec((tm, D_MODEL), lambda i: (i, 0)),
                  pl.BlockSpec((1, D_MODEL), lambda i: (0, 0)),
                  pl.BlockSpec((D_MODEL, ODD_IN_PAD), lambda i: (0, 0)),
                  pl.BlockSpec((LANES, qk), lambda i: (0, 0)),
                  pl.BlockSpec((1, qk), lambda i: (0, 0))],
        out_specs=[pl.BlockSpec((tm, ODD_Z), lambda i: (i, 0)),
                   pl.BlockSpec((tm, qk), lambda i: (i, 0))],
        out_shape=[jax.ShapeDtypeStruct((n, ODD_Z), BF16),
                   jax.ShapeDtypeStruct((n, qk), F32)],
        compiler_params=_params(1),
    )(x2d, gamma.reshape(1, D_MODEL), w_pad, wlr_pad, b_gate.reshape(1, qk))


def _mem_kv_kernel(x_ref, g_ref, wk_ref, wv_ref, k5_ref, v5_ref, kb_ref, vb_ref, *, seqs, mem_len):
    h = _rms(x_ref[...], g_ref[0]).astype(BF16)
    for w_ref, o5_ref, ob_ref in ((wk_ref, k5_ref, kb_ref), (wv_ref, v5_ref, vb_ref)):
        y = _dot(h, w_ref[0])
        ob_ref[0] = y.astype(BF16)
        for s in range(seqs):
            for hd in range(M_HEADS):
                o5_ref[0, s, :, hd, :] = y[s * mem_len:(s + 1) * mem_len,
                                           hd * M_HEAD_DIM:(hd + 1) * M_HEAD_DIM]


def _mem_kv(mem, ln_mem, wk, wv, seqs):
    bsz, mem_len, _ = mem.shape
    depth = ln_mem.shape[0]
    width = M_HEADS * M_HEAD_DIM
    n = bsz * mem_len
    tm = seqs * mem_len
    out5 = jax.ShapeDtypeStruct((depth, bsz, mem_len, M_HEADS, M_HEAD_DIM), F32)
    outb = jax.ShapeDtypeStruct((depth, n, width), BF16)
    spec5 = pl.BlockSpec((1, seqs, mem_len, M_HEADS, M_HEAD_DIM), lambda l, i: (l, i, 0, 0, 0))
    specb = pl.BlockSpec((1, tm, width), lambda l, i: (l, i, 0))
    wspec = pl.BlockSpec((1, D_MODEL, width), lambda l, i: (l, 0, 0))
    return pl.pallas_call(
        functools.partial(_mem_kv_kernel, seqs=seqs, mem_len=mem_len),
        grid=(depth, n // tm),
        in_specs=[pl.BlockSpec((tm, D_MODEL), lambda l, i: (i, 0)),
                  pl.BlockSpec((1, 1, D_MODEL), lambda l, i: (l, 0, 0)),
                  wspec, wspec],
        out_specs=[spec5, spec5, specb, specb],
        out_shape=[out5, out5, outb, outb],
        compiler_params=_params(2),
    )(mem.reshape(n, D_MODEL), ln_mem.reshape(depth, 1, D_MODEL), wk, wv)


def _diff_attn_kernel(lam_ref, q_ref, k_ref, v_ref, bias_ref, subln_ref, o_ref,
                      vt_sc, s_sc, acc_sc, *, tq, nd, hp, out_scale):
    i = pl.program_id(2)
    dv = A_V_DIM
    chains = [(h, c) for h in range(hp) for c in range(2)]
    n_ch = len(chains)

    @pl.when(i == 0)
    def _():
        for h in range(hp):
            for jj in range(nd):
                vt_sc[h, jj, :dv, :] = v_ref[0, jj * tq:(jj + 1) * tq, h * dv:(h + 1) * dv].T
                vt_sc[h, jj, dv:, :] = jnp.ones((ONES_ROWS, tq), BF16)

    row = lax.broadcasted_iota(jnp.int32, (dv, tq), 0)
    qzt = []
    for h in range(hp):
        qt = (q_ref[0, :, h * dv:(h + 1) * dv] * (A_QK_DIM ** -0.5)).T
        for c in range(2):
            qzt.append(jnp.where((row >= A_QK_DIM) == (c == 1), qt, jnp.zeros_like(qt)))

    def raw_scores(j, n):
        h = chains[n][0]
        start = pl.multiple_of(j * tq, tq)
        return _dot(k_ref[0, pl.ds(start, tq), h * dv:(h + 1) * dv], qzt[n])

    kk = lax.broadcasted_iota(jnp.int32, (tq, tq), 0)
    qq = lax.broadcasted_iota(jnp.int32, (tq, tq), 1)
    diag_penalty = jnp.where((kk // CHUNK) <= (qq // CHUNK), 0.0, NEG_INF)

    def pass1(js, ms):
        ms = list(ms)
        items = [(j, n) for j in js for n in range(n_ch)]
        raw_next = raw_scores(*items[0])
        for idx, (j, n) in enumerate(items):
            raw = raw_next
            if idx + 1 < len(items):
                raw_next = raw_scores(*items[idx + 1])
            h, c = chains[n]
            if c == 0:
                bias = bias_ref[h, j - i + (nd - 1)] + diag_penalty * jnp.where(j == i, 1.0, 0.0)
            s = (raw + bias) * LOG2_E
            s_sc[n, j] = s
            ms[n] = jnp.maximum(ms[n], jnp.max(s, axis=0, keepdims=True))
        return tuple(ms)

    def pass2(js):
        items = [(j, n) for j in js for n in range(n_ch)]
        p_next = jnp.exp2(s_sc[items[0][1], items[0][0]] - m_fin[items[0][1]])
        for idx, (j, n) in enumerate(items):
            p = p_next
            if idx + 1 < len(items):
                jn, nn = items[idx + 1]
                p_next = jnp.exp2(s_sc[nn, jn] - m_fin[nn])
            acc_sc[n] += _dot(vt_sc[chains[n][0], j], p.astype(BF16))

    n_tiles = i + 1
    plan = []
    first = 0
    for width in (4, 2, 1):
        trips = (n_tiles - first) // width
        plan.append((width, trips, first))
        first = first + trips * width

    ms = tuple(jnp.full((1, tq), NEG_INF, F32) for _ in range(n_ch))
    for width, trips, start in plan:
        ms = lax.fori_loop(
            0, trips,
            lambda tt, m, width=width, start=start:
                pass1([start + tt * width + u for u in range(width)], m), ms)
    m_fin = ms

    acc_sc[...] = jnp.zeros(acc_sc.shape, F32)
    for width, trips, start in plan:
        def trip2(tt, carry, width=width, start=start):
            pass2([start + tt * width + u for u in range(width)])
            return carry

        lax.fori_loop(0, trips, trip2, 0)
    for h in range(hp):
        acc0 = acc_sc[2 * h]
        acc1 = acc_sc[2 * h + 1]
        out0 = acc0[:dv] / acc0[dv:dv + 1]
        out1 = acc1[:dv] / acc1[dv:dv + 1]
        o = (out0 - lam_ref[0, 0] * out1).T
        o_ref[:, h * dv:(h + 1) * dv] = (_head_rms(o) * subln_ref[...] * out_scale).astype(BF16)


def _diff_attn(lam, z3, bias_tiles, subln, tq, hp, out_scale):
    bsz, t, _ = z3.shape
    nd = t // tq
    width = hp * A_V_DIM
    groups = A_HEADS // hp
    return pl.pallas_call(
        functools.partial(_diff_attn_kernel, tq=tq, nd=nd, hp=hp, out_scale=out_scale),
        grid=(bsz, groups, nd),
        in_specs=[pl.BlockSpec(memory_space=pltpu.SMEM),
                  pl.BlockSpec((1, tq, width), lambda b, g, i: (b, i, g)),
                  pl.BlockSpec((1, t, width), lambda b, g, i: (b, 0, groups + g)),
                  pl.BlockSpec((1, t, width), lambda b, g, i: (b, 0, 2 * groups + g)),
                  pl.BlockSpec((hp, nd, tq, tq), lambda b, g, i: (g, 0, 0, 0),
                               pipeline_mode=pl.Buffered(1)),
                  pl.BlockSpec((1, A_V_DIM), lambda b, g, i: (0, 0))],
        out_specs=pl.BlockSpec((tq, width), lambda b, g, i: (b * nd + i, g)),
        out_shape=jax.ShapeDtypeStruct((bsz * t, A_HEADS * A_V_DIM), BF16),
        scratch_shapes=[pltpu.VMEM((hp, nd, A_V_DIM + ONES_ROWS, tq), BF16),
                        pltpu.VMEM((2 * hp, nd, tq, tq), F32),
                        pltpu.VMEM((2 * hp, A_V_DIM + ONES_ROWS, tq), F32)],
        compiler_params=_params(3),
    )(lam, z3, z3, z3, bias_tiles, subln.reshape(1, A_V_DIM))


def _diff_attn_cached_kernel(lam_ref, q_ref, kn_ref, vn_ref, kp_ref, vp_ref, bp_ref, bn_ref,
                             subln_ref, o_ref, *, t, past, out_scale):
    lane = lax.broadcasted_iota(jnp.int32, (t, A_V_DIM), 1)
    qpos_p = past + lax.broadcasted_iota(jnp.int32, (2 * t, past), 0) % t
    kpos_p = lax.broadcasted_iota(jnp.int32, (2 * t, past), 1)
    ok_p = (kpos_p // CHUNK) <= (qpos_p // CHUNK)
    qpos_n = past + lax.broadcasted_iota(jnp.int32, (2 * t, t), 0) % t
    kpos_n = past + lax.broadcasted_iota(jnp.int32, (2 * t, t), 1)
    ok_n = (kpos_n // CHUNK) <= (qpos_n // CHUNK)
    for h in range(A_HEADS):
        cs = slice(h * A_V_DIM, (h + 1) * A_V_DIM)
        q = q_ref[0, :, cs] * (A_QK_DIM ** -0.5)
        zero = jnp.zeros_like(q)
        qp = jnp.concatenate([jnp.where(lane < A_QK_DIM, q, zero),
                              jnp.where(lane >= A_QK_DIM, q, zero)], axis=0)
        kp = kp_ref[0, :, h, :].astype(BF16)
        vp = vp_ref[0, :, h, :].astype(BF16)
        kn = kn_ref[0, :, cs]
        vn = vn_ref[0, :, cs]
        bp = bp_ref[h]
        bn = bn_ref[h]
        sp = jnp.where(ok_p, _dot_nt(qp, kp) + jnp.concatenate([bp, bp], axis=0), NEG_INF)
        sn = jnp.where(ok_n, _dot_nt(qp, kn) + jnp.concatenate([bn, bn], axis=0), NEG_INF)
        m = jnp.maximum(jnp.max(sp, axis=-1, keepdims=True), jnp.max(sn, axis=-1, keepdims=True))
        pp = jnp.exp(sp - m)
        pn = jnp.exp(sn - m)
        l = jnp.sum(pp, axis=-1, keepdims=True) + jnp.sum(pn, axis=-1, keepdims=True)
        out = (_dot(pp.astype(BF16), vp) + _dot(pn.astype(BF16), vn)) / l
        o = out[:t] - lam_ref[0, 0] * out[t:]
        o_ref[:, cs] = (_head_rms(o) * subln_ref[...] * out_scale).astype(BF16)


def _diff_attn_cached(lam, z3, past_k, past_v, bias_past, bias_new, subln, out_scale):
    bsz, t, _ = z3.shape
    past = past_k.shape[1]
    width = A_HEADS * A_V_DIM
    cache_spec = pl.BlockSpec((1, past, A_HEADS, A_V_DIM), lambda b: (b, 0, 0, 0))
    return pl.pallas_call(
        functools.partial(_diff_attn_cached_kernel, t=t, past=past, out_scale=out_scale),
        grid=(bsz,),
        in_specs=[pl.BlockSpec(memory_space=pltpu.SMEM),
                  pl.BlockSpec((1, t, width), lambda b: (b, 0, 0)),
                  pl.BlockSpec((1, t, width), lambda b: (b, 0, 1)),
                  pl.BlockSpec((1, t, width), lambda b: (b, 0, 2)),
                  cache_spec, cache_spec,
                  pl.BlockSpec((A_HEADS, t, past), lambda b: (0, 0, 0)),
                  pl.BlockSpec((A_HEADS, t, t), lambda b: (0, 0, 0)),
                  pl.BlockSpec((1, A_V_DIM), lambda b: (0, 0))],
        out_specs=pl.BlockSpec((t, width), lambda b: (b, 0)),
        out_shape=jax.ShapeDtypeStruct((bsz * t, width), BF16),
        compiler_params=_params(1),
    )(lam, z3, z3, z3, past_k, past_v, bias_past, bias_new, subln.reshape(1, A_V_DIM))


def _retention_kernel(q_ref, k_ref, v_ref, gt_ref, cos_ref, sin_ref, dec_ref, qd_ref, kd_ref,
                      cd_ref, s0_ref, o_ref, s_out_ref, s_sc):
    t = pl.program_id(1)

    @pl.when(t == 0)
    def _():
        s_sc[...] = s0_ref[0]

    d = B_QK_DIM
    half = d // 2
    cos = cos_ref[...]
    sin = sin_ref[...]
    for h in range(B_HEADS):
        cs = slice(h * d, (h + 1) * d)
        q = q_ref[0, :, cs].astype(F32)
        k = k_ref[0, :, cs].astype(F32)
        qr = (q * cos + pltpu.roll(q, half, 1) * sin) * (d ** -0.5)
        kr = k * cos + pltpu.roll(k, half, 1) * sin
        v = v_ref[0, :, cs]
        att = _dot_nt(qr.astype(BF16), kr.astype(BF16)) * dec_ref[h]
        s = s_sc[h]
        o = _dot(att.astype(BF16), v) + _dot((qr * qd_ref[h]).astype(BF16), s.astype(BF16))
        s_sc[h] = s * cd_ref[h] + _dot_tn((kr * kd_ref[h]).astype(BF16), v)
        gt = gt_ref[0, :, cs].astype(F32)
        o_ref[:, cs] = (_head_rms(o) * _silu(gt)).astype(BF16)

    @pl.when(t == pl.num_programs(1) - 1)
    def _():
        s_out_ref[0] = s_sc[...]


def _retention(z3, cos, sin, consts, s0, c):
    bsz, t, _ = z3.shape
    nt = t // c
    dec, qd, kd, cd = consts
    d = B_QK_DIM
    width = B_HEADS * d
    base = 3
    full3 = lambda b, i: (0, 0, 0)
    return pl.pallas_call(
        _retention_kernel,
        grid=(bsz, nt),
        in_specs=[pl.BlockSpec((1, c, width), lambda b, i: (b, i, base)),
                  pl.BlockSpec((1, c, width), lambda b, i: (b, i, base + 1)),
                  pl.BlockSpec((1, c, width), lambda b, i: (b, i, base + 2)),
                  pl.BlockSpec((1, c, width), lambda b, i: (b, i, base + 3)),
                  pl.BlockSpec((c, d), lambda b, i: (i, 0)),
                  pl.BlockSpec((c, d), lambda b, i: (i, 0)),
                  pl.BlockSpec((B_HEADS, c, c), full3),
                  pl.BlockSpec((B_HEADS, c, d), full3),
                  pl.BlockSpec((B_HEADS, c, d), full3),
                  pl.BlockSpec((B_HEADS, 1, d), full3),
                  pl.BlockSpec((1, B_HEADS, d, d), lambda b, i: (b, 0, 0, 0))],
        out_specs=[pl.BlockSpec((c, width), lambda b, i: (b * nt + i, 0)),
                   pl.BlockSpec((1, B_HEADS, d, d), lambda b, i: (b, 0, 0, 0))],
        out_shape=[jax.ShapeDtypeStruct((bsz * t, width), BF16),
                   jax.ShapeDtypeStruct((bsz, B_HEADS, d, d), F32)],
        scratch_shapes=[pltpu.VMEM((B_HEADS, d, d), F32)],
        compiler_params=_params(2),
    )(z3, z3, z3, z3, cos, sin, dec, qd, kd, cd, s0)


def _retention_consts(c):
    log_g = jnp.log1p(-jnp.exp2(-5.0 - jnp.arange(B_HEADS, dtype=F32)))
    idx = jnp.arange(c, dtype=F32)
    dist = idx[:, None] - idx[None, :]
    dec = jnp.where(dist >= 0, jnp.exp(jnp.maximum(dist, 0.0)[None] * log_g[:, None, None]), 0.0)
    qd = jnp.exp((idx[None, :] + 1.0) * log_g[:, None])
    kd = jnp.exp((c - 1.0 - idx)[None, :] * log_g[:, None])
    cd = jnp.exp(c * log_g)
    bc = lambda a: jnp.broadcast_to(a[..., None], a.shape + (B_QK_DIM,))
    return dec, bc(qd), bc(kd), bc(cd[:, None])


def _rotary_tables(pos):
    half = B_QK_DIM // 2
    inv = ROPE_BASE ** (-jnp.arange(half, dtype=F32) / half)
    ang = pos.astype(F32)[:, None] * inv[None, :]
    cos, sin = jnp.cos(ang), jnp.sin(ang)
    return jnp.concatenate([cos, cos], axis=-1), jnp.concatenate([-sin, sin], axis=-1)


def _gla_levels(c):
    return [c >> (l + 1) for l in range(int(math.log2(c)))]


def _gla_consts(c):
    levels = _gla_levels(c)
    rows = np.arange(c)
    mats = []
    for s in levels:
        ref = (rows // (2 * s)) * 2 * s + s - 1
        a = np.zeros((c, c), np.float32)
        for i in range(c):
            if i & s:
                a[i, ref[i] + 1:i + 1] = 1.0
            else:
                a[i, i + 1:ref[i] + 1] = 1.0
        mats.append(a)
    mats.append(np.tril(np.ones((c, c), np.float32)))
    mats.append(np.triu(np.ones((c, c), np.float32), 1))
    lv = np.full((c, c), -1, np.int32)
    for i in range(c):
        lv[i, i] = len(levels)
        for j in range(i):
            lv[i, j] = levels.index(1 << int(math.floor(math.log2(i ^ j))))
    return jnp.asarray(np.concatenate(mats, axis=0), BF16), jnp.asarray(lv)


def _gla_kernel(q_ref, k_ref, v_ref, r_ref, g_ref, s0_ref, a_ref, lv_ref, nw_ref,
                o_ref, s_out_ref, st_sc, *, c):
    t = pl.program_id(1)
    dk, dv = C_QK_DIM, C_V_DIM

    @pl.when(t == 0)
    def _():
        for h in range(C_HEADS):
            st_sc[h] = s0_ref[0, h].T

    levels = _gla_levels(c)
    n_lv = len(levels)
    half_blocks = (c // 2) % LANES == 0
    lv = lv_ref[...]
    row = lax.broadcasted_iota(jnp.int32, (c, dk), 0)
    g = g_ref[0]
    xs = _dot(a_ref[...], g.astype(BF16))
    for h in range(C_HEADS):
        ks = slice(h * dk, (h + 1) * dk)
        vs = slice(h * dv, (h + 1) * dv)
        x = xs[:, ks]
        q = q_ref[0, :, ks].astype(F32) * (dk ** -0.5)
        k = k_ref[0, :, ks].astype(F32)
        v = v_ref[0, :, vs]
        if half_blocks:
            hc = c // 2
            lvh = lv[:hc, :hc]
            diag = [jnp.zeros((hc, hc), F32), jnp.zeros((hc, hc), F32)]
            qb, kb = q.astype(BF16), k.astype(BF16)
            pending = (n_lv, [_dot_nt(qb[:hc], kb[:hc]), _dot_nt(qb[hc:], kb[hc:])])
        else:
            att = jnp.zeros((c, c), F32)
            pending = (n_lv, _dot_nt(q.astype(BF16), k.astype(BF16)))
        for l, s in enumerate(levels):
            e = jnp.exp(x[l * c:(l + 1) * c])
            up = (row & s) != 0
            mix = jnp.where(up, q, k) * e
            qt = jnp.where(up, mix, 0.0).astype(BF16)
            kt = jnp.where(up, 0.0, mix).astype(BF16)
            if not half_blocks:
                prod = _dot_nt(qt, kt)
                att = jnp.where(lv == pending[0], pending[1], att)
            elif l == 0:
                lower_left = _dot_nt(qt[hc:], kt[:hc])
                continue
            else:
                prod = [_dot_nt(qt[:hc], kt[:hc]), _dot_nt(qt[hc:], kt[hc:])]
                diag = [jnp.where(lvh == pending[0], pending[1][u], diag[u]) for u in range(2)]
            pending = (l, prod)
        if half_blocks:
            diag = [jnp.where(lvh == pending[0], pending[1][u], diag[u]) for u in range(2)]
            att = jnp.concatenate(
                [jnp.concatenate([diag[0], jnp.zeros((hc, hc), F32)], axis=1),
                 jnp.concatenate([lower_left, diag[1]], axis=1)], axis=0)
        else:
            att = jnp.where(lv == pending[0], pending[1], att)
        b = x[n_lv * c:(n_lv + 1) * c]
        rem = x[(n_lv + 1) * c:(n_lv + 2) * c]
        st = st_sc[h]
        o = _dot(att.astype(BF16), v) + _dot_nt((q * jnp.exp(b)).astype(BF16), st.astype(BF16))
        kd = (k * jnp.exp(rem)).astype(BF16)
        st_sc[h] = st * jnp.exp(b[c - 1:c, :]) + _dot_tn(v, kd)
        r = r_ref[0, :, vs].astype(F32)
        o_ref[:, vs] = (_head_rms(o) * nw_ref[...] * _silu(r)).astype(BF16)

    @pl.when(t == pl.num_programs(1) - 1)
    def _():
        for h in range(C_HEADS):
            s_out_ref[0, h] = st_sc[h].T


def _gla(z3, log_a3, s0, a_stack, lv, norm_w, c):
    bsz, t, _ = z3.shape
    nt = t // c
    dk, dv = C_QK_DIM, C_V_DIM
    qk_w, v_w = C_HEADS * dk, C_HEADS * dv
    const = lambda b, i: (0, 0)
    return pl.pallas_call(
        functools.partial(_gla_kernel, c=c),
        grid=(bsz, nt),
        in_specs=[pl.BlockSpec((1, c, qk_w), lambda b, i: (b, i, 0)),
                  pl.BlockSpec((1, c, qk_w), lambda b, i: (b, i, 1)),
                  pl.BlockSpec((1, c, v_w), lambda b, i: (b, i, 1)),
                  pl.BlockSpec((1, c, v_w), lambda b, i: (b, i, 2)),
                  pl.BlockSpec((1, c, qk_w), lambda b, i: (b, i, 0)),
                  pl.BlockSpec((1, C_HEADS, dk, dv), lambda b, i: (b, 0, 0, 0)),
                  pl.BlockSpec(a_stack.shape, const),
                  pl.BlockSpec((c, c), const),
                  pl.BlockSpec((1, dv), const)],
        out_specs=[pl.BlockSpec((c, v_w), lambda b, i: (b * nt + i, 0)),
                   pl.BlockSpec((1, C_HEADS, dk, dv), lambda b, i: (b, 0, 0, 0))],
        out_shape=[jax.ShapeDtypeStruct((bsz * t, v_w), BF16),
                   jax.ShapeDtypeStruct((bsz, C_HEADS, dk, dv), F32)],
        scratch_shapes=[pltpu.VMEM((C_HEADS, dv, dk), F32)],
        compiler_params=_params(2),
    )(z3, z3, z3, z3, log_a3, s0, a_stack, lv, norm_w.reshape(1, dv))


def _cross_kernel(*refs, n_mix):
    x_ref = refs[0]
    a_refs = refs[1:1 + n_mix]
    w_refs = refs[1 + n_mix:1 + 2 * n_mix]
    g_ref, wq_ref, mk_ref, mv_ref, wo_ref, o_ref = refs[1 + 2 * n_mix:]
    x = x_ref[...]
    for a_ref, w_ref in zip(a_refs, w_refs):
        x = x + _dot(a_ref[...], w_ref[...])
    h = _rms(x, g_ref[...]).astype(BF16)
    q = (_dot(h, wq_ref[0]) * (M_HEAD_DIM ** -0.5)).astype(BF16)

    def head(ref, hd):
        if len(ref.shape) == 5:
            return ref[0, 0, :, hd, :].astype(BF16)
        return ref[0, 0, :, hd * M_HEAD_DIM:(hd + 1) * M_HEAD_DIM].astype(BF16)

    outs = []
    for hd in range(M_HEADS):
        cs = slice(hd * M_HEAD_DIM, (hd + 1) * M_HEAD_DIM)
        s = _dot_nt(q[:, cs], head(mk_ref, hd))
        p = jnp.exp(s - jnp.max(s, axis=-1, keepdims=True))
        l = jnp.sum(p, axis=-1, keepdims=True)
        outs.append((_dot(p.astype(BF16), head(mv_ref, hd)) / l).astype(BF16))
    o_ref[...] = x + _dot(jnp.concatenate(outs, axis=1), wo_ref[0])


def _cross(x2d, mix_list, w_mix_list, gamma, wq, mk, mv, wo, layer, t, tm):
    n = x2d.shape[0]
    nt = t // tm
    width = M_HEADS * M_HEAD_DIM
    rows = lambda b, i: (b * nt + i, 0)
    const = lambda b, i: (0, 0)
    of_layer = lambda b, i: (layer, 0, 0)
    mem_spec = pl.BlockSpec((1, 1) + mk.shape[2:], lambda b, i: (layer, b) + (0,) * (mk.ndim - 2))
    in_specs = [pl.BlockSpec((tm, D_MODEL), rows)]
    in_specs += [pl.BlockSpec((tm, a.shape[1]), rows) for a in mix_list]
    in_specs += [pl.BlockSpec(w.shape, const) for w in w_mix_list]
    in_specs += [pl.BlockSpec((1, D_MODEL), const),
                 pl.BlockSpec((1, D_MODEL, width), of_layer),
                 mem_spec, mem_spec,
                 pl.BlockSpec((1, width, D_MODEL), of_layer)]
    return pl.pallas_call(
        functools.partial(_cross_kernel, n_mix=len(mix_list)),
        grid=(n // t, nt),
        in_specs=in_specs,
        out_specs=pl.BlockSpec((tm, D_MODEL), rows),
        out_shape=jax.ShapeDtypeStruct((n, D_MODEL), F32),
        compiler_params=_params(2),
    )(x2d, *mix_list, *w_mix_list, gamma.reshape(1, D_MODEL), wq, mk, mv, wo)


def _ffn_kernel(x_ref, g_ref, wg_ref, wu_ref, cw_ref, cb_ref, prev_ref, wd_ref, *rest,
                tm, final_norm):
    if final_norm:
        gf_ref, o_ref, tail_ref, carry = rest
    else:
        o_ref, tail_ref, carry = rest
    t = pl.program_id(1)
    lo = SUBLANES - 2

    @pl.when(t == 0)
    def _():
        carry[lo:SUBLANES, :] = prev_ref[0]

    x = x_ref[...]
    h = _rms(x, g_ref[...]).astype(BF16)
    row = lax.broadcasted_iota(jnp.int32, (SUBLANES, FF_CHUNK), 0)
    acc = x
    n_chunks = D_FF // FF_CHUNK
    chunk = lambda c: slice(c * FF_CHUNK, (c + 1) * FF_CHUNK)
    nxt = (_dot(h, wg_ref[0, :, chunk(0)]), _dot(h, wu_ref[0, :, chunk(0)]))
    pending = []
    for c in range(n_chunks):
        cs = chunk(c)
        gate, up = nxt
        if c + 1 < n_chunks:
            nxt = (_dot(h, wg_ref[0, :, chunk(c + 1)]), _dot(h, wu_ref[0, :, chunk(c + 1)]))
        p1 = carry[SUBLANES - 1:SUBLANES, cs]
        p2 = carry[lo:lo + 1, cs]
        r1 = pltpu.roll(gate, 1, 0)
        r2 = pltpu.roll(gate, 2, 0)
        h1 = jnp.where(row == 0, p1, r1[:SUBLANES])
        h2 = jnp.where(row == 0, p2, jnp.where(row == 1, p1, r2[:SUBLANES]))
        g1 = jnp.concatenate([h1, r1[SUBLANES:]], axis=0)
        g2 = jnp.concatenate([h2, r2[SUBLANES:]], axis=0)
        carry[:, cs] = gate[tm - SUBLANES:tm]
        conv = cb_ref[:, cs] + cw_ref[0:1, cs] * g2
        conv = conv + cw_ref[1:2, cs] * g1
        conv = conv + cw_ref[2:3, cs] * gate
        pending.append((_gelu_tanh(conv) * up).astype(BF16))
        if len(pending) == DOWN_GROUP or c == n_chunks - 1:
            lo_col = (c + 1 - len(pending)) * FF_CHUNK
            acc = acc + _dot(jnp.concatenate(pending, axis=1),
                             wd_ref[0, lo_col:(c + 1) * FF_CHUNK, :])
            pending = []
    if final_norm:
        o_ref[...] = _rms(acc, gf_ref[...])
    else:
        o_ref[...] = acc

    @pl.when(t == pl.num_programs(1) - 1)
    def _():
        tail_ref[0] = carry[lo:SUBLANES, :]


def _ffn(x2d, gamma, wg, wu, conv_w, conv_b, prev, wd, layer, t, tm, final_gamma=None):
    n = x2d.shape[0]
    nt = t // tm
    bsz = n // t
    const = lambda b, i: (0, 0)
    of_layer = lambda b, i: (layer, 0, 0)
    resident = dict(pipeline_mode=pl.Buffered(1))
    in_specs = [pl.BlockSpec((tm, D_MODEL), lambda b, i: (b * nt + i, 0)),
                pl.BlockSpec((1, D_MODEL), const),
                pl.BlockSpec((1, D_MODEL, D_FF), of_layer, **resident),
                pl.BlockSpec((1, D_MODEL, D_FF), of_layer, **resident),
                pl.BlockSpec((3, D_FF), const),
                pl.BlockSpec((1, D_FF), const),
                pl.BlockSpec((1, 2, D_FF), lambda b, i: (b, 0, 0)),
                pl.BlockSpec((1, D_FF, D_MODEL), of_layer, **resident)]
    args = [x2d, gamma.reshape(1, D_MODEL), wg, wu, conv_w, conv_b.reshape(1, D_FF), prev, wd]
    if final_gamma is not None:
        in_specs.append(pl.BlockSpec((1, D_MODEL), const))
        args.append(final_gamma.reshape(1, D_MODEL))
    return pl.pallas_call(
        functools.partial(_ffn_kernel, tm=tm, final_norm=final_gamma is not None),
        grid=(bsz, nt),
        in_specs=in_specs,
        out_specs=[pl.BlockSpec((tm, D_MODEL), lambda b, i: (b * nt + i, 0)),
                   pl.BlockSpec((1, 2, D_FF), lambda b, i: (b, 0, 0))],
        out_shape=[jax.ShapeDtypeStruct((n, D_MODEL), F32),
                   jax.ShapeDtypeStruct((bsz, 2, D_FF), F32)],
        scratch_shapes=[pltpu.VMEM((SUBLANES, D_FF), F32)],
        compiler_params=_params(2),
    )(*args)


def _rel_bucket(rel):
    nb = REL_BUCKETS // 2
    max_exact = nb // 2
    n = jnp.abs(rel)
    nf = jnp.maximum(n, 1).astype(F32)
    large = max_exact + (jnp.log(nf / max_exact) / math.log(REL_MAX_DIST / max_exact)
                         * (nb - max_exact)).astype(jnp.int32)
    large = jnp.minimum(large, nb - 1)
    return jnp.where(rel > 0, nb, 0) + jnp.where(n < max_exact, n, large)


def _toeplitz_kernel(w_ref, o_ref, *, rows, cols):
    w = w_ref[0]
    x = jnp.broadcast_to(w, (rows, w.shape[1]))
    o_ref[0] = pltpu.roll(x, 0, 1, stride=1, stride_axis=0)[:, :cols]


def _toeplitz(w, rows, cols):
    groups, period = w.shape
    return pl.pallas_call(
        functools.partial(_toeplitz_kernel, rows=rows, cols=cols),
        grid=(groups,),
        in_specs=[pl.BlockSpec((1, 1, period), lambda g: (g, 0, 0))],
        out_specs=pl.BlockSpec((1, rows, cols), lambda g: (g, 0, 0)),
        out_shape=jax.ShapeDtypeStruct((groups, rows, cols), F32),
        compiler_params=_params(1),
    )(w.reshape(groups, 1, period))


def _bias_tiles(rel_bias, tq, nd):
    period = 2 * tq
    n = np.arange(period)
    rel = np.stack([np.where(n < tq, (d - (nd - 1)) * tq - n, (d - (nd - 1)) * tq + period - n)
                    for d in range(nd)]).astype(np.int32)
    w = jnp.transpose(rel_bias[_rel_bucket(jnp.asarray(rel))], (2, 0, 1)).astype(F32)
    return _toeplitz(w.reshape(A_HEADS * nd, period), tq, tq).reshape(A_HEADS, nd, tq, tq)


def _bias_rows(rel_bias, t, past):
    n_keys = past + t
    period = -(-(n_keys + t) // LANES) * LANES
    m = np.arange(period)
    rel = np.where(m < n_keys, m - past, m - period - past).astype(np.int32)
    w = jnp.transpose(rel_bias[_rel_bucket(jnp.asarray(rel))], (1, 0)).astype(F32)
    bias = _toeplitz(w, t, n_keys)
    return bias[:, :, :past], bias[:, :, past:]


def _forward(x, past_k, past_v, ret_state, gla_state, conv_prev, mem_k, mem_v, p, w):
    bsz, t, _ = x.shape
    n = bsz * t
    past = 0 if past_k is None else past_k.shape[2]
    tm = min(512, n)
    tseq = min(512, t)
    tffn = min(256, t)
    tq = min(256, t)
    c_ret = min(256, t)
    c_gla = min(256, t)
    x2 = x.reshape(n, D_MODEL)
    pos = past + jnp.arange(t, dtype=jnp.int32)
    new_conv = []

    z, ak, av = _in_even(x2, p["ln_mix"][0], w["w_in_even"], tm)
    z3 = z.reshape(bsz, t, EVEN_IN)
    lam_init = 0.8 - 0.6 * math.exp(-0.3 * 0)
    lam = (jnp.exp(jnp.sum(p["diff_lq1"][0].astype(F32) * p["diff_lk1"][0].astype(F32)))
           - jnp.exp(jnp.sum(p["diff_lq2"][0].astype(F32) * p["diff_lk2"][0].astype(F32)))
           + lam_init).reshape(1, 1).astype(F32)
    if past == 0:
        o_a = _diff_attn(lam, z3, _bias_tiles(p["rel_bias"], tq, t // tq), p["diff_subln"][0],
                         tq, A_HEADS, 1.0 - lam_init)
    else:
        bias_past, bias_new = _bias_rows(p["rel_bias"], t, past)
        o_a = _diff_attn_cached(lam, z3, past_k[0], past_v[0], bias_past, bias_new,
                                p["diff_subln"][0], 1.0 - lam_init)
    cos, sin = _rotary_tables(pos)
    o_b, s_ret = _retention(z3, cos, sin, _retention_consts(c_ret), ret_state[0], c_ret)
    x2 = _cross(x2, [o_a, o_b], [w["w_out_even_a"], w["w_out_even_b"]], p["ln_cross"][0],
                w["w_cq"], mem_k, mem_v, w["w_co"], 0, t, tseq)
    x2, tail = _ffn(x2, p["ln_ffn"][0], w["w_ffn_gate"], w["w_ffn_up"], p["ffn_conv_w"][0],
                    p["ffn_conv_b"][0], conv_prev[0], w["w_ffn_down"], 0, t, tffn)
    new_conv.append(tail)

    zc, log_a = _in_odd(x2, p["ln_mix"][1], w["w_in_odd"], w["w_gate_lr"], p["b_gate"][0], tm)
    a_stack, lv = _gla_consts(c_gla)
    o_c, s_gla = _gla(zc.reshape(bsz, t, ODD_Z), log_a.reshape(bsz, t, C_HEADS * C_QK_DIM),
                      gla_state[0], a_stack, lv, p["gla_norm"][0], c_gla)
    x2 = _cross(x2, [o_c], [w["w_out_odd"]], p["ln_cross"][1],
                w["w_cq"], mem_k, mem_v, w["w_co"], 1, t, tseq)
    y, tail = _ffn(x2, p["ln_ffn"][1], w["w_ffn_gate"], w["w_ffn_up"], p["ffn_conv_w"][1],
                   p["ffn_conv_b"][1], conv_prev[1], w["w_ffn_down"], 1, t, tffn,
                   final_gamma=p["ln_final"])
    new_conv.append(tail)

    return (y.reshape(bsz, t, D_MODEL),
            ak.reshape(1, bsz, t, A_HEADS, A_V_DIM), av.reshape(1, bsz, t, A_HEADS, A_V_DIM),
            s_ret[None], s_gla[None], jnp.stack(new_conv))


def kernel(x_prompt, x_sample, cache_diff_k, cache_diff_v, state_retention, state_gla, cache_ffn_conv, cache_mem_k, cache_mem_v, mem_prompt, ln_mix, ln_cross, ln_ffn, ln_mem, ln_final, w_in_even, w_out_even, diff_lq1, diff_lk1, diff_lq2, diff_lk2, diff_subln, rel_bias, w_in_odd, w_gate_lr, b_gate, gla_norm, w_out_odd, w_cq, w_ck, w_cv, w_co, w_ffn_gate, w_ffn_up, ffn_conv_w, ffn_conv_b, w_ffn_down):
    p = dict(ln_mix=ln_mix, ln_cross=ln_cross, ln_ffn=ln_ffn, ln_final=ln_final,
             diff_lq1=diff_lq1, diff_lk1=diff_lk1, diff_lq2=diff_lq2, diff_lk2=diff_lk2,
             diff_subln=diff_subln, rel_bias=rel_bias, b_gate=b_gate, gla_norm=gla_norm,
             ffn_conv_w=ffn_conv_w, ffn_conv_b=ffn_conv_b)
    a_v = A_HEADS * A_V_DIM
    w = dict(
        w_in_even=w_in_even[0].astype(BF16),
        w_out_even_a=w_out_even[0, :a_v].astype(BF16),
        w_out_even_b=w_out_even[0, a_v:].astype(BF16),
        w_in_odd=jnp.pad(w_in_odd[0], ((0, 0), (0, ODD_IN_PAD - w_in_odd.shape[2]))).astype(BF16),
        w_gate_lr=jnp.pad(w_gate_lr[0], ((0, LANES - C_GATE_RANK), (0, 0))).astype(BF16),
        w_out_odd=w_out_odd[0].astype(BF16),
        w_cq=w_cq.astype(BF16), w_co=w_co.astype(BF16),
        w_ffn_gate=w_ffn_gate.astype(BF16), w_ffn_up=w_ffn_up.astype(BF16),
        w_ffn_down=w_ffn_down.astype(BF16))

    bp, mem_len, _ = mem_prompt.shape
    m_width = M_HEADS * M_HEAD_DIM
    depth = ln_mem.shape[0]
    mem_k_p, mem_v_p, mk_b, mv_b = _mem_kv(mem_prompt, ln_mem, w_ck.astype(BF16),
                                           w_cv.astype(BF16), seqs=2)

    dt = x_prompt.dtype
    zero_ret = jnp.zeros((1, bp, B_HEADS, B_QK_DIM, B_QK_DIM), dt)
    zero_gla = jnp.zeros((1, bp, C_HEADS, C_QK_DIM, C_V_DIM), dt)
    zero_conv = jnp.zeros((depth, bp, 2, D_FF), dt)
    y_p, dk_p, dv_p, ret_p, gla_p, conv_p = _forward(
        x_prompt, None, None, zero_ret, zero_gla, zero_conv,
        mk_b.reshape(depth, bp, mem_len, m_width), mv_b.reshape(depth, bp, mem_len, m_width), p, w)
    y_s, dk_s, dv_s, ret_s, gla_s, conv_s = _forward(
        x_sample, cache_diff_k, cache_diff_v, state_retention, state_gla, cache_ffn_conv,
        cache_mem_k, cache_mem_v, p, w)
    return (y_p, y_s, dk_p, dv_p, ret_p, gla_p, conv_p, mem_k_p, mem_v_p,
            dk_s, dv_s, ret_s, gla_s, conv_s)
```

```python
import functools
import math

import numpy as np
import jax
import jax.numpy as jnp
from jax import lax
from jax.experimental import pallas as pl
from jax.experimental.pallas import tpu as pltpu

F32 = jnp.float32
BF16 = jnp.bfloat16

D_MODEL = 1024
CHUNK = 64
A_HEADS = 4
A_QK_DIM = 64
A_V_DIM = 128
B_HEADS = 4
B_QK_DIM = 128
C_HEADS = 4
C_QK_DIM = 128
C_V_DIM = 256
C_GATE_RANK = 16
C_GATE_TAU = 16.0
M_HEADS = 4
M_HEAD_DIM = 256
REL_BUCKETS = 32
REL_MAX_DIST = 128
D_FF = 2816
ROPE_BASE = 10000.0
EPS = 1e-6
NEG_INF = -1e30

EVEN_IN = 3584
ODD_Z = 3072
LANES = 128
SUBLANES = 8
ODD_IN_PAD = ODD_Z + LANES
FF_CHUNK = 256
DOWN_GROUP = 6
FFN_SUB_ROWS = 256
CROSS_SUB_ROWS = 512
PROJ_SUB_ROWS = 512
ONES_ROWS = 16
LOG2_E = math.log2(math.e)
V7X_VMEM_LIMIT_BYTES = 56 * 1024 * 1024


def _params(n_axes):
    return pltpu.CompilerParams(dimension_semantics=("arbitrary",) * n_axes,
                                vmem_limit_bytes=V7X_VMEM_LIMIT_BYTES)


def _dot(a, b):
    return jnp.dot(a, b, preferred_element_type=F32)


def _dot_nt(a, b):
    return lax.dot_general(a, b, (((1,), (1,)), ((), ())), preferred_element_type=F32)


def _dot_tn(a, b):
    return lax.dot_general(a, b, (((0,), (0,)), ((), ())), preferred_element_type=F32)


def _rms(x, g):
    return x * lax.rsqrt(jnp.mean(x * x, axis=-1, keepdims=True) + EPS) * g


def _head_rms(x):
    return x * lax.rsqrt(jnp.mean(x * x, axis=-1, keepdims=True) + EPS)


def _silu(x):
    return x * (1.0 / (1.0 + jnp.exp(-x)))


def _gelu_tanh(x):
    c0 = math.sqrt(2.0 / math.pi)
    return x * (0.5 + 0.5 * jnp.tanh(x * (c0 + (c0 * 0.044715) * (x * x))))


def _log_sigmoid(x):
    return jnp.minimum(x, 0.0) - jnp.log(1.0 + jnp.exp(-jnp.abs(x)))


def _store_head_major(o_ref, val, tm, n_heads, head_dim):
    tiles = head_dim // LANES
    group = n_heads * tiles
    for hd in range(n_heads):
        for part in range(tiles):
            col = hd * head_dim + part * LANES
            o_ref[pl.ds(part * n_heads + hd, tm, stride=group), :] = val[:, col:col + LANES]


def _in_even_kernel(x_ref, g_ref, w_ref, z_ref, k_ref, v_ref, *, n_sub):
    width = A_HEADS * A_V_DIM
    tr = x_ref.shape[0] // n_sub
    h_next = _rms(x_ref[0:tr, :], g_ref[...]).astype(BF16)
    for sub in range(n_sub):
        rows = slice(sub * tr, (sub + 1) * tr)
        h = h_next
        for c in range(EVEN_IN // width):
            zc = _dot(h, w_ref[:, c * width:(c + 1) * width])
            z_ref[rows, c * width:(c + 1) * width] = zc.astype(BF16)
            head_rows = slice(sub * tr * A_HEADS, (sub + 1) * tr * A_HEADS)
            if c == 0 and sub + 1 < n_sub:
                h_next = _rms(x_ref[(sub + 1) * tr:(sub + 2) * tr, :], g_ref[...]).astype(BF16)
            if c == 1:
                _store_head_major(k_ref.at[head_rows, :], zc, tr, A_HEADS, A_V_DIM)
            if c == 2:
                _store_head_major(v_ref.at[head_rows, :], zc, tr, A_HEADS, A_V_DIM)


def _in_even(x2d, gamma, w, tm):
    n = x2d.shape[0]
    return pl.pallas_call(
        functools.partial(_in_even_kernel, n_sub=max(1, tm // PROJ_SUB_ROWS)),
        grid=(n // tm,),
        in_specs=[pl.BlockSpec((tm, D_MODEL), lambda i: (i, 0)),
                  pl.BlockSpec((1, D_MODEL), lambda i: (0, 0)),
                  pl.BlockSpec((D_MODEL, EVEN_IN), lambda i: (0, 0), pipeline_mode=pl.Buffered(1))],
        out_specs=[pl.BlockSpec((tm, EVEN_IN), lambda i: (i, 0)),
                   pl.BlockSpec((tm * A_HEADS, A_V_DIM), lambda i: (i, 0)),
                   pl.BlockSpec((tm * A_HEADS, A_V_DIM), lambda i: (i, 0))],
        out_shape=[jax.ShapeDtypeStruct((n, EVEN_IN), BF16),
                   jax.ShapeDtypeStruct((n * A_HEADS, A_V_DIM), F32),
                   jax.ShapeDtypeStruct((n * A_HEADS, A_V_DIM), F32)],
        compiler_params=_params(1),
    )(x2d, gamma.reshape(1, D_MODEL), w)


def _in_odd_kernel(x_ref, g_ref, w_ref, wlr_ref, bg_ref, z_ref, la_ref, *, n_sub):
    width = 512
    tr = x_ref.shape[0] // n_sub

    def norm_and_gates(sub):
        rows = slice(sub * tr, (sub + 1) * tr)
        h = _rms(x_ref[rows, :], g_ref[...]).astype(BF16)
        ca = _dot(h, w_ref[:, ODD_Z:ODD_IN_PAD]).astype(BF16)
        pre = _dot(ca, wlr_ref[...]) + bg_ref[...]
        la_ref[rows, :] = _log_sigmoid(pre) / C_GATE_TAU
        return h

    h_next = norm_and_gates(0)
    for sub in range(n_sub):
        rows = slice(sub * tr, (sub + 1) * tr)
        h = h_next
        for c in range(ODD_Z // width):
            z_ref[rows, c * width:(c + 1) * width] = _dot(
                h, w_ref[:, c * width:(c + 1) * width]).astype(BF16)
            if c == 0 and sub + 1 < n_sub:
                h_next = norm_and_gates(sub + 1)


def _in_odd(x2d, gamma, w_pad, wlr_pad, b_gate, tm):
    n = x2d.shape[0]
    qk = C_HEADS * C_QK_DIM
    return pl.pallas_call(
        functools.partial(_in_odd_kernel, n_sub=max(1, tm // PROJ_SUB_ROWS)),
        grid=(n // tm,),
        in_specs=[pl.BlockSpec((tm, D_MODEL), lambda i: (i, 0)),
                  pl.BlockSpec((1, D_MODEL), lambda i: (0, 0)),
                  pl.BlockSpec((D_MODEL, ODD_IN_PAD), lambda i: (0, 0),
                               pipeline_mode=pl.Buffered(1)),
                  pl.BlockSpec((LANES, qk), lambda i: (0, 0)),
                  pl.BlockSpec((1, qk), lambda i: (0, 0))],
        out_specs=[pl.BlockSpec((tm, ODD_Z), lambda i: (i, 0)),
                   pl.BlockSpec((tm, qk), lambda i: (i, 0))],
        out_shape=[jax.ShapeDtypeStruct((n, ODD_Z), BF16),
                   jax.ShapeDtypeStruct((n, qk), F32)],
        compiler_params=_params(1),
    )(x2d, gamma.reshape(1, D_MODEL), w_pad, wlr_pad, b_gate.reshape(1, qk))


def _mem_kv_kernel(x_ref, g_ref, wk_ref, wv_ref, k5_ref, v5_ref, kb_ref, vb_ref, *, seqs, mem_len):
    h = _rms(x_ref[...], g_ref[0]).astype(BF16)
    for w_ref, o5_ref, ob_ref in ((wk_ref, k5_ref, kb_ref), (wv_ref, v5_ref, vb_ref)):
        y = _dot(h, w_ref[0])
        ob_ref[0] = y.astype(BF16)
        for s in range(seqs):
            for hd in range(M_HEADS):
                o5_ref[0, s, :, hd, :] = y[s * mem_len:(s + 1) * mem_len,
                                           hd * M_HEAD_DIM:(hd + 1) * M_HEAD_DIM]


def _mem_kv(mem, ln_mem, wk, wv, seqs):
    bsz, mem_len, _ = mem.shape
    depth = ln_mem.shape[0]
    width = M_HEADS * M_HEAD_DIM
    n = bsz * mem_len
    tm = seqs * mem_len
    out5 = jax.ShapeDtypeStruct((depth, bsz, mem_len, M_HEADS, M_HEAD_DIM), F32)
    outb = jax.ShapeDtypeStruct((depth, n, width), BF16)
    spec5 = pl.BlockSpec((1, seqs, mem_len, M_HEADS, M_HEAD_DIM), lambda l, i: (l, i, 0, 0, 0))
    specb = pl.BlockSpec((1, tm, width), lambda l, i: (l, i, 0))
    wspec = pl.BlockSpec((1, D_MODEL, width), lambda l, i: (l, 0, 0))
    return pl.pallas_call(
        functools.partial(_mem_kv_kernel, seqs=seqs, mem_len=mem_len),
        grid=(depth, n // tm),
        in_specs=[pl.BlockSpec((tm, D_MODEL), lambda l, i: (i, 0)),
                  pl.BlockSpec((1, 1, D_MODEL), lambda l, i: (l, 0, 0)),
                  wspec, wspec],
        out_specs=[spec5, spec5, specb, specb],
        out_shape=[out5, out5, outb, outb],
        compiler_params=_params(2),
    )(mem.reshape(n, D_MODEL), ln_mem.reshape(depth, 1, D_MODEL), wk, wv)


def _diff_attn_kernel(lam_ref, q_ref, k_ref, v_ref, bias_ref, subln_ref, o_ref,
                      vt_sc, s_sc, acc_sc, *, tq, nd, hp, out_scale):
    i = pl.program_id(2)
    dv = A_V_DIM
    chains = [(h, c) for h in range(hp) for c in range(2)]
    n_ch = len(chains)

    @pl.when(i == 0)
    def _():
        for h in range(hp):
            for jj in range(nd):
                vt_sc[h, jj, :dv, :] = v_ref[0, jj * tq:(jj + 1) * tq, h * dv:(h + 1) * dv].T
                vt_sc[h, jj, dv:, :] = jnp.ones((ONES_ROWS, tq), BF16)

    row = lax.broadcasted_iota(jnp.int32, (dv, tq), 0)
    qzt = []
    for h in range(hp):
        qt = (q_ref[0, :, h * dv:(h + 1) * dv] * (A_QK_DIM ** -0.5)).T
        for c in range(2):
            qzt.append(jnp.where((row >= A_QK_DIM) == (c == 1), qt, jnp.zeros_like(qt)))

    def raw_scores(j, n):
        h = chains[n][0]
        start = pl.multiple_of(j * tq, tq)
        return _dot(k_ref[0, pl.ds(start, tq), h * dv:(h + 1) * dv], qzt[n])

    kk = lax.broadcasted_iota(jnp.int32, (tq, tq), 0)
    qq = lax.broadcasted_iota(jnp.int32, (tq, tq), 1)
    diag_penalty = jnp.where((kk // CHUNK) <= (qq // CHUNK), 0.0, NEG_INF)

    def pass1(js, ms):
        ms = list(ms)
        items = [(j, n) for j in js for n in range(n_ch)]
        raw_next = raw_scores(*items[0])
        for idx, (j, n) in enumerate(items):
            raw = raw_next
            if idx + 1 < len(items):
                raw_next = raw_scores(*items[idx + 1])
            h, c = chains[n]
            if c == 0:
                bias = bias_ref[h, j - i + (nd - 1)] + diag_penalty * jnp.where(j == i, 1.0, 0.0)
            s = (raw + bias) * LOG2_E
            s_sc[n, j] = s
            ms[n] = jnp.maximum(ms[n], jnp.max(s, axis=0, keepdims=True))
        return tuple(ms)

    def pass2(js):
        items = [(j, n) for j in js for n in range(n_ch)]
        p_next = jnp.exp2(s_sc[items[0][1], items[0][0]] - m_fin[items[0][1]])
        for idx, (j, n) in enumerate(items):
            p = p_next
            if idx + 1 < len(items):
                jn, nn = items[idx + 1]
                p_next = jnp.exp2(s_sc[nn, jn] - m_fin[nn])
            acc_sc[n] += _dot(vt_sc[chains[n][0], j], p.astype(BF16))

    n_tiles = i + 1
    plan = []
    first = 0
    for width in (4, 2, 1):
        trips = (n_tiles - first) // width
        plan.append((width, trips, first))
        first = first + trips * width

    ms = tuple(jnp.full((1, tq), NEG_INF, F32) for _ in range(n_ch))
    for width, trips, start in plan:
        ms = lax.fori_loop(
            0, trips,
            lambda tt, m, width=width, start=start:
                pass1([start + tt * width + u for u in range(width)], m), ms)
    m_fin = ms

    acc_sc[...] = jnp.zeros(acc_sc.shape, F32)
    for width, trips, start in plan:
        def trip2(tt, carry, width=width, start=start):
            pass2([start + tt * width + u for u in range(width)])
            return carry

        lax.fori_loop(0, trips, trip2, 0)
    for h in range(hp):
        acc0 = acc_sc[2 * h]
        acc1 = acc_sc[2 * h + 1]
        out0 = acc0[:dv] / acc0[dv:dv + 1]
        out1 = acc1[:dv] / acc1[dv:dv + 1]
        o = (out0 - lam_ref[0, 0] * out1).T
        o_ref[:, h * dv:(h + 1) * dv] = (_head_rms(o) * subln_ref[...] * out_scale).astype(BF16)


def _diff_attn(lam, z3, bias_tiles, subln, tq, hp, out_scale):
    bsz, t, _ = z3.shape
    nd = t // tq
    width = hp * A_V_DIM
    groups = A_HEADS // hp
    return pl.pallas_call(
        functools.partial(_diff_attn_kernel, tq=tq, nd=nd, hp=hp, out_scale=out_scale),
        grid=(bsz, groups, nd),
        in_specs=[pl.BlockSpec(memory_space=pltpu.SMEM),
                  pl.BlockSpec((1, tq, width), lambda b, g, i: (b, i, g)),
                  pl.BlockSpec((1, t, width), lambda b, g, i: (b, 0, groups + g)),
                  pl.BlockSpec((1, t, width), lambda b, g, i: (b, 0, 2 * groups + g)),
                  pl.BlockSpec((hp, nd, tq, tq), lambda b, g, i: (g, 0, 0, 0),
                               pipeline_mode=pl.Buffered(1)),
                  pl.BlockSpec((1, A_V_DIM), lambda b, g, i: (0, 0))],
        out_specs=pl.BlockSpec((tq, width), lambda b, g, i: (b * nd + i, g)),
        out_shape=jax.ShapeDtypeStruct((bsz * t, A_HEADS * A_V_DIM), BF16),
        scratch_shapes=[pltpu.VMEM((hp, nd, A_V_DIM + ONES_ROWS, tq), BF16),
                        pltpu.VMEM((2 * hp, nd, tq, tq), F32),
                        pltpu.VMEM((2 * hp, A_V_DIM + ONES_ROWS, tq), F32)],
        compiler_params=_params(3),
    )(lam, z3, z3, z3, bias_tiles, subln.reshape(1, A_V_DIM))


def _diff_attn_cached_kernel(lam_ref, q_ref, kn_ref, vn_ref, kp_ref, vp_ref, bp_ref, bn_ref,
                             subln_ref, o_ref, *, t, past, out_scale):
    lane = lax.broadcasted_iota(jnp.int32, (t, A_V_DIM), 1)
    qpos_p = past + lax.broadcasted_iota(jnp.int32, (2 * t, past), 0) % t
    kpos_p = lax.broadcasted_iota(jnp.int32, (2 * t, past), 1)
    ok_p = (kpos_p // CHUNK) <= (qpos_p // CHUNK)
    qpos_n = past + lax.broadcasted_iota(jnp.int32, (2 * t, t), 0) % t
    kpos_n = past + lax.broadcasted_iota(jnp.int32, (2 * t, t), 1)
    ok_n = (kpos_n // CHUNK) <= (qpos_n // CHUNK)
    for h in range(A_HEADS):
        cs = slice(h * A_V_DIM, (h + 1) * A_V_DIM)
        q = q_ref[0, :, cs] * (A_QK_DIM ** -0.5)
        zero = jnp.zeros_like(q)
        qp = jnp.concatenate([jnp.where(lane < A_QK_DIM, q, zero),
                              jnp.where(lane >= A_QK_DIM, q, zero)], axis=0)
        kp = kp_ref[0, :, h, :].astype(BF16)
        vp = vp_ref[0, :, h, :].astype(BF16)
        kn = kn_ref[0, :, cs]
        vn = vn_ref[0, :, cs]
        bp = bp_ref[h]
        bn = bn_ref[h]
        sp = jnp.where(ok_p, _dot_nt(qp, kp) + jnp.concatenate([bp, bp], axis=0), NEG_INF)
        sn = jnp.where(ok_n, _dot_nt(qp, kn) + jnp.concatenate([bn, bn], axis=0), NEG_INF)
        m = jnp.maximum(jnp.max(sp, axis=-1, keepdims=True), jnp.max(sn, axis=-1, keepdims=True))
        pp = jnp.exp(sp - m)
        pn = jnp.exp(sn - m)
        l = jnp.sum(pp, axis=-1, keepdims=True) + jnp.sum(pn, axis=-1, keepdims=True)
        out = (_dot(pp.astype(BF16), vp) + _dot(pn.astype(BF16), vn)) / l
        o = out[:t] - lam_ref[0, 0] * out[t:]
        o_ref[:, cs] = (_head_rms(o) * subln_ref[...] * out_scale).astype(BF16)


def _diff_attn_cached(lam, z3, past_k, past_v, bias_past, bias_new, subln, out_scale):
    bsz, t, _ = z3.shape
    past = past_k.shape[1]
    width = A_HEADS * A_V_DIM
    cache_spec = pl.BlockSpec((1, past, A_HEADS, A_V_DIM), lambda b: (b, 0, 0, 0))
    return pl.pallas_call(
        functools.partial(_diff_attn_cached_kernel, t=t, past=past, out_scale=out_scale),
        grid=(bsz,),
        in_specs=[pl.BlockSpec(memory_space=pltpu.SMEM),
                  pl.BlockSpec((1, t, width), lambda b: (b, 0, 0)),
                  pl.BlockSpec((1, t, width), lambda b: (b, 0, 1)),
                  pl.BlockSpec((1, t, width), lambda b: (b, 0, 2)),
                  cache_spec, cache_spec,
                  pl.BlockSpec((A_HEADS, t, past), lambda b: (0, 0, 0)),
                  pl.BlockSpec((A_HEADS, t, t), lambda b: (0, 0, 0)),
                  pl.BlockSpec((1, A_V_DIM), lambda b: (0, 0))],
        out_specs=pl.BlockSpec((t, width), lambda b: (b, 0)),
        out_shape=jax.ShapeDtypeStruct((bsz * t, width), BF16),
        compiler_params=_params(1),
    )(lam, z3, z3, z3, past_k, past_v, bias_past, bias_new, subln.reshape(1, A_V_DIM))


def _retention_kernel(q_ref, k_ref, v_ref, gt_ref, cos_ref, sin_ref, dec_ref, qd_ref, kd_ref,
                      cd_ref, s0_ref, o_ref, s_out_ref, s_sc):
    t = pl.program_id(1)

    @pl.when(t == 0)
    def _():
        s_sc[...] = s0_ref[0]

    d = B_QK_DIM
    half = d // 2
    cos = cos_ref[...]
    sin = sin_ref[...]
    for h in range(B_HEADS):
        cs = slice(h * d, (h + 1) * d)
        q = q_ref[0, :, cs].astype(F32)
        k = k_ref[0, :, cs].astype(F32)
        qr = (q * cos + pltpu.roll(q, half, 1) * sin) * (d ** -0.5)
        kr = k * cos + pltpu.roll(k, half, 1) * sin
        v = v_ref[0, :, cs]
        att = _dot_nt(qr.astype(BF16), kr.astype(BF16)) * dec_ref[h]
        s = s_sc[h]
        o = _dot(att.astype(BF16), v) + _dot((qr * qd_ref[h]).astype(BF16), s.astype(BF16))
        s_sc[h] = s * cd_ref[h] + _dot_tn((kr * kd_ref[h]).astype(BF16), v)
        gt = gt_ref[0, :, cs].astype(F32)
        o_ref[:, cs] = (_head_rms(o) * _silu(gt)).astype(BF16)

    @pl.when(t == pl.num_programs(1) - 1)
    def _():
        s_out_ref[0] = s_sc[...]


def _retention(z3, cos, sin, consts, s0, c):
    bsz, t, _ = z3.shape
    nt = t // c
    dec, qd, kd, cd = consts
    d = B_QK_DIM
    width = B_HEADS * d
    base = 3
    full3 = lambda b, i: (0, 0, 0)
    return pl.pallas_call(
        _retention_kernel,
        grid=(bsz, nt),
        in_specs=[pl.BlockSpec((1, c, width), lambda b, i: (b, i, base)),
                  pl.BlockSpec((1, c, width), lambda b, i: (b, i, base + 1)),
                  pl.BlockSpec((1, c, width), lambda b, i: (b, i, base + 2)),
                  pl.BlockSpec((1, c, width), lambda b, i: (b, i, base + 3)),
                  pl.BlockSpec((c, d), lambda b, i: (i, 0)),
                  pl.BlockSpec((c, d), lambda b, i: (i, 0)),
                  pl.BlockSpec((B_HEADS, c, c), full3),
                  pl.BlockSpec((B_HEADS, c, d), full3),
                  pl.BlockSpec((B_HEADS, c, d), full3),
                  pl.BlockSpec((B_HEADS, 1, d), full3),
                  pl.BlockSpec((1, B_HEADS, d, d), lambda b, i: (b, 0, 0, 0))],
        out_specs=[pl.BlockSpec((c, width), lambda b, i: (b * nt + i, 0)),
                   pl.BlockSpec((1, B_HEADS, d, d), lambda b, i: (b, 0, 0, 0))],
        out_shape=[jax.ShapeDtypeStruct((bsz * t, width), BF16),
                   jax.ShapeDtypeStruct((bsz, B_HEADS, d, d), F32)],
        scratch_shapes=[pltpu.VMEM((B_HEADS, d, d), F32)],
        compiler_params=_params(2),
    )(z3, z3, z3, z3, cos, sin, dec, qd, kd, cd, s0)


def _retention_consts(c):
    log_g = jnp.log1p(-jnp.exp2(-5.0 - jnp.arange(B_HEADS, dtype=F32)))
    idx = jnp.arange(c, dtype=F32)
    dist = idx[:, None] - idx[None, :]
    dec = jnp.where(dist >= 0, jnp.exp(jnp.maximum(dist, 0.0)[None] * log_g[:, None, None]), 0.0)
    qd = jnp.exp((idx[None, :] + 1.0) * log_g[:, None])
    kd = jnp.exp((c - 1.0 - idx)[None, :] * log_g[:, None])
    cd = jnp.exp(c * log_g)
    bc = lambda a: jnp.broadcast_to(a[..., None], a.shape + (B_QK_DIM,))
    return dec, bc(qd), bc(kd), bc(cd[:, None])


def _rotary_tables(pos):
    half = B_QK_DIM // 2
    inv = ROPE_BASE ** (-jnp.arange(half, dtype=F32) / half)
    ang = pos.astype(F32)[:, None] * inv[None, :]
    cos, sin = jnp.cos(ang), jnp.sin(ang)
    return jnp.concatenate([cos, cos], axis=-1), jnp.concatenate([-sin, sin], axis=-1)


def _gla_levels(c):
    return [c >> (l + 1) for l in range(int(math.log2(c)))]


def _gla_consts(c):
    levels = _gla_levels(c)
    rows = np.arange(c)
    mats = []
    for s in levels:
        ref = (rows // (2 * s)) * 2 * s + s - 1
        a = np.zeros((c, c), np.float32)
        for i in range(c):
            if i & s:
                a[i, ref[i] + 1:i + 1] = 1.0
            else:
                a[i, i + 1:ref[i] + 1] = 1.0
        mats.append(a)
    mats.append(np.tril(np.ones((c, c), np.float32)))
    mats.append(np.triu(np.ones((c, c), np.float32), 1))
    lv = np.full((c, c), -1, np.int32)
    for i in range(c):
        lv[i, i] = len(levels)
        for j in range(i):
            lv[i, j] = levels.index(1 << int(math.floor(math.log2(i ^ j))))
    return jnp.asarray(np.concatenate(mats, axis=0), BF16), jnp.asarray(lv)


def _gla_kernel(q_ref, k_ref, v_ref, r_ref, g_ref, s0_ref, a_ref, lv_ref, nw_ref,
                o_ref, s_out_ref, st_sc, *, c):
    t = pl.program_id(1)
    dk, dv = C_QK_DIM, C_V_DIM

    @pl.when(t == 0)
    def _():
        for h in range(C_HEADS):
            st_sc[h] = s0_ref[0, h].T

    levels = _gla_levels(c)
    n_lv = len(levels)
    half_blocks = (c // 2) % LANES == 0
    lv = lv_ref[...]
    row = lax.broadcasted_iota(jnp.int32, (c, dk), 0)
    g = g_ref[0]
    xs = _dot(a_ref[...], g.astype(BF16))
    for h in range(C_HEADS):
        ks = slice(h * dk, (h + 1) * dk)
        vs = slice(h * dv, (h + 1) * dv)
        x = xs[:, ks]
        q = q_ref[0, :, ks].astype(F32) * (dk ** -0.5)
        k = k_ref[0, :, ks].astype(F32)
        v = v_ref[0, :, vs]
        if half_blocks:
            hc = c // 2
            lvh = lv[:hc, :hc]
            diag = [jnp.zeros((hc, hc), F32), jnp.zeros((hc, hc), F32)]
            qb, kb = q.astype(BF16), k.astype(BF16)
            pending = (n_lv, [_dot_nt(qb[:hc], kb[:hc]), _dot_nt(qb[hc:], kb[hc:])])
        else:
            att = jnp.zeros((c, c), F32)
            pending = (n_lv, _dot_nt(q.astype(BF16), k.astype(BF16)))
        for l, s in enumerate(levels):
            e = jnp.exp(x[l * c:(l + 1) * c])
            up = (row & s) != 0
            mix = jnp.where(up, q, k) * e
            qt = jnp.where(up, mix, 0.0).astype(BF16)
            kt = jnp.where(up, 0.0, mix).astype(BF16)
            if not half_blocks:
                prod = _dot_nt(qt, kt)
                att = jnp.where(lv == pending[0], pending[1], att)
            elif l == 0:
                lower_left = _dot_nt(qt[hc:], kt[:hc])
                continue
            else:
                prod = [_dot_nt(qt[:hc], kt[:hc]), _dot_nt(qt[hc:], kt[hc:])]
                diag = [jnp.where(lvh == pending[0], pending[1][u], diag[u]) for u in range(2)]
            pending = (l, prod)
        if half_blocks:
            diag = [jnp.where(lvh == pending[0], pending[1][u], diag[u]) for u in range(2)]
            att = jnp.concatenate(
                [jnp.concatenate([diag[0], jnp.zeros((hc, hc), F32)], axis=1),
                 jnp.concatenate([lower_left, diag[1]], axis=1)], axis=0)
        else:
            att = jnp.where(lv == pending[0], pending[1], att)
        b = x[n_lv * c:(n_lv + 1) * c]
        rem = x[(n_lv + 1) * c:(n_lv + 2) * c]
        st = st_sc[h]
        o = _dot(att.astype(BF16), v) + _dot_nt((q * jnp.exp(b)).astype(BF16), st.astype(BF16))
        kd = (k * jnp.exp(rem)).astype(BF16)
        st_sc[h] = st * jnp.exp(b[c - 1:c, :]) + _dot_tn(v, kd)
        r = r_ref[0, :, vs].astype(F32)
        o_ref[:, vs] = (_head_rms(o) * nw_ref[...] * _silu(r)).astype(BF16)

    @pl.when(t == pl.num_programs(1) - 1)
    def _():
        for h in range(C_HEADS):
            s_out_ref[0, h] = st_sc[h].T


def _gla(z3, log_a3, s0, a_stack, lv, norm_w, c):
    bsz, t, _ = z3.shape
    nt = t // c
    dk, dv = C_QK_DIM, C_V_DIM
    qk_w, v_w = C_HEADS * dk, C_HEADS * dv
    const = lambda b, i: (0, 0)
    return pl.pallas_call(
        functools.partial(_gla_kernel, c=c),
        grid=(bsz, nt),
        in_specs=[pl.BlockSpec((1, c, qk_w), lambda b, i: (b, i, 0)),
                  pl.BlockSpec((1, c, qk_w), lambda b, i: (b, i, 1)),
                  pl.BlockSpec((1, c, v_w), lambda b, i: (b, i, 1)),
                  pl.BlockSpec((1, c, v_w), lambda b, i: (b, i, 2)),
                  pl.BlockSpec((1, c, qk_w), lambda b, i: (b, i, 0)),
                  pl.BlockSpec((1, C_HEADS, dk, dv), lambda b, i: (b, 0, 0, 0)),
                  pl.BlockSpec(a_stack.shape, const),
                  pl.BlockSpec((c, c), const),
                  pl.BlockSpec((1, dv), const)],
        out_specs=[pl.BlockSpec((c, v_w), lambda b, i: (b * nt + i, 0)),
                   pl.BlockSpec((1, C_HEADS, dk, dv), lambda b, i: (b, 0, 0, 0))],
        out_shape=[jax.ShapeDtypeStruct((bsz * t, v_w), BF16),
                   jax.ShapeDtypeStruct((bsz, C_HEADS, dk, dv), F32)],
        scratch_shapes=[pltpu.VMEM((C_HEADS, dv, dk), F32)],
        compiler_params=_params(2),
    )(z3, z3, z3, z3, log_a3, s0, a_stack, lv, norm_w.reshape(1, dv))


def _cross_kernel(*refs, n_mix, n_sub):
    x_ref = refs[0]
    a_refs = refs[1:1 + n_mix]
    w_refs = refs[1 + n_mix:1 + 2 * n_mix]
    g_ref, wq_ref, mk_ref, mv_ref, wo_ref, o_ref = refs[1 + 2 * n_mix:]
    def head(ref, hd):
        if len(ref.shape) == 5:
            return ref[0, 0, :, hd, :].astype(BF16)
        return ref[0, 0, :, hd * M_HEAD_DIM:(hd + 1) * M_HEAD_DIM].astype(BF16)

    tr = x_ref.shape[0] // n_sub
    xs = []
    for sub in range(n_sub):
        rows = slice(sub * tr, (sub + 1) * tr)
        x = x_ref[rows, :]
        for a_ref, w_ref in zip(a_refs, w_refs):
            x = x + _dot(a_ref[rows, :], w_ref[...])
        xs.append(x)
    for sub in range(n_sub):
        x = xs[sub]
        h = _rms(x, g_ref[...]).astype(BF16)
        q = (_dot(h, wq_ref[0]) * (M_HEAD_DIM ** -0.5)).astype(BF16)
        outs = []
        for hd in range(M_HEADS):
            cs = slice(hd * M_HEAD_DIM, (hd + 1) * M_HEAD_DIM)
            s = _dot_nt(q[:, cs], head(mk_ref, hd))
            p = jnp.exp(s - jnp.max(s, axis=-1, keepdims=True))
            l = jnp.sum(p, axis=-1, keepdims=True)
            outs.append((_dot(p.astype(BF16), head(mv_ref, hd)) / l).astype(BF16))
        o_ref[sub * tr:(sub + 1) * tr, :] = x + _dot(jnp.concatenate(outs, axis=1), wo_ref[0])


def _cross(x2d, mix_list, w_mix_list, gamma, wq, mk, mv, wo, layer, t, tm):
    n = x2d.shape[0]
    nt = t // tm
    width = M_HEADS * M_HEAD_DIM
    rows = lambda b, i: (b * nt + i, 0)
    const = lambda b, i: (0, 0)
    of_layer = lambda b, i: (layer, 0, 0)
    mem_spec = pl.BlockSpec((1, 1) + mk.shape[2:], lambda b, i: (layer, b) + (0,) * (mk.ndim - 2))
    in_specs = [pl.BlockSpec((tm, D_MODEL), rows)]
    in_specs += [pl.BlockSpec((tm, a.shape[1]), rows) for a in mix_list]
    in_specs += [pl.BlockSpec(w.shape, const) for w in w_mix_list]
    in_specs += [pl.BlockSpec((1, D_MODEL), const),
                 pl.BlockSpec((1, D_MODEL, width), of_layer),
                 mem_spec, mem_spec,
                 pl.BlockSpec((1, width, D_MODEL), of_layer)]
    return pl.pallas_call(
        functools.partial(_cross_kernel, n_mix=len(mix_list), n_sub=max(1, tm // CROSS_SUB_ROWS)),
        grid=(n // t, nt),
        in_specs=in_specs,
        out_specs=pl.BlockSpec((tm, D_MODEL), rows),
        out_shape=jax.ShapeDtypeStruct((n, D_MODEL), F32),
        compiler_params=_params(2),
    )(x2d, *mix_list, *w_mix_list, gamma.reshape(1, D_MODEL), wq, mk, mv, wo)


def _ffn_kernel(x_ref, g_ref, wg_ref, wu_ref, cw_ref, cb_ref, prev_ref, wd_ref, *rest,
                tm, n_sub, final_norm):
    if final_norm:
        gf_ref, o_ref, tail_ref, carry = rest
    else:
        o_ref, tail_ref, carry = rest
    t = pl.program_id(1)
    lo = SUBLANES - 2

    @pl.when(t == 0)
    def _():
        carry[lo:SUBLANES, :] = prev_ref[0]

    row = lax.broadcasted_iota(jnp.int32, (SUBLANES, FF_CHUNK), 0)
    n_chunks = D_FF // FF_CHUNK
    chunk = lambda c: slice(c * FF_CHUNK, (c + 1) * FF_CHUNK)
    tr = tm // n_sub
    h_next = _rms(x_ref[0:tr, :], g_ref[...]).astype(BF16)
    for sub in range(n_sub):
        rows = slice(sub * tr, (sub + 1) * tr)
        h = h_next
        acc = x_ref[rows, :]
        nxt = (_dot(h, wg_ref[0, :, chunk(0)]), _dot(h, wu_ref[0, :, chunk(0)]))
        pending = []
        for c in range(n_chunks):
            cs = chunk(c)
            gate, up = nxt
            if c + 1 < n_chunks:
                nxt = (_dot(h, wg_ref[0, :, chunk(c + 1)]), _dot(h, wu_ref[0, :, chunk(c + 1)]))
            if c == n_chunks // 2 and sub + 1 < n_sub:
                h_next = _rms(x_ref[(sub + 1) * tr:(sub + 2) * tr, :], g_ref[...]).astype(BF16)
            p1 = carry[SUBLANES - 1:SUBLANES, cs]
            p2 = carry[lo:lo + 1, cs]
            r1 = pltpu.roll(gate, 1, 0)
            r2 = pltpu.roll(gate, 2, 0)
            h1 = jnp.where(row == 0, p1, r1[:SUBLANES])
            h2 = jnp.where(row == 0, p2, jnp.where(row == 1, p1, r2[:SUBLANES]))
            g1 = jnp.concatenate([h1, r1[SUBLANES:]], axis=0)
            g2 = jnp.concatenate([h2, r2[SUBLANES:]], axis=0)
            carry[:, cs] = gate[tr - SUBLANES:tr]
            conv = cb_ref[:, cs] + cw_ref[0:1, cs] * g2
            conv = conv + cw_ref[1:2, cs] * g1
            conv = conv + cw_ref[2:3, cs] * gate
            pending.append((_gelu_tanh(conv) * up).astype(BF16))
            if len(pending) == DOWN_GROUP or c == n_chunks - 1:
                lo_col = (c + 1 - len(pending)) * FF_CHUNK
                acc = acc + _dot(jnp.concatenate(pending, axis=1),
                                 wd_ref[0, lo_col:(c + 1) * FF_CHUNK, :])
                pending = []
        if final_norm:
            o_ref[rows, :] = _rms(acc, gf_ref[...])
        else:
            o_ref[rows, :] = acc

    @pl.when(t == pl.num_programs(1) - 1)
    def _():
        tail_ref[0] = carry[lo:SUBLANES, :]


def _ffn(x2d, gamma, wg, wu, conv_w, conv_b, prev, wd, layer, t, tm, final_gamma=None):
    n = x2d.shape[0]
    nt = t // tm
    bsz = n // t
    const = lambda b, i: (0, 0)
    of_layer = lambda b, i: (layer, 0, 0)
    resident = dict(pipeline_mode=pl.Buffered(1))
    in_specs = [pl.BlockSpec((tm, D_MODEL), lambda b, i: (b * nt + i, 0)),
                pl.BlockSpec((1, D_MODEL), const),
                pl.BlockSpec((1, D_MODEL, D_FF), of_layer, **resident),
                pl.BlockSpec((1, D_MODEL, D_FF), of_layer, **resident),
                pl.BlockSpec((3, D_FF), const),
                pl.BlockSpec((1, D_FF), const),
                pl.BlockSpec((1, 2, D_FF), lambda b, i: (b, 0, 0)),
                pl.BlockSpec((1, D_FF, D_MODEL), of_layer, **resident)]
    args = [x2d, gamma.reshape(1, D_MODEL), wg, wu, conv_w, conv_b.reshape(1, D_FF), prev, wd]
    if final_gamma is not None:
        in_specs.append(pl.BlockSpec((1, D_MODEL), const))
        args.append(final_gamma.reshape(1, D_MODEL))
    return pl.pallas_call(
        functools.partial(_ffn_kernel, tm=tm, n_sub=max(1, tm // FFN_SUB_ROWS),
                          final_norm=final_gamma is not None),
        grid=(bsz, nt),
        in_specs=in_specs,
        out_specs=[pl.BlockSpec((tm, D_MODEL), lambda b, i: (b * nt + i, 0)),
                   pl.BlockSpec((1, 2, D_FF), lambda b, i: (b, 0, 0))],
        out_shape=[jax.ShapeDtypeStruct((n, D_MODEL), F32),
                   jax.ShapeDtypeStruct((bsz, 2, D_FF), F32)],
        scratch_shapes=[pltpu.VMEM((SUBLANES, D_FF), F32)],
        compiler_params=_params(2),
    )(*args)


def _rel_bucket(rel):
    nb = REL_BUCKETS // 2
    max_exact = nb // 2
    n = jnp.abs(rel)
    nf = jnp.maximum(n, 1).astype(F32)
    large = max_exact + (jnp.log(nf / max_exact) / math.log(REL_MAX_DIST / max_exact)
                         * (nb - max_exact)).astype(jnp.int32)
    large = jnp.minimum(large, nb - 1)
    return jnp.where(rel > 0, nb, 0) + jnp.where(n < max_exact, n, large)


def _toeplitz_kernel(w_ref, o_ref, *, rows, cols):
    w = w_ref[0]
    x = jnp.broadcast_to(w, (rows, w.shape[1]))
    o_ref[0] = pltpu.roll(x, 0, 1, stride=1, stride_axis=0)[:, :cols]


def _toeplitz(w, rows, cols):
    groups, period = w.shape
    return pl.pallas_call(
        functools.partial(_toeplitz_kernel, rows=rows, cols=cols),
        grid=(groups,),
        in_specs=[pl.BlockSpec((1, 1, period), lambda g: (g, 0, 0))],
        out_specs=pl.BlockSpec((1, rows, cols), lambda g: (g, 0, 0)),
        out_shape=jax.ShapeDtypeStruct((groups, rows, cols), F32),
        compiler_params=_params(1),
    )(w.reshape(groups, 1, period))


def _bias_tiles(rel_bias, tq, nd):
    period = 2 * tq
    n = np.arange(period)
    rel = np.stack([np.where(n < tq, (d - (nd - 1)) * tq - n, (d - (nd - 1)) * tq + period - n)
                    for d in range(nd)]).astype(np.int32)
    w = jnp.transpose(rel_bias[_rel_bucket(jnp.asarray(rel))], (2, 0, 1)).astype(F32)
    return _toeplitz(w.reshape(A_HEADS * nd, period), tq, tq).reshape(A_HEADS, nd, tq, tq)


def _bias_rows(rel_bias, t, past):
    n_keys = past + t
    period = -(-(n_keys + t) // LANES) * LANES
    m = np.arange(period)
    rel = np.where(m < n_keys, m - past, m - period - past).astype(np.int32)
    w = jnp.transpose(rel_bias[_rel_bucket(jnp.asarray(rel))], (1, 0)).astype(F32)
    bias = _toeplitz(w, t, n_keys)
    return bias[:, :, :past], bias[:, :, past:]


def _forward(x, past_k, past_v, ret_state, gla_state, conv_prev, mem_k, mem_v, p, w):
    bsz, t, _ = x.shape
    n = bsz * t
    past = 0 if past_k is None else past_k.shape[2]
    tm = min(2 * PROJ_SUB_ROWS, n)
    tseq = min(2 * CROSS_SUB_ROWS, t)
    tffn = min(2 * FFN_SUB_ROWS, t)
    tq = min(256, t)
    c_ret = min(256, t)
    c_gla = min(256, t)
    x2 = x.reshape(n, D_MODEL)
    pos = past + jnp.arange(t, dtype=jnp.int32)
    new_conv = []

    z, ak, av = _in_even(x2, p["ln_mix"][0], w["w_in_even"], tm)
    z3 = z.reshape(bsz, t, EVEN_IN)
    lam_init = 0.8 - 0.6 * math.exp(-0.3 * 0)
    lam = (jnp.exp(jnp.sum(p["diff_lq1"][0].astype(F32) * p["diff_lk1"][0].astype(F32)))
           - jnp.exp(jnp.sum(p["diff_lq2"][0].astype(F32) * p["diff_lk2"][0].astype(F32)))
           + lam_init).reshape(1, 1).astype(F32)
    if past == 0:
        o_a = _diff_attn(lam, z3, _bias_tiles(p["rel_bias"], tq, t // tq), p["diff_subln"][0],
                         tq, A_HEADS, 1.0 - lam_init)
    else:
        bias_past, bias_new = _bias_rows(p["rel_bias"], t, past)
        o_a = _diff_attn_cached(lam, z3, past_k[0], past_v[0], bias_past, bias_new,
                                p["diff_subln"][0], 1.0 - lam_init)
    cos, sin = _rotary_tables(pos)
    o_b, s_ret = _retention(z3, cos, sin, _retention_consts(c_ret), ret_state[0], c_ret)
    x2 = _cross(x2, [o_a, o_b], [w["w_out_even_a"], w["w_out_even_b"]], p["ln_cross"][0],
                w["w_cq"], mem_k, mem_v, w["w_co"], 0, t, tseq)
    x2, tail = _ffn(x2, p["ln_ffn"][0], w["w_ffn_gate"], w["w_ffn_up"], p["ffn_conv_w"][0],
                    p["ffn_conv_b"][0], conv_prev[0], w["w_ffn_down"], 0, t, tffn)
    new_conv.append(tail)

    zc, log_a = _in_odd(x2, p["ln_mix"][1], w["w_in_odd"], w["w_gate_lr"], p["b_gate"][0], tm)
    a_stack, lv = _gla_consts(c_gla)
    o_c, s_gla = _gla(zc.reshape(bsz, t, ODD_Z), log_a.reshape(bsz, t, C_HEADS * C_QK_DIM),
                      gla_state[0], a_stack, lv, p["gla_norm"][0], c_gla)
    x2 = _cross(x2, [o_c], [w["w_out_odd"]], p["ln_cross"][1],
                w["w_cq"], mem_k, mem_v, w["w_co"], 1, t, tseq)
    y, tail = _ffn(x2, p["ln_ffn"][1], w["w_ffn_gate"], w["w_ffn_up"], p["ffn_conv_w"][1],
                   p["ffn_conv_b"][1], conv_prev[1], w["w_ffn_down"], 1, t, tffn,
                   final_gamma=p["ln_final"])
    new_conv.append(tail)

    return (y.reshape(bsz, t, D_MODEL),
            ak.reshape(1, bsz, t, A_HEADS, A_V_DIM), av.reshape(1, bsz, t, A_HEADS, A_V_DIM),
            s_ret[None], s_gla[None], jnp.stack(new_conv))


def kernel(x_prompt, x_sample, cache_diff_k, cache_diff_v, state_retention, state_gla, cache_ffn_conv, cache_mem_k, cache_mem_v, mem_prompt, ln_mix, ln_cross, ln_ffn, ln_mem, ln_final, w_in_even, w_out_even, diff_lq1, diff_lk1, diff_lq2, diff_lk2, diff_subln, rel_bias, w_in_odd, w_gate_lr, b_gate, gla_norm, w_out_odd, w_cq, w_ck, w_cv, w_co, w_ffn_gate, w_ffn_up, ffn_conv_w, ffn_conv_b, w_ffn_down):
    p = dict(ln_mix=ln_mix, ln_cross=ln_cross, ln_ffn=ln_ffn, ln_final=ln_final,
             diff_lq1=diff_lq1, diff_lk1=diff_lk1, diff_lq2=diff_lq2, diff_lk2=diff_lk2,
             diff_subln=diff_subln, rel_bias=rel_bias, b_gate=b_gate, gla_norm=gla_norm,
             ffn_conv_w=ffn_conv_w, ffn_conv_b=ffn_conv_b)
    a_v = A_HEADS * A_V_DIM
    w = dict(
        w_in_even=w_in_even[0].astype(BF16),
        w_out_even_a=w_out_even[0, :a_v].astype(BF16),
        w_out_even_b=w_out_even[0, a_v:].astype(BF16),
        w_in_odd=jnp.pad(w_in_odd[0], ((0, 0), (0, ODD_IN_PAD - w_in_odd.shape[2]))).astype(BF16),
        w_gate_lr=jnp.pad(w_gate_lr[0], ((0, LANES - C_GATE_RANK), (0, 0))).astype(BF16),
        w_out_odd=w_out_odd[0].astype(BF16),
        w_cq=w_cq.astype(BF16), w_co=w_co.astype(BF16),
        w_ffn_gate=w_ffn_gate.astype(BF16), w_ffn_up=w_ffn_up.astype(BF16),
        w_ffn_down=w_ffn_down.astype(BF16))

    bp, mem_len, _ = mem_prompt.shape
    m_width = M_HEADS * M_HEAD_DIM
    depth = ln_mem.shape[0]
    mem_k_p, mem_v_p, mk_b, mv_b = _mem_kv(mem_prompt, ln_mem, w_ck.astype(BF16),
                                           w_cv.astype(BF16), seqs=2)

    dt = x_prompt.dtype
    zero_ret = jnp.zeros((1, bp, B_HEADS, B_QK_DIM, B_QK_DIM), dt)
    zero_gla = jnp.zeros((1, bp, C_HEADS, C_QK_DIM, C_V_DIM), dt)
    zero_conv = jnp.zeros((depth, bp, 2, D_FF), dt)
    y_p, dk_p, dv_p, ret_p, gla_p, conv_p = _forward(
        x_prompt, None, None, zero_ret, zero_gla, zero_conv,
        mk_b.reshape(depth, bp, mem_len, m_width), mv_b.reshape(depth, bp, mem_len, m_width), p, w)
    y_s, dk_s, dv_s, ret_s, gla_s, conv_s = _forward(
        x_sample, cache_diff_k, cache_diff_v, state_retention, state_gla, cache_ffn_conv,
        cache_mem_k, cache_mem_v, p, w)
    return (y_p, y_s, dk_p, dv_p, ret_p, gla_p, conv_p, mem_k_p, mem_v_p,
            dk_s, dv_s, ret_s, gla_s, conv_s)
```

```python
import functools
import math

import numpy as np
import jax
import jax.numpy as jnp
from jax import lax
from jax.experimental import pallas as pl
from jax.experimental.pallas import tpu as pltpu

F32 = jnp.float32
BF16 = jnp.bfloat16

D_MODEL = 1024
CHUNK = 64
A_HEADS = 4
A_QK_DIM = 64
A_V_DIM = 128
B_HEADS = 4
B_QK_DIM = 128
C_HEADS = 4
C_QK_DIM = 128
C_V_DIM = 256
C_GATE_RANK = 16
C_GATE_TAU = 16.0
M_HEADS = 4
M_HEAD_DIM = 256
REL_BUCKETS = 32
REL_MAX_DIST = 128
D_FF = 2816
ROPE_BASE = 10000.0
EPS = 1e-6
NEG_INF = -1e30

EVEN_IN = 3584
ODD_Z = 3072
LANES = 128
SUBLANES = 8
ODD_IN_PAD = ODD_Z + LANES
FF_CHUNK = 256
DOWN_GROUP = 6
FFN_SUB_ROWS = 256
CROSS_SUB_ROWS = 512
PROJ_SUB_ROWS = 512
ONES_ROWS = 16
LOG2_E = math.log2(math.e)
V7X_VMEM_LIMIT_BYTES = 56 * 1024 * 1024


def _params(n_axes):
    return pltpu.CompilerParams(dimension_semantics=("arbitrary",) * n_axes,
                                vmem_limit_bytes=V7X_VMEM_LIMIT_BYTES)


def _dot(a, b):
    return jnp.dot(a, b, preferred_element_type=F32)


def _dot_nt(a, b):
    return lax.dot_general(a, b, (((1,), (1,)), ((), ())), preferred_element_type=F32)


def _dot_tn(a, b):
    return lax.dot_general(a, b, (((0,), (0,)), ((), ())), preferred_element_type=F32)


def _rms(x, g):
    return x * lax.rsqrt(jnp.mean(x * x, axis=-1, keepdims=True) + EPS) * g


def _head_rms(x):
    return x * lax.rsqrt(jnp.mean(x * x, axis=-1, keepdims=True) + EPS)


def _silu(x):
    return x * (1.0 / (1.0 + jnp.exp(-x)))


def _gelu_tanh(x):
    c0 = math.sqrt(2.0 / math.pi)
    return x * (0.5 + 0.5 * jnp.tanh(x * (c0 + (c0 * 0.044715) * (x * x))))


def _log_sigmoid(x):
    return jnp.minimum(x, 0.0) - jnp.log(1.0 + jnp.exp(-jnp.abs(x)))


def _store_head_major(o_ref, val, tm, n_heads, head_dim):
    tiles = head_dim // LANES
    group = n_heads * tiles
    for hd in range(n_heads):
        for part in range(tiles):
            col = hd * head_dim + part * LANES
            o_ref[pl.ds(part * n_heads + hd, tm, stride=group), :] = val[:, col:col + LANES]


def _in_even_kernel(x_ref, g_ref, w_ref, z_ref, k_ref, v_ref, *, n_sub):
    width = A_HEADS * A_V_DIM
    tr = x_ref.shape[0] // n_sub
    h_next = _rms(x_ref[0:tr, :], g_ref[...]).astype(BF16)
    for sub in range(n_sub):
        rows = slice(sub * tr, (sub + 1) * tr)
        h = h_next
        for c in range(EVEN_IN // width):
            zc = _dot(h, w_ref[:, c * width:(c + 1) * width])
            z_ref[rows, c * width:(c + 1) * width] = zc.astype(BF16)
            head_rows = slice(sub * tr * A_HEADS, (sub + 1) * tr * A_HEADS)
            if c == 0 and sub + 1 < n_sub:
                h_next = _rms(x_ref[(sub + 1) * tr:(sub + 2) * tr, :], g_ref[...]).astype(BF16)
            if c == 1:
                _store_head_major(k_ref.at[head_rows, :], zc, tr, A_HEADS, A_V_DIM)
            if c == 2:
                _store_head_major(v_ref.at[head_rows, :], zc, tr, A_HEADS, A_V_DIM)


def _in_even(x2d, gamma, w, tm):
    n = x2d.shape[0]
    return pl.pallas_call(
        functools.partial(_in_even_kernel, n_sub=max(1, tm // PROJ_SUB_ROWS)),
        grid=(n // tm,),
        in_specs=[pl.BlockSpec((tm, D_MODEL), lambda i: (i, 0)),
                  pl.BlockSpec((1, D_MODEL), lambda i: (0, 0)),
                  pl.BlockSpec((D_MODEL, EVEN_IN), lambda i: (0, 0), pipeline_mode=pl.Buffered(1))],
        out_specs=[pl.BlockSpec((tm, EVEN_IN), lambda i: (i, 0)),
                   pl.BlockSpec((tm * A_HEADS, A_V_DIM), lambda i: (i, 0)),
                   pl.BlockSpec((tm * A_HEADS, A_V_DIM), lambda i: (i, 0))],
        out_shape=[jax.ShapeDtypeStruct((n, EVEN_IN), BF16),
                   jax.ShapeDtypeStruct((n * A_HEADS, A_V_DIM), F32),
                   jax.ShapeDtypeStruct((n * A_HEADS, A_V_DIM), F32)],
        compiler_params=_params(1),
    )(x2d, gamma.reshape(1, D_MODEL), w)


def _in_odd_kernel(x_ref, g_ref, w_ref, wlr_ref, bg_ref, z_ref, la_ref, *, n_sub):
    width = 512
    tr = x_ref.shape[0] // n_sub

    def norm_and_gates(sub):
        rows = slice(sub * tr, (sub + 1) * tr)
        h = _rms(x_ref[rows, :], g_ref[...]).astype(BF16)
        ca = _dot(h, w_ref[:, ODD_Z:ODD_IN_PAD]).astype(BF16)
        pre = _dot(ca, wlr_ref[...]) + bg_ref[...]
        la_ref[rows, :] = _log_sigmoid(pre) / C_GATE_TAU
        return h

    h_next = norm_and_gates(0)
    for sub in range(n_sub):
        rows = slice(sub * tr, (sub + 1) * tr)
        h = h_next
        for c in range(ODD_Z // width):
            z_ref[rows, c * width:(c + 1) * width] = _dot(
                h, w_ref[:, c * width:(c + 1) * width]).astype(BF16)
            if c == 0 and sub + 1 < n_sub:
                h_next = norm_and_gates(sub + 1)


def _in_odd(x2d, gamma, w_pad, wlr_pad, b_gate, tm):
    n = x2d.shape[0]
    qk = C_HEADS * C_QK_DIM
    return pl.pallas_call(
        functools.partial(_in_odd_kernel, n_sub=max(1, tm // PROJ_SUB_ROWS)),
        grid=(n // tm,),
        in_specs=[pl.BlockSpec((tm, D_MODEL), lambda i: (i, 0)),
                  pl.BlockSpec((1, D_MODEL), lambda i: (0, 0)),
                  pl.BlockSpec((D_MODEL, ODD_IN_PAD), lambda i: (0, 0),
                               pipeline_mode=pl.Buffered(1)),
                  pl.BlockSpec((LANES, qk), lambda i: (0, 0)),
                  pl.BlockSpec((1, qk), lambda i: (0, 0))],
        out_specs=[pl.BlockSpec((tm, ODD_Z), lambda i: (i, 0)),
                   pl.BlockSpec((tm, qk), lambda i: (i, 0))],
        out_shape=[jax.ShapeDtypeStruct((n, ODD_Z), BF16),
                   jax.ShapeDtypeStruct((n, qk), F32)],
        compiler_params=_params(1),
    )(x2d, gamma.reshape(1, D_MODEL), w_pad, wlr_pad, b_gate.reshape(1, qk))


def _mem_kv_kernel(x_ref, g_ref, wk_ref, wv_ref, k5_ref, v5_ref, kb_ref, vb_ref, *, seqs, mem_len):
    h = _rms(x_ref[...], g_ref[0]).astype(BF16)
    for w_ref, o5_ref, ob_ref in ((wk_ref, k5_ref, kb_ref), (wv_ref, v5_ref, vb_ref)):
        y = _dot(h, w_ref[0])
        ob_ref[0] = y.astype(BF16)
        for s in range(seqs):
            for hd in range(M_HEADS):
                o5_ref[0, s, :, hd, :] = y[s * mem_len:(s + 1) * mem_len,
                                           hd * M_HEAD_DIM:(hd + 1) * M_HEAD_DIM]


def _mem_kv(mem, ln_mem, wk, wv, seqs):
    bsz, mem_len, _ = mem.shape
    depth = ln_mem.shape[0]
    width = M_HEADS * M_HEAD_DIM
    n = bsz * mem_len
    tm = seqs * mem_len
    out5 = jax.ShapeDtypeStruct((depth, bsz, mem_len, M_HEADS, M_HEAD_DIM), F32)
    outb = jax.ShapeDtypeStruct((depth, n, width), BF16)
    spec5 = pl.BlockSpec((1, seqs, mem_len, M_HEADS, M_HEAD_DIM), lambda l, i: (l, i, 0, 0, 0))
    specb = pl.BlockSpec((1, tm, width), lambda l, i: (l, i, 0))
    wspec = pl.BlockSpec((1, D_MODEL, width), lambda l, i: (l, 0, 0))
    return pl.pallas_call(
        functools.partial(_mem_kv_kernel, seqs=seqs, mem_len=mem_len),
        grid=(depth, n // tm),
        in_specs=[pl.BlockSpec((tm, D_MODEL), lambda l, i: (i, 0)),
                  pl.BlockSpec((1, 1, D_MODEL), lambda l, i: (l, 0, 0)),
                  wspec, wspec],
        out_specs=[spec5, spec5, specb, specb],
        out_shape=[out5, out5, outb, outb],
        compiler_params=_params(2),
    )(mem.reshape(n, D_MODEL), ln_mem.reshape(depth, 1, D_MODEL), wk, wv)


def _diff_attn_kernel(lam_ref, q_ref, k_ref, v_ref, bias_ref, subln_ref, o_ref,
                      vt_sc, s_sc, acc_sc, *, tq, nd, hp, out_scale):
    i = pl.program_id(2)
    dv = A_V_DIM
    chains = [(h, c) for h in range(hp) for c in range(2)]
    n_ch = len(chains)

    @pl.when(i == 0)
    def _():
        for h in range(hp):
            for jj in range(nd):
                vt_sc[h, jj, :dv, :] = v_ref[0, jj * tq:(jj + 1) * tq, h * dv:(h + 1) * dv].T
                vt_sc[h, jj, dv:, :] = jnp.ones((ONES_ROWS, tq), BF16)

    row = lax.broadcasted_iota(jnp.int32, (dv, tq), 0)
    qzt = []
    for h in range(hp):
        qt = (q_ref[0, :, h * dv:(h + 1) * dv] * (A_QK_DIM ** -0.5)).T
        for c in range(2):
            qzt.append(jnp.where((row >= A_QK_DIM) == (c == 1), qt, jnp.zeros_like(qt)))

    def raw_scores(j, n):
        h = chains[n][0]
        start = pl.multiple_of(j * tq, tq)
        return _dot(k_ref[0, pl.ds(start, tq), h * dv:(h + 1) * dv], qzt[n])

    kk = lax.broadcasted_iota(jnp.int32, (tq, tq), 0)
    qq = lax.broadcasted_iota(jnp.int32, (tq, tq), 1)
    diag_penalty = jnp.where((kk // CHUNK) <= (qq // CHUNK), 0.0, NEG_INF)

    def pass1(js, ms):
        ms = list(ms)
        items = [(j, n) for j in js for n in range(n_ch)]
        raw_next = raw_scores(*items[0])
        for idx, (j, n) in enumerate(items):
            raw = raw_next
            if idx + 1 < len(items):
                raw_next = raw_scores(*items[idx + 1])
            h, c = chains[n]
            if c == 0:
                bias = bias_ref[h, j - i + (nd - 1)] + diag_penalty * jnp.where(j == i, 1.0, 0.0)
            s = (raw + bias) * LOG2_E
            s_sc[n, j] = s
            ms[n] = jnp.maximum(ms[n], jnp.max(s, axis=0, keepdims=True))
        return tuple(ms)

    def pass2(js):
        items = [(j, n) for j in js for n in range(n_ch)]
        p_next = jnp.exp2(s_sc[items[0][1], items[0][0]] - m_fin[items[0][1]])
        for idx, (j, n) in enumerate(items):
            p = p_next
            if idx + 1 < len(items):
                jn, nn = items[idx + 1]
                p_next = jnp.exp2(s_sc[nn, jn] - m_fin[nn])
            acc_sc[n] += _dot(vt_sc[chains[n][0], j], p.astype(BF16))

    n_tiles = i + 1
    plan = []
    first = 0
    for width in (4, 2, 1):
        trips = (n_tiles - first) // width
        plan.append((width, trips, first))
        first = first + trips * width

    ms = tuple(jnp.full((1, tq), NEG_INF, F32) for _ in range(n_ch))
    for width, trips, start in plan:
        ms = lax.fori_loop(
            0, trips,
            lambda tt, m, width=width, start=start:
                pass1([start + tt * width + u for u in range(width)], m), ms)
    m_fin = ms

    acc_sc[...] = jnp.zeros(acc_sc.shape, F32)
    for width, trips, start in plan:
        def trip2(tt, carry, width=width, start=start):
            pass2([start + tt * width + u for u in range(width)])
            return carry

        lax.fori_loop(0, trips, trip2, 0)
    for h in range(hp):
        acc0 = acc_sc[2 * h]
        acc1 = acc_sc[2 * h + 1]
        out0 = acc0[:dv] / acc0[dv:dv + 1]
        out1 = acc1[:dv] / acc1[dv:dv + 1]
        o = (out0 - lam_ref[0, 0] * out1).T
        o_ref[:, h * dv:(h + 1) * dv] = (_head_rms(o) * subln_ref[...] * out_scale).astype(BF16)


def _diff_attn(lam, z3, bias_tiles, subln, tq, hp, out_scale):
    bsz, t, _ = z3.shape
    nd = t // tq
    width = hp * A_V_DIM
    groups = A_HEADS // hp
    return pl.pallas_call(
        functools.partial(_diff_attn_kernel, tq=tq, nd=nd, hp=hp, out_scale=out_scale),
        grid=(bsz, groups, nd),
        in_specs=[pl.BlockSpec(memory_space=pltpu.SMEM),
                  pl.BlockSpec((1, tq, width), lambda b, g, i: (b, i, g)),
                  pl.BlockSpec((1, t, width), lambda b, g, i: (b, 0, groups + g)),
                  pl.BlockSpec((1, t, width), lambda b, g, i: (b, 0, 2 * groups + g)),
                  pl.BlockSpec((hp, nd, tq, tq), lambda b, g, i: (g, 0, 0, 0),
                               pipeline_mode=pl.Buffered(1)),
                  pl.BlockSpec((1, A_V_DIM), lambda b, g, i: (0, 0))],
        out_specs=pl.BlockSpec((tq, width), lambda b, g, i: (b * nd + i, g)),
        out_shape=jax.ShapeDtypeStruct((bsz * t, A_HEADS * A_V_DIM), BF16),
        scratch_shapes=[pltpu.VMEM((hp, nd, A_V_DIM + ONES_ROWS, tq), BF16),
                        pltpu.VMEM((2 * hp, nd, tq, tq), F32),
                        pltpu.VMEM((2 * hp, A_V_DIM + ONES_ROWS, tq), F32)],
        compiler_params=_params(3),
    )(lam, z3, z3, z3, bias_tiles, subln.reshape(1, A_V_DIM))


def _diff_attn_cached_kernel(lam_ref, q_ref, kn_ref, vn_ref, kp_ref, vp_ref, bp_ref, bn_ref,
                             subln_ref, o_ref, *, t, past, out_scale):
    lane = lax.broadcasted_iota(jnp.int32, (t, A_V_DIM), 1)
    qpos_p = past + lax.broadcasted_iota(jnp.int32, (2 * t, past), 0) % t
    kpos_p = lax.broadcasted_iota(jnp.int32, (2 * t, past), 1)
    ok_p = (kpos_p // CHUNK) <= (qpos_p // CHUNK)
    qpos_n = past + lax.broadcasted_iota(jnp.int32, (2 * t, t), 0) % t
    kpos_n = past + lax.broadcasted_iota(jnp.int32, (2 * t, t), 1)
    ok_n = (kpos_n // CHUNK) <= (qpos_n // CHUNK)
    for h in range(A_HEADS):
        cs = slice(h * A_V_DIM, (h + 1) * A_V_DIM)
        q = q_ref[0, :, cs] * (A_QK_DIM ** -0.5)
        zero = jnp.zeros_like(q)
        qp = jnp.concatenate([jnp.where(lane < A_QK_DIM, q, zero),
                              jnp.where(lane >= A_QK_DIM, q, zero)], axis=0)
        kp = kp_ref[0, :, h, :].astype(BF16)
        vp = vp_ref[0, :, h, :].astype(BF16)
        kn = kn_ref[0, :, cs]
        vn = vn_ref[0, :, cs]
        bp = bp_ref[h]
        bn = bn_ref[h]
        sp = jnp.where(ok_p, _dot_nt(qp, kp) + jnp.concatenate([bp, bp], axis=0), NEG_INF)
        sn = jnp.where(ok_n, _dot_nt(qp, kn) + jnp.concatenate([bn, bn], axis=0), NEG_INF)
        m = jnp.maximum(jnp.max(sp, axis=-1, keepdims=True), jnp.max(sn, axis=-1, keepdims=True))
        pp = jnp.exp(sp - m)
        pn = jnp.exp(sn - m)
        l = jnp.sum(pp, axis=-1, keepdims=True) + jnp.sum(pn, axis=-1, keepdims=True)
        out = (_dot(pp.astype(BF16), vp) + _dot(pn.astype(BF16), vn)) / l
        o = out[:t] - lam_ref[0, 0] * out[t:]
        o_ref[:, cs] = (_head_rms(o) * subln_ref[...] * out_scale).astype(BF16)


def _diff_attn_cached(lam, z3, past_k, past_v, bias_past, bias_new, subln, out_scale):
    bsz, t, _ = z3.shape
    past = past_k.shape[1]
    width = A_HEADS * A_V_DIM
    cache_spec = pl.BlockSpec((1, past, A_HEADS, A_V_DIM), lambda b: (b, 0, 0, 0))
    return pl.pallas_call(
        functools.partial(_diff_attn_cached_kernel, t=t, past=past, out_scale=out_scale),
        grid=(bsz,),
        in_specs=[pl.BlockSpec(memory_space=pltpu.SMEM),
                  pl.BlockSpec((1, t, width), lambda b: (b, 0, 0)),
                  pl.BlockSpec((1, t, width), lambda b: (b, 0, 1)),
                  pl.BlockSpec((1, t, width), lambda b: (b, 0, 2)),
                  cache_spec, cache_spec,
                  pl.BlockSpec((A_HEADS, t, past), lambda b: (0, 0, 0)),
                  pl.BlockSpec((A_HEADS, t, t), lambda b: (0, 0, 0)),
                  pl.BlockSpec((1, A_V_DIM), lambda b: (0, 0))],
        out_specs=pl.BlockSpec((t, width), lambda b: (b, 0)),
        out_shape=jax.ShapeDtypeStruct((bsz * t, width), BF16),
        compiler_params=_params(1),
    )(lam, z3, z3, z3, past_k, past_v, bias_past, bias_new, subln.reshape(1, A_V_DIM))


def _retention_kernel(q_ref, k_ref, v_ref, gt_ref, cos_ref, sin_ref, dec_ref, qd_ref, kd_ref,
                      cd_ref, s0_ref, o_ref, s_out_ref, s_sc):
    t = pl.program_id(1)

    @pl.when(t == 0)
    def _():
        s_sc[...] = s0_ref[0]

    d = B_QK_DIM
    half = d // 2
    c = dec_ref.shape[1]
    for ci in range(q_ref.shape[1] // c):
        rs = slice(ci * c, (ci + 1) * c)
        cos = cos_ref[rs, :]
        sin = sin_ref[rs, :]
        for h in range(B_HEADS):
            cs = slice(h * d, (h + 1) * d)
            q = q_ref[0, rs, cs].astype(F32)
            k = k_ref[0, rs, cs].astype(F32)
            qr = (q * cos + pltpu.roll(q, half, 1) * sin) * (d ** -0.5)
            kr = k * cos + pltpu.roll(k, half, 1) * sin
            v = v_ref[0, rs, cs]
            att = _dot_nt(qr.astype(BF16), kr.astype(BF16)) * dec_ref[h]
            s = s_sc[h]
            o = _dot(att.astype(BF16), v) + _dot((qr * qd_ref[h]).astype(BF16), s.astype(BF16))
            s_sc[h] = s * cd_ref[h] + _dot_tn((kr * kd_ref[h]).astype(BF16), v)
            gt = gt_ref[0, rs, cs].astype(F32)
            o_ref[rs, cs] = (_head_rms(o) * _silu(gt)).astype(BF16)

    @pl.when(t == pl.num_programs(1) - 1)
    def _():
        s_out_ref[0] = s_sc[...]


def _retention(z3, cos, sin, consts, s0, c, tb):
    bsz, t, _ = z3.shape
    nt = t // tb
    dec, qd, kd, cd = consts
    d = B_QK_DIM
    width = B_HEADS * d
    base = 3
    full3 = lambda b, i: (0, 0, 0)
    return pl.pallas_call(
        _retention_kernel,
        grid=(bsz, nt),
        in_specs=[pl.BlockSpec((1, tb, width), lambda b, i: (b, i, base)),
                  pl.BlockSpec((1, tb, width), lambda b, i: (b, i, base + 1)),
                  pl.BlockSpec((1, tb, width), lambda b, i: (b, i, base + 2)),
                  pl.BlockSpec((1, tb, width), lambda b, i: (b, i, base + 3)),
                  pl.BlockSpec((tb, d), lambda b, i: (i, 0)),
                  pl.BlockSpec((tb, d), lambda b, i: (i, 0)),
                  pl.BlockSpec((B_HEADS, c, c), full3),
                  pl.BlockSpec((B_HEADS, c, d), full3),
                  pl.BlockSpec((B_HEADS, c, d), full3),
                  pl.BlockSpec((B_HEADS, 1, d), full3),
                  pl.BlockSpec((1, B_HEADS, d, d), lambda b, i: (b, 0, 0, 0))],
        out_specs=[pl.BlockSpec((tb, width), lambda b, i: (b * nt + i, 0)),
                   pl.BlockSpec((1, B_HEADS, d, d), lambda b, i: (b, 0, 0, 0))],
        out_shape=[jax.ShapeDtypeStruct((bsz * t, width), BF16),
                   jax.ShapeDtypeStruct((bsz, B_HEADS, d, d), F32)],
        scratch_shapes=[pltpu.VMEM((B_HEADS, d, d), F32)],
        compiler_params=_params(2),
    )(z3, z3, z3, z3, cos, sin, dec, qd, kd, cd, s0)


def _retention_consts(c):
    log_g = jnp.log1p(-jnp.exp2(-5.0 - jnp.arange(B_HEADS, dtype=F32)))
    idx = jnp.arange(c, dtype=F32)
    dist = idx[:, None] - idx[None, :]
    dec = jnp.where(dist >= 0, jnp.exp(jnp.maximum(dist, 0.0)[None] * log_g[:, None, None]), 0.0)
    qd = jnp.exp((idx[None, :] + 1.0) * log_g[:, None])
    kd = jnp.exp((c - 1.0 - idx)[None, :] * log_g[:, None])
    cd = jnp.exp(c * log_g)
    bc = lambda a: jnp.broadcast_to(a[..., None], a.shape + (B_QK_DIM,))
    return dec, bc(qd), bc(kd), bc(cd[:, None])


def _rotary_tables(pos):
    half = B_QK_DIM // 2
    inv = ROPE_BASE ** (-jnp.arange(half, dtype=F32) / half)
    ang = pos.astype(F32)[:, None] * inv[None, :]
    cos, sin = jnp.cos(ang), jnp.sin(ang)
    return jnp.concatenate([cos, cos], axis=-1), jnp.concatenate([-sin, sin], axis=-1)


def _gla_levels(c):
    return [c >> (l + 1) for l in range(int(math.log2(c)))]


def _gla_consts(c):
    levels = _gla_levels(c)
    rows = np.arange(c)
    mats = []
    for s in levels:
        ref = (rows // (2 * s)) * 2 * s + s - 1
        a = np.zeros((c, c), np.float32)
        for i in range(c):
            if i & s:
                a[i, ref[i] + 1:i + 1] = 1.0
            else:
                a[i, i + 1:ref[i] + 1] = 1.0
        mats.append(a)
    mats.append(np.tril(np.ones((c, c), np.float32)))
    mats.append(np.triu(np.ones((c, c), np.float32), 1))
    lv = np.full((c, c), -1, np.int32)
    for i in range(c):
        lv[i, i] = len(levels)
        for j in range(i):
            lv[i, j] = levels.index(1 << int(math.floor(math.log2(i ^ j))))
    return jnp.asarray(np.concatenate(mats, axis=0), BF16), jnp.asarray(lv)


def _gla_kernel(q_ref, k_ref, v_ref, r_ref, g_ref, s0_ref, a_ref, lv_ref, nw_ref,
                o_ref, s_out_ref, st_sc, *, c, n_chunks):
    t = pl.program_id(1)
    dk, dv = C_QK_DIM, C_V_DIM

    @pl.when(t == 0)
    def _():
        for h in range(C_HEADS):
            st_sc[h] = s0_ref[0, h].T

    levels = _gla_levels(c)
    n_lv = len(levels)
    half_blocks = (c // 2) % LANES == 0
    lv = lv_ref[...]
    row = lax.broadcasted_iota(jnp.int32, (c, dk), 0)

    def gate_sums(ci):
        return _dot(a_ref[...], g_ref[0, ci * c:(ci + 1) * c, :].astype(BF16))

    xs_next = gate_sums(0)
    for ci, h in [(ci, h) for ci in range(n_chunks) for h in range(C_HEADS)]:
        rs = slice(ci * c, (ci + 1) * c)
        ks = slice(h * dk, (h + 1) * dk)
        vs = slice(h * dv, (h + 1) * dv)
        if h == 0:
            xs = xs_next
        if h == 1 and ci + 1 < n_chunks:
            xs_next = gate_sums(ci + 1)
        x = xs[:, ks]
        q = q_ref[0, rs, ks].astype(F32) * (dk ** -0.5)
        k = k_ref[0, rs, ks].astype(F32)
        v = v_ref[0, rs, vs]
        if half_blocks:
            hc = c // 2
            lvh = lv[:hc, :hc]
            diag = [jnp.zeros((hc, hc), F32), jnp.zeros((hc, hc), F32)]
            qb, kb = q.astype(BF16), k.astype(BF16)
            pending = (n_lv, [_dot_nt(qb[:hc], kb[:hc]), _dot_nt(qb[hc:], kb[hc:])])
        else:
            att = jnp.zeros((c, c), F32)
            pending = (n_lv, _dot_nt(q.astype(BF16), k.astype(BF16)))
        for l, s in enumerate(levels):
            e = jnp.exp(x[l * c:(l + 1) * c])
            up = (row & s) != 0
            mix = jnp.where(up, q, k) * e
            qt = jnp.where(up, mix, 0.0).astype(BF16)
            kt = jnp.where(up, 0.0, mix).astype(BF16)
            if not half_blocks:
                prod = _dot_nt(qt, kt)
                att = jnp.where(lv == pending[0], pending[1], att)
            elif l == 0:
                lower_left = _dot_nt(qt[hc:], kt[:hc])
                continue
            else:
                prod = [_dot_nt(qt[:hc], kt[:hc]), _dot_nt(qt[hc:], kt[hc:])]
                diag = [jnp.where(lvh == pending[0], pending[1][u], diag[u]) for u in range(2)]
            pending = (l, prod)
        if half_blocks:
            diag = [jnp.where(lvh == pending[0], pending[1][u], diag[u]) for u in range(2)]
            att = jnp.concatenate(
                [jnp.concatenate([diag[0], jnp.zeros((hc, hc), F32)], axis=1),
                 jnp.concatenate([lower_left, diag[1]], axis=1)], axis=0)
        else:
            att = jnp.where(lv == pending[0], pending[1], att)
        b = x[n_lv * c:(n_lv + 1) * c]
        rem = x[(n_lv + 1) * c:(n_lv + 2) * c]
        st = st_sc[h]
        o = _dot(att.astype(BF16), v) + _dot_nt((q * jnp.exp(b)).astype(BF16), st.astype(BF16))
        kd = (k * jnp.exp(rem)).astype(BF16)
        st_sc[h] = st * jnp.exp(b[c - 1:c, :]) + _dot_tn(v, kd)
        r = r_ref[0, rs, vs].astype(F32)
        o_ref[rs, vs] = (_head_rms(o) * nw_ref[...] * _silu(r)).astype(BF16)

    @pl.when(t == pl.num_programs(1) - 1)
    def _():
        for h in range(C_HEADS):
            s_out_ref[0, h] = st_sc[h].T


def _gla(z3, log_a3, s0, a_stack, lv, norm_w, c, tb):
    bsz, t, _ = z3.shape
    nt = t // tb
    dk, dv = C_QK_DIM, C_V_DIM
    qk_w, v_w = C_HEADS * dk, C_HEADS * dv
    const = lambda b, i: (0, 0)
    return pl.pallas_call(
        functools.partial(_gla_kernel, c=c, n_chunks=tb // c),
        grid=(bsz, nt),
        in_specs=[pl.BlockSpec((1, tb, qk_w), lambda b, i: (b, i, 0)),
                  pl.BlockSpec((1, tb, qk_w), lambda b, i: (b, i, 1)),
                  pl.BlockSpec((1, tb, v_w), lambda b, i: (b, i, 1)),
                  pl.BlockSpec((1, tb, v_w), lambda b, i: (b, i, 2)),
                  pl.BlockSpec((1, tb, qk_w), lambda b, i: (b, i, 0)),
                  pl.BlockSpec((1, C_HEADS, dk, dv), lambda b, i: (b, 0, 0, 0)),
                  pl.BlockSpec(a_stack.shape, const),
                  pl.BlockSpec((c, c), const),
                  pl.BlockSpec((1, dv), const)],
        out_specs=[pl.BlockSpec((tb, v_w), lambda b, i: (b * nt + i, 0)),
                   pl.BlockSpec((1, C_HEADS, dk, dv), lambda b, i: (b, 0, 0, 0))],
        out_shape=[jax.ShapeDtypeStruct((bsz * t, v_w), BF16),
                   jax.ShapeDtypeStruct((bsz, C_HEADS, dk, dv), F32)],
        scratch_shapes=[pltpu.VMEM((C_HEADS, dv, dk), F32)],
        compiler_params=_params(2),
    )(z3, z3, z3, z3, log_a3, s0, a_stack, lv, norm_w.reshape(1, dv))


def _cross_kernel(*refs, n_mix, n_sub):
    x_ref = refs[0]
    a_refs = refs[1:1 + n_mix]
    w_refs = refs[1 + n_mix:1 + 2 * n_mix]
    g_ref, wq_ref, mk_ref, mv_ref, wo_ref, o_ref = refs[1 + 2 * n_mix:]
    def head(ref, hd):
        if len(ref.shape) == 5:
            return ref[0, 0, :, hd, :].astype(BF16)
        return ref[0, 0, :, hd * M_HEAD_DIM:(hd + 1) * M_HEAD_DIM].astype(BF16)

    tr = x_ref.shape[0] // n_sub
    xs = []
    for sub in range(n_sub):
        rows = slice(sub * tr, (sub + 1) * tr)
        x = x_ref[rows, :]
        for a_ref, w_ref in zip(a_refs, w_refs):
            x = x + _dot(a_ref[rows, :], w_ref[...])
        xs.append(x)
    for sub in range(n_sub):
        x = xs[sub]
        h = _rms(x, g_ref[...]).astype(BF16)
        q = (_dot(h, wq_ref[0]) * (M_HEAD_DIM ** -0.5)).astype(BF16)
        outs = []
        for hd in range(M_HEADS):
            cs = slice(hd * M_HEAD_DIM, (hd + 1) * M_HEAD_DIM)
            s = _dot_nt(q[:, cs], head(mk_ref, hd))
            p = jnp.exp(s - jnp.max(s, axis=-1, keepdims=True))
            l = jnp.sum(p, axis=-1, keepdims=True)
            outs.append((_dot(p.astype(BF16), head(mv_ref, hd)) / l).astype(BF16))
        o_ref[sub * tr:(sub + 1) * tr, :] = x + _dot(jnp.concatenate(outs, axis=1), wo_ref[0])


def _cross(x2d, mix_list, w_mix_list, gamma, wq, mk, mv, wo, layer, t, tm):
    n = x2d.shape[0]
    nt = t // tm
    width = M_HEADS * M_HEAD_DIM
    rows = lambda b, i: (b * nt + i, 0)
    const = lambda b, i: (0, 0)
    of_layer = lambda b, i: (layer, 0, 0)
    mem_spec = pl.BlockSpec((1, 1) + mk.shape[2:], lambda b, i: (layer, b) + (0,) * (mk.ndim - 2))
    in_specs = [pl.BlockSpec((tm, D_MODEL), rows)]
    in_specs += [pl.BlockSpec((tm, a.shape[1]), rows) for a in mix_list]
    in_specs += [pl.BlockSpec(w.shape, const) for w in w_mix_list]
    in_specs += [pl.BlockSpec((1, D_MODEL), const),
                 pl.BlockSpec((1, D_MODEL, width), of_layer),
                 mem_spec, mem_spec,
                 pl.BlockSpec((1, width, D_MODEL), of_layer)]
    return pl.pallas_call(
        functools.partial(_cross_kernel, n_mix=len(mix_list), n_sub=max(1, tm // CROSS_SUB_ROWS)),
        grid=(n // t, nt),
        in_specs=in_specs,
        out_specs=pl.BlockSpec((tm, D_MODEL), rows),
        out_shape=jax.ShapeDtypeStruct((n, D_MODEL), F32),
        compiler_params=_params(2),
    )(x2d, *mix_list, *w_mix_list, gamma.reshape(1, D_MODEL), wq, mk, mv, wo)


def _ffn_kernel(x_ref, g_ref, wg_ref, wu_ref, cw_ref, cb_ref, prev_ref, wd_ref, *rest,
                tm, n_sub, final_norm):
    if final_norm:
        gf_ref, o_ref, tail_ref, carry = rest
    else:
        o_ref, tail_ref, carry = rest
    t = pl.program_id(1)
    lo = SUBLANES - 2

    @pl.when(t == 0)
    def _():
        carry[lo:SUBLANES, :] = prev_ref[0]

    row = lax.broadcasted_iota(jnp.int32, (SUBLANES, FF_CHUNK), 0)
    n_chunks = D_FF // FF_CHUNK
    chunk = lambda c: slice(c * FF_CHUNK, (c + 1) * FF_CHUNK)
    tr = tm // n_sub
    h_next = _rms(x_ref[0:tr, :], g_ref[...]).astype(BF16)
    for sub in range(n_sub):
        rows = slice(sub * tr, (sub + 1) * tr)
        h = h_next
        acc = x_ref[rows, :]
        nxt = (_dot(h, wg_ref[0, :, chunk(0)]), _dot(h, wu_ref[0, :, chunk(0)]))
        pending = []
        for c in range(n_chunks):
            cs = chunk(c)
            gate, up = nxt
            if c + 1 < n_chunks:
                nxt = (_dot(h, wg_ref[0, :, chunk(c + 1)]), _dot(h, wu_ref[0, :, chunk(c + 1)]))
            if c == n_chunks // 2 and sub + 1 < n_sub:
                h_next = _rms(x_ref[(sub + 1) * tr:(sub + 2) * tr, :], g_ref[...]).astype(BF16)
            p1 = carry[SUBLANES - 1:SUBLANES, cs]
            p2 = carry[lo:lo + 1, cs]
            r1 = pltpu.roll(gate, 1, 0)
            r2 = pltpu.roll(gate, 2, 0)
            h1 = jnp.where(row == 0, p1, r1[:SUBLANES])
            h2 = jnp.where(row == 0, p2, jnp.where(row == 1, p1, r2[:SUBLANES]))
            g1 = jnp.concatenate([h1, r1[SUBLANES:]], axis=0)
            g2 = jnp.concatenate([h2, r2[SUBLANES:]], axis=0)
            carry[:, cs] = gate[tr - SUBLANES:tr]
            conv = cb_ref[:, cs] + cw_ref[0:1, cs] * g2
            conv = conv + cw_ref[1:2, cs] * g1
            conv = conv + cw_ref[2:3, cs] * gate
            pending.append((_gelu_tanh(conv) * up).astype(BF16))
            if len(pending) == DOWN_GROUP or c == n_chunks - 1:
                lo_col = (c + 1 - len(pending)) * FF_CHUNK
                acc = acc + _dot(jnp.concatenate(pending, axis=1),
                                 wd_ref[0, lo_col:(c + 1) * FF_CHUNK, :])
                pending = []
        if final_norm:
            o_ref[rows, :] = _rms(acc, gf_ref[...])
        else:
            o_ref[rows, :] = acc

    @pl.when(t == pl.num_programs(1) - 1)
    def _():
        tail_ref[0] = carry[lo:SUBLANES, :]


def _ffn(x2d, gamma, wg, wu, conv_w, conv_b, prev, wd, layer, t, tm, final_gamma=None):
    n = x2d.shape[0]
    nt = t // tm
    bsz = n // t
    const = lambda b, i: (0, 0)
    of_layer = lambda b, i: (layer, 0, 0)
    resident = dict(pipeline_mode=pl.Buffered(1))
    in_specs = [pl.BlockSpec((tm, D_MODEL), lambda b, i: (b * nt + i, 0)),
                pl.BlockSpec((1, D_MODEL), const),
                pl.BlockSpec((1, D_MODEL, D_FF), of_layer, **resident),
                pl.BlockSpec((1, D_MODEL, D_FF), of_layer, **resident),
                pl.BlockSpec((3, D_FF), const),
                pl.BlockSpec((1, D_FF), const),
                pl.BlockSpec((1, 2, D_FF), lambda b, i: (b, 0, 0)),
                pl.BlockSpec((1, D_FF, D_MODEL), of_layer, **resident)]
    args = [x2d, gamma.reshape(1, D_MODEL), wg, wu, conv_w, conv_b.reshape(1, D_FF), prev, wd]
    if final_gamma is not None:
        in_specs.append(pl.BlockSpec((1, D_MODEL), const))
        args.append(final_gamma.reshape(1, D_MODEL))
    return pl.pallas_call(
        functools.partial(_ffn_kernel, tm=tm, n_sub=max(1, tm // FFN_SUB_ROWS),
                          final_norm=final_gamma is not None),
        grid=(bsz, nt),
        in_specs=in_specs,
        out_specs=[pl.BlockSpec((tm, D_MODEL), lambda b, i: (b * nt + i, 0)),
                   pl.BlockSpec((1, 2, D_FF), lambda b, i: (b, 0, 0))],
        out_shape=[jax.ShapeDtypeStruct((n, D_MODEL), F32),
                   jax.ShapeDtypeStruct((bsz, 2, D_FF), F32)],
        scratch_shapes=[pltpu.VMEM((SUBLANES, D_FF), F32)],
        compiler_params=_params(2),
    )(*args)


def _rel_bucket(rel):
    nb = REL_BUCKETS // 2
    max_exact = nb // 2
    n = jnp.abs(rel)
    nf = jnp.maximum(n, 1).astype(F32)
    large = max_exact + (jnp.log(nf / max_exact) / math.log(REL_MAX_DIST / max_exact)
                         * (nb - max_exact)).astype(jnp.int32)
    large = jnp.minimum(large, nb - 1)
    return jnp.where(rel > 0, nb, 0) + jnp.where(n < max_exact, n, large)


def _toeplitz_kernel(w_ref, o_ref, *, rows, cols):
    w = w_ref[0]
    x = jnp.broadcast_to(w, (rows, w.shape[1]))
    o_ref[0] = pltpu.roll(x, 0, 1, stride=1, stride_axis=0)[:, :cols]


def _toeplitz(w, rows, cols):
    groups, period = w.shape
    return pl.pallas_call(
        functools.partial(_toeplitz_kernel, rows=rows, cols=cols),
        grid=(groups,),
        in_specs=[pl.BlockSpec((1, 1, period), lambda g: (g, 0, 0))],
        out_specs=pl.BlockSpec((1, rows, cols), lambda g: (g, 0, 0)),
        out_shape=jax.ShapeDtypeStruct((groups, rows, cols), F32),
        compiler_params=_params(1),
    )(w.reshape(groups, 1, period))


def _bias_tiles(rel_bias, tq, nd):
    period = 2 * tq
    n = np.arange(period)
    rel = np.stack([np.where(n < tq, (d - (nd - 1)) * tq - n, (d - (nd - 1)) * tq + period - n)
                    for d in range(nd)]).astype(np.int32)
    w = jnp.transpose(rel_bias[_rel_bucket(jnp.asarray(rel))], (2, 0, 1)).astype(F32)
    return _toeplitz(w.reshape(A_HEADS * nd, period), tq, tq).reshape(A_HEADS, nd, tq, tq)


def _bias_rows(rel_bias, t, past):
    n_keys = past + t
    period = -(-(n_keys + t) // LANES) * LANES
    m = np.arange(period)
    rel = np.where(m < n_keys, m - past, m - period - past).astype(np.int32)
    w = jnp.transpose(rel_bias[_rel_bucket(jnp.asarray(rel))], (1, 0)).astype(F32)
    bias = _toeplitz(w, t, n_keys)
    return bias[:, :, :past], bias[:, :, past:]


def _forward(x, past_k, past_v, ret_state, gla_state, conv_prev, mem_k, mem_v, p, w):
    bsz, t, _ = x.shape
    n = bsz * t
    past = 0 if past_k is None else past_k.shape[2]
    tm = min(2 * PROJ_SUB_ROWS, n)
    tseq = min(2 * CROSS_SUB_ROWS, t)
    tffn = min(2 * FFN_SUB_ROWS, t)
    tq = min(256, t)
    c_ret = min(256, t)
    c_gla = min(256, t)
    x2 = x.reshape(n, D_MODEL)
    pos = past + jnp.arange(t, dtype=jnp.int32)
    new_conv = []

    z, ak, av = _in_even(x2, p["ln_mix"][0], w["w_in_even"], tm)
    z3 = z.reshape(bsz, t, EVEN_IN)
    lam_init = 0.8 - 0.6 * math.exp(-0.3 * 0)
    lam = (jnp.exp(jnp.sum(p["diff_lq1"][0].astype(F32) * p["diff_lk1"][0].astype(F32)))
           - jnp.exp(jnp.sum(p["diff_lq2"][0].astype(F32) * p["diff_lk2"][0].astype(F32)))
           + lam_init).reshape(1, 1).astype(F32)
    if past == 0:
        o_a = _diff_attn(lam, z3, _bias_tiles(p["rel_bias"], tq, t // tq), p["diff_subln"][0],
                         tq, A_HEADS, 1.0 - lam_init)
    else:
        bias_past, bias_new = _bias_rows(p["rel_bias"], t, past)
        o_a = _diff_attn_cached(lam, z3, past_k[0], past_v[0], bias_past, bias_new,
                                p["diff_subln"][0], 1.0 - lam_init)
    cos, sin = _rotary_tables(pos)
    o_b, s_ret = _retention(z3, cos, sin, _retention_consts(c_ret), ret_state[0], c_ret,
                            min(2 * c_ret, t))
    x2 = _cross(x2, [o_a, o_b], [w["w_out_even_a"], w["w_out_even_b"]], p["ln_cross"][0],
                w["w_cq"], mem_k, mem_v, w["w_co"], 0, t, tseq)
    x2, tail = _ffn(x2, p["ln_ffn"][0], w["w_ffn_gate"], w["w_ffn_up"], p["ffn_conv_w"][0],
                    p["ffn_conv_b"][0], conv_prev[0], w["w_ffn_down"], 0, t, tffn)
    new_conv.append(tail)

    zc, log_a = _in_odd(x2, p["ln_mix"][1], w["w_in_odd"], w["w_gate_lr"], p["b_gate"][0], tm)
    a_stack, lv = _gla_consts(c_gla)
    o_c, s_gla = _gla(zc.reshape(bsz, t, ODD_Z), log_a.reshape(bsz, t, C_HEADS * C_QK_DIM),
                      gla_state[0], a_stack, lv, p["gla_norm"][0], c_gla, min(2 * c_gla, t))
    x2 = _cross(x2, [o_c], [w["w_out_odd"]], p["ln_cross"][1],
                w["w_cq"], mem_k, mem_v, w["w_co"], 1, t, tseq)
    y, tail = _ffn(x2, p["ln_ffn"][1], w["w_ffn_gate"], w["w_ffn_up"], p["ffn_conv_w"][1],
                   p["ffn_conv_b"][1], conv_prev[1], w["w_ffn_down"], 1, t, tffn,
                   final_gamma=p["ln_final"])
    new_conv.append(tail)

    return (y.reshape(bsz, t, D_MODEL),
            ak.reshape(1, bsz, t, A_HEADS, A_V_DIM), av.reshape(1, bsz, t, A_HEADS, A_V_DIM),
            s_ret[None], s_gla[None], jnp.stack(new_conv))


def kernel(x_prompt, x_sample, cache_diff_k, cache_diff_v, state_retention, state_gla, cache_ffn_conv, cache_mem_k, cache_mem_v, mem_prompt, ln_mix, ln_cross, ln_ffn, ln_mem, ln_final, w_in_even, w_out_even, diff_lq1, diff_lk1, diff_lq2, diff_lk2, diff_subln, rel_bias, w_in_odd, w_gate_lr, b_gate, gla_norm, w_out_odd, w_cq, w_ck, w_cv, w_co, w_ffn_gate, w_ffn_up, ffn_conv_w, ffn_conv_b, w_ffn_down):
    p = dict(ln_mix=ln_mix, ln_cross=ln_cross, ln_ffn=ln_ffn, ln_final=ln_final,
             diff_lq1=diff_lq1, diff_lk1=diff_lk1, diff_lq2=diff_lq2, diff_lk2=diff_lk2,
             diff_subln=diff_subln, rel_bias=rel_bias, b_gate=b_gate, gla_norm=gla_norm,
             ffn_conv_w=ffn_conv_w, ffn_conv_b=ffn_conv_b)
    a_v = A_HEADS * A_V_DIM
    w = dict(
        w_in_even=w_in_even[0].astype(BF16),
        w_out_even_a=w_out_even[0, :a_v].astype(BF16),
        w_out_even_b=w_out_even[0, a_v:].astype(BF16),
        w_in_odd=jnp.pad(w_in_odd[0], ((0, 0), (0, ODD_IN_PAD - w_in_odd.shape[2]))).astype(BF16),
        w_gate_lr=jnp.pad(w_gate_lr[0], ((0, LANES - C_GATE_RANK), (0, 0))).astype(BF16),
        w_out_odd=w_out_odd[0].astype(BF16),
        w_cq=w_cq.astype(BF16), w_co=w_co.astype(BF16),
        w_ffn_gate=w_ffn_gate.astype(BF16), w_ffn_up=w_ffn_up.astype(BF16),
        w_ffn_down=w_ffn_down.astype(BF16))

    bp, mem_len, _ = mem_prompt.shape
    m_width = M_HEADS * M_HEAD_DIM
    depth = ln_mem.shape[0]
    mem_k_p, mem_v_p, mk_b, mv_b = _mem_kv(mem_prompt, ln_mem, w_ck.astype(BF16),
                                           w_cv.astype(BF16), seqs=2)

    dt = x_prompt.dtype
    zero_ret = jnp.zeros((1, bp, B_HEADS, B_QK_DIM, B_QK_DIM), dt)
    zero_gla = jnp.zeros((1, bp, C_HEADS, C_QK_DIM, C_V_DIM), dt)
    zero_conv = jnp.zeros((depth, bp, 2, D_FF), dt)
    y_p, dk_p, dv_p, ret_p, gla_p, conv_p = _forward(
        x_prompt, None, None, zero_ret, zero_gla, zero_conv,
        mk_b.reshape(depth, bp, mem_len, m_width), mv_b.reshape(depth, bp, mem_len, m_width), p, w)
    y_s, dk_s, dv_s, ret_s, gla_s, conv_s = _forward(
        x_sample, cache_diff_k, cache_diff_v, state_retention, state_gla, cache_ffn_conv,
        cache_mem_k, cache_mem_v, p, w)
    return (y_p, y_s, dk_p, dv_p, ret_p, gla_p, conv_p, mem_k_p, mem_v_p,
            dk_s, dv_s, ret_s, gla_s, conv_s)
```

```python
import functools
import math

import numpy as np
import jax
import jax.numpy as jnp
from jax import lax
from jax.experimental import pallas as pl
from jax.experimental.pallas import tpu as pltpu

F32 = jnp.float32
BF16 = jnp.bfloat16

D_MODEL = 1024
CHUNK = 64
A_HEADS = 4
A_QK_DIM = 64
A_V_DIM = 128
B_HEADS = 4
B_QK_DIM = 128
C_HEADS = 4
C_QK_DIM = 128
C_V_DIM = 256
C_GATE_RANK = 16
C_GATE_TAU = 16.0
M_HEADS = 4
M_HEAD_DIM = 256
REL_BUCKETS = 32
REL_MAX_DIST = 128
D_FF = 2816
ROPE_BASE = 10000.0
EPS = 1e-6
NEG_INF = -1e30

EVEN_IN = 3584
ODD_Z = 3072
LANES = 128
SUBLANES = 8
ODD_IN_PAD = ODD_Z + LANES
FF_CHUNK = 256
DOWN_GROUP = 6
FFN_SUB_ROWS = 256
CROSS_SUB_ROWS = 512
PROJ_SUB_ROWS = 512
PROJ_CHUNK = 512
ATTN_TILE = 256
RECURRENT_CHUNK = 256
ONES_ROWS = 16
LOG2_E = math.log2(math.e)
V7X_VMEM_LIMIT_BYTES = 56 * 1024 * 1024


def _params(n_axes):
    return pltpu.CompilerParams(dimension_semantics=("arbitrary",) * n_axes,
                                vmem_limit_bytes=V7X_VMEM_LIMIT_BYTES)


def _dot(a, b):
    return jnp.dot(a, b, preferred_element_type=F32)


def _dot_nt(a, b):
    return lax.dot_general(a, b, (((1,), (1,)), ((), ())), preferred_element_type=F32)


def _dot_tn(a, b):
    return lax.dot_general(a, b, (((0,), (0,)), ((), ())), preferred_element_type=F32)


def _rms(x, g):
    return x * lax.rsqrt(jnp.mean(x * x, axis=-1, keepdims=True) + EPS) * g


def _head_rms(x):
    return x * lax.rsqrt(jnp.mean(x * x, axis=-1, keepdims=True) + EPS)


def _silu(x):
    return x * (1.0 / (1.0 + jnp.exp(-x)))


def _gelu_tanh(x):
    c0 = math.sqrt(2.0 / math.pi)
    return x * (0.5 + 0.5 * jnp.tanh(x * (c0 + (c0 * 0.044715) * (x * x))))


def _log_sigmoid(x):
    return jnp.minimum(x, 0.0) - jnp.log(1.0 + jnp.exp(-jnp.abs(x)))


def _store_head_major(o_ref, val, tm, n_heads, head_dim):
    tiles = head_dim // LANES
    group = n_heads * tiles
    for hd in range(n_heads):
        for part in range(tiles):
            col = hd * head_dim + part * LANES
            o_ref[pl.ds(part * n_heads + hd, tm, stride=group), :] = val[:, col:col + LANES]


def _in_even_kernel(x_ref, g_ref, w_ref, z_ref, k_ref, v_ref, *, n_sub):
    width = A_HEADS * A_V_DIM
    tr = x_ref.shape[0] // n_sub
    h_next = _rms(x_ref[0:tr, :], g_ref[...]).astype(BF16)
    for sub in range(n_sub):
        rows = slice(sub * tr, (sub + 1) * tr)
        h = h_next
        for c in range(EVEN_IN // width):
            zc = _dot(h, w_ref[:, c * width:(c + 1) * width])
            z_ref[rows, c * width:(c + 1) * width] = zc.astype(BF16)
            head_rows = slice(sub * tr * A_HEADS, (sub + 1) * tr * A_HEADS)
            if c == 0 and sub + 1 < n_sub:
                h_next = _rms(x_ref[(sub + 1) * tr:(sub + 2) * tr, :], g_ref[...]).astype(BF16)
            if c == 1:
                _store_head_major(k_ref.at[head_rows, :], zc, tr, A_HEADS, A_V_DIM)
            if c == 2:
                _store_head_major(v_ref.at[head_rows, :], zc, tr, A_HEADS, A_V_DIM)


def _in_even(x2d, gamma, w, tm):
    n = x2d.shape[0]
    return pl.pallas_call(
        functools.partial(_in_even_kernel, n_sub=max(1, tm // PROJ_SUB_ROWS)),
        grid=(n // tm,),
        in_specs=[pl.BlockSpec((tm, D_MODEL), lambda i: (i, 0)),
                  pl.BlockSpec((1, D_MODEL), lambda i: (0, 0)),
                  pl.BlockSpec((D_MODEL, EVEN_IN), lambda i: (0, 0), pipeline_mode=pl.Buffered(1))],
        out_specs=[pl.BlockSpec((tm, EVEN_IN), lambda i: (i, 0)),
                   pl.BlockSpec((tm * A_HEADS, A_V_DIM), lambda i: (i, 0)),
                   pl.BlockSpec((tm * A_HEADS, A_V_DIM), lambda i: (i, 0))],
        out_shape=[jax.ShapeDtypeStruct((n, EVEN_IN), BF16),
                   jax.ShapeDtypeStruct((n * A_HEADS, A_V_DIM), F32),
                   jax.ShapeDtypeStruct((n * A_HEADS, A_V_DIM), F32)],
        compiler_params=_params(1),
    )(x2d, gamma.reshape(1, D_MODEL), w)


def _in_odd_kernel(x_ref, g_ref, w_ref, wlr_ref, bg_ref, z_ref, la_ref, *, n_sub):
    width = PROJ_CHUNK
    tr = x_ref.shape[0] // n_sub

    def norm_and_gates(sub):
        rows = slice(sub * tr, (sub + 1) * tr)
        h = _rms(x_ref[rows, :], g_ref[...]).astype(BF16)
        ca = _dot(h, w_ref[:, ODD_Z:ODD_IN_PAD]).astype(BF16)
        pre = _dot(ca, wlr_ref[...]) + bg_ref[...]
        la_ref[rows, :] = _log_sigmoid(pre) / C_GATE_TAU
        return h

    h_next = norm_and_gates(0)
    for sub in range(n_sub):
        rows = slice(sub * tr, (sub + 1) * tr)
        h = h_next
        for c in range(ODD_Z // width):
            z_ref[rows, c * width:(c + 1) * width] = _dot(
                h, w_ref[:, c * width:(c + 1) * width]).astype(BF16)
            if c == 0 and sub + 1 < n_sub:
                h_next = norm_and_gates(sub + 1)


def _in_odd(x2d, gamma, w_pad, wlr_pad, b_gate, tm):
    n = x2d.shape[0]
    qk = C_HEADS * C_QK_DIM
    return pl.pallas_call(
        functools.partial(_in_odd_kernel, n_sub=max(1, tm // PROJ_SUB_ROWS)),
        grid=(n // tm,),
        in_specs=[pl.BlockSpec((tm, D_MODEL), lambda i: (i, 0)),
                  pl.BlockSpec((1, D_MODEL), lambda i: (0, 0)),
                  pl.BlockSpec((D_MODEL, ODD_IN_PAD), lambda i: (0, 0),
                               pipeline_mode=pl.Buffered(1)),
                  pl.BlockSpec((LANES, qk), lambda i: (0, 0)),
                  pl.BlockSpec((1, qk), lambda i: (0, 0))],
        out_specs=[pl.BlockSpec((tm, ODD_Z), lambda i: (i, 0)),
                   pl.BlockSpec((tm, qk), lambda i: (i, 0))],
        out_shape=[jax.ShapeDtypeStruct((n, ODD_Z), BF16),
                   jax.ShapeDtypeStruct((n, qk), F32)],
        compiler_params=_params(1),
    )(x2d, gamma.reshape(1, D_MODEL), w_pad, wlr_pad, b_gate.reshape(1, qk))


def _mem_kv_kernel(x_ref, g_ref, wk_ref, wv_ref, k5_ref, v5_ref, kb_ref, vb_ref, *, seqs, mem_len):
    h = _rms(x_ref[...], g_ref[0]).astype(BF16)
    for w_ref, o5_ref, ob_ref in ((wk_ref, k5_ref, kb_ref), (wv_ref, v5_ref, vb_ref)):
        y = _dot(h, w_ref[0])
        ob_ref[0] = y.astype(BF16)
        for s in range(seqs):
            for hd in range(M_HEADS):
                o5_ref[0, s, :, hd, :] = y[s * mem_len:(s + 1) * mem_len,
                                           hd * M_HEAD_DIM:(hd + 1) * M_HEAD_DIM]


def _mem_kv(mem, ln_mem, wk, wv, seqs):
    bsz, mem_len, _ = mem.shape
    depth = ln_mem.shape[0]
    width = M_HEADS * M_HEAD_DIM
    n = bsz * mem_len
    tm = seqs * mem_len
    out5 = jax.ShapeDtypeStruct((depth, bsz, mem_len, M_HEADS, M_HEAD_DIM), F32)
    outb = jax.ShapeDtypeStruct((depth, n, width), BF16)
    spec5 = pl.BlockSpec((1, seqs, mem_len, M_HEADS, M_HEAD_DIM), lambda l, i: (l, i, 0, 0, 0))
    specb = pl.BlockSpec((1, tm, width), lambda l, i: (l, i, 0))
    wspec = pl.BlockSpec((1, D_MODEL, width), lambda l, i: (l, 0, 0))
    return pl.pallas_call(
        functools.partial(_mem_kv_kernel, seqs=seqs, mem_len=mem_len),
        grid=(depth, n // tm),
        in_specs=[pl.BlockSpec((tm, D_MODEL), lambda l, i: (i, 0)),
                  pl.BlockSpec((1, 1, D_MODEL), lambda l, i: (l, 0, 0)),
                  wspec, wspec],
        out_specs=[spec5, spec5, specb, specb],
        out_shape=[out5, out5, outb, outb],
        compiler_params=_params(2),
    )(mem.reshape(n, D_MODEL), ln_mem.reshape(depth, 1, D_MODEL), wk, wv)


def _diff_attn_kernel(lam_ref, q_ref, k_ref, v_ref, bias_ref, subln_ref, o_ref,
                      vt_sc, s_sc, acc_sc, *, tq, nd, hp, out_scale):
    i = pl.program_id(2)
    dv = A_V_DIM
    chains = [(h, c) for h in range(hp) for c in range(2)]
    n_ch = len(chains)

    @pl.when(i == 0)
    def _():
        for h in range(hp):
            for jj in range(nd):
                vt_sc[h, jj, :dv, :] = v_ref[0, jj * tq:(jj + 1) * tq, h * dv:(h + 1) * dv].T
                vt_sc[h, jj, dv:, :] = jnp.ones((ONES_ROWS, tq), BF16)

    row = lax.broadcasted_iota(jnp.int32, (dv, tq), 0)
    qzt = []
    for h in range(hp):
        qt = (q_ref[0, :, h * dv:(h + 1) * dv] * (A_QK_DIM ** -0.5)).T
        for c in range(2):
            qzt.append(jnp.where((row >= A_QK_DIM) == (c == 1), qt, jnp.zeros_like(qt)))

    def raw_scores(j, n):
        h = chains[n][0]
        start = pl.multiple_of(j * tq, tq)
        return _dot(k_ref[0, pl.ds(start, tq), h * dv:(h + 1) * dv], qzt[n])

    kk = lax.broadcasted_iota(jnp.int32, (tq, tq), 0)
    qq = lax.broadcasted_iota(jnp.int32, (tq, tq), 1)
    diag_penalty = jnp.where((kk // CHUNK) <= (qq // CHUNK), 0.0, NEG_INF)

    def pass1(js, ms):
        ms = list(ms)
        items = [(j, n) for j in js for n in range(n_ch)]
        raw_next = raw_scores(*items[0])
        for idx, (j, n) in enumerate(items):
            raw = raw_next
            if idx + 1 < len(items):
                raw_next = raw_scores(*items[idx + 1])
            h, c = chains[n]
            if c == 0:
                bias = bias_ref[h, j - i + (nd - 1)] + diag_penalty * jnp.where(j == i, 1.0, 0.0)
            s = (raw + bias) * LOG2_E
            s_sc[n, j] = s
            ms[n] = jnp.maximum(ms[n], jnp.max(s, axis=0, keepdims=True))
        return tuple(ms)

    def pass2(js):
        items = [(j, n) for j in js for n in range(n_ch)]
        p_next = jnp.exp2(s_sc[items[0][1], items[0][0]] - m_fin[items[0][1]])
        for idx, (j, n) in enumerate(items):
            p = p_next
            if idx + 1 < len(items):
                jn, nn = items[idx + 1]
                p_next = jnp.exp2(s_sc[nn, jn] - m_fin[nn])
            acc_sc[n] += _dot(vt_sc[chains[n][0], j], p.astype(BF16))

    n_tiles = i + 1
    plan = []
    first = 0
    for width in (4, 2, 1):
        trips = (n_tiles - first) // width
        plan.append((width, trips, first))
        first = first + trips * width

    ms = tuple(jnp.full((1, tq), NEG_INF, F32) for _ in range(n_ch))
    for width, trips, start in plan:
        ms = lax.fori_loop(
            0, trips,
            lambda tt, m, width=width, start=start:
                pass1([start + tt * width + u for u in range(width)], m), ms)
    m_fin = ms

    acc_sc[...] = jnp.zeros(acc_sc.shape, F32)
    for width, trips, start in plan:
        def trip2(tt, carry, width=width, start=start):
            pass2([start + tt * width + u for u in range(width)])
            return carry

        lax.fori_loop(0, trips, trip2, 0)
    for h in range(hp):
        acc0 = acc_sc[2 * h]
        acc1 = acc_sc[2 * h + 1]
        out0 = acc0[:dv] / acc0[dv:dv + 1]
        out1 = acc1[:dv] / acc1[dv:dv + 1]
        o = (out0 - lam_ref[0, 0] * out1).T
        o_ref[:, h * dv:(h + 1) * dv] = (_head_rms(o) * subln_ref[...] * out_scale).astype(BF16)


def _diff_attn(lam, z3, bias_tiles, subln, tq, hp, out_scale):
    bsz, t, _ = z3.shape
    nd = t // tq
    width = hp * A_V_DIM
    groups = A_HEADS // hp
    return pl.pallas_call(
        functools.partial(_diff_attn_kernel, tq=tq, nd=nd, hp=hp, out_scale=out_scale),
        grid=(bsz, groups, nd),
        in_specs=[pl.BlockSpec(memory_space=pltpu.SMEM),
                  pl.BlockSpec((1, tq, width), lambda b, g, i: (b, i, g)),
                  pl.BlockSpec((1, t, width), lambda b, g, i: (b, 0, groups + g)),
                  pl.BlockSpec((1, t, width), lambda b, g, i: (b, 0, 2 * groups + g)),
                  pl.BlockSpec((hp, nd, tq, tq), lambda b, g, i: (g, 0, 0, 0),
                               pipeline_mode=pl.Buffered(1)),
                  pl.BlockSpec((1, A_V_DIM), lambda b, g, i: (0, 0))],
        out_specs=pl.BlockSpec((tq, width), lambda b, g, i: (b * nd + i, g)),
        out_shape=jax.ShapeDtypeStruct((bsz * t, A_HEADS * A_V_DIM), BF16),
        scratch_shapes=[pltpu.VMEM((hp, nd, A_V_DIM + ONES_ROWS, tq), BF16),
                        pltpu.VMEM((2 * hp, nd, tq, tq), F32),
                        pltpu.VMEM((2 * hp, A_V_DIM + ONES_ROWS, tq), F32)],
        compiler_params=_params(3),
    )(lam, z3, z3, z3, bias_tiles, subln.reshape(1, A_V_DIM))


def _diff_attn_cached_kernel(lam_ref, q_ref, kn_ref, vn_ref, kp_ref, vp_ref, bp_ref, bn_ref,
                             subln_ref, o_ref, *, t, past, out_scale):
    lane = lax.broadcasted_iota(jnp.int32, (t, A_V_DIM), 1)
    qpos_p = past + lax.broadcasted_iota(jnp.int32, (2 * t, past), 0) % t
    kpos_p = lax.broadcasted_iota(jnp.int32, (2 * t, past), 1)
    ok_p = (kpos_p // CHUNK) <= (qpos_p // CHUNK)
    qpos_n = past + lax.broadcasted_iota(jnp.int32, (2 * t, t), 0) % t
    kpos_n = past + lax.broadcasted_iota(jnp.int32, (2 * t, t), 1)
    ok_n = (kpos_n // CHUNK) <= (qpos_n // CHUNK)
    for h in range(A_HEADS):
        cs = slice(h * A_V_DIM, (h + 1) * A_V_DIM)
        q = q_ref[0, :, cs] * (A_QK_DIM ** -0.5)
        zero = jnp.zeros_like(q)
        qp = jnp.concatenate([jnp.where(lane < A_QK_DIM, q, zero),
                              jnp.where(lane >= A_QK_DIM, q, zero)], axis=0)
        kp = kp_ref[0, :, h, :].astype(BF16)
        vp = vp_ref[0, :, h, :].astype(BF16)
        kn = kn_ref[0, :, cs]
        vn = vn_ref[0, :, cs]
        bp = bp_ref[h]
        bn = bn_ref[h]
        sp = jnp.where(ok_p, _dot_nt(qp, kp) + jnp.concatenate([bp, bp], axis=0), NEG_INF)
        sn = jnp.where(ok_n, _dot_nt(qp, kn) + jnp.concatenate([bn, bn], axis=0), NEG_INF)
        m = jnp.maximum(jnp.max(sp, axis=-1, keepdims=True), jnp.max(sn, axis=-1, keepdims=True))
        pp = jnp.exp(sp - m)
        pn = jnp.exp(sn - m)
        l = jnp.sum(pp, axis=-1, keepdims=True) + jnp.sum(pn, axis=-1, keepdims=True)
        out = (_dot(pp.astype(BF16), vp) + _dot(pn.astype(BF16), vn)) / l
        o = out[:t] - lam_ref[0, 0] * out[t:]
        o_ref[:, cs] = (_head_rms(o) * subln_ref[...] * out_scale).astype(BF16)


def _diff_attn_cached(lam, z3, past_k, past_v, bias_past, bias_new, subln, out_scale):
    bsz, t, _ = z3.shape
    past = past_k.shape[1]
    width = A_HEADS * A_V_DIM
    cache_spec = pl.BlockSpec((1, past, A_HEADS, A_V_DIM), lambda b: (b, 0, 0, 0))
    return pl.pallas_call(
        functools.partial(_diff_attn_cached_kernel, t=t, past=past, out_scale=out_scale),
        grid=(bsz,),
        in_specs=[pl.BlockSpec(memory_space=pltpu.SMEM),
                  pl.BlockSpec((1, t, width), lambda b: (b, 0, 0)),
                  pl.BlockSpec((1, t, width), lambda b: (b, 0, 1)),
                  pl.BlockSpec((1, t, width), lambda b: (b, 0, 2)),
                  cache_spec, cache_spec,
                  pl.BlockSpec((A_HEADS, t, past), lambda b: (0, 0, 0)),
                  pl.BlockSpec((A_HEADS, t, t), lambda b: (0, 0, 0)),
                  pl.BlockSpec((1, A_V_DIM), lambda b: (0, 0))],
        out_specs=pl.BlockSpec((t, width), lambda b: (b, 0)),
        out_shape=jax.ShapeDtypeStruct((bsz * t, width), BF16),
        compiler_params=_params(1),
    )(lam, z3, z3, z3, past_k, past_v, bias_past, bias_new, subln.reshape(1, A_V_DIM))


def _retention_kernel(q_ref, k_ref, v_ref, gt_ref, cos_ref, sin_ref, dec_ref, qd_ref, kd_ref,
                      cd_ref, s0_ref, o_ref, s_out_ref, s_sc):
    t = pl.program_id(1)

    @pl.when(t == 0)
    def _():
        s_sc[...] = s0_ref[0]

    d = B_QK_DIM
    half = d // 2
    c = dec_ref.shape[1]
    for ci in range(q_ref.shape[1] // c):
        rs = slice(ci * c, (ci + 1) * c)
        cos = cos_ref[rs, :]
        sin = sin_ref[rs, :]
        for h in range(B_HEADS):
            cs = slice(h * d, (h + 1) * d)
            q = q_ref[0, rs, cs].astype(F32)
            k = k_ref[0, rs, cs].astype(F32)
            qr = (q * cos + pltpu.roll(q, half, 1) * sin) * (d ** -0.5)
            kr = k * cos + pltpu.roll(k, half, 1) * sin
            v = v_ref[0, rs, cs]
            att = _dot_nt(qr.astype(BF16), kr.astype(BF16)) * dec_ref[h]
            s = s_sc[h]
            o = _dot(att.astype(BF16), v) + _dot((qr * qd_ref[h]).astype(BF16), s.astype(BF16))
            s_sc[h] = s * cd_ref[h] + _dot_tn((kr * kd_ref[h]).astype(BF16), v)
            gt = gt_ref[0, rs, cs].astype(F32)
            o_ref[rs, cs] = (_head_rms(o) * _silu(gt)).astype(BF16)

    @pl.when(t == pl.num_programs(1) - 1)
    def _():
        s_out_ref[0] = s_sc[...]


def _retention(z3, cos, sin, consts, s0, c, tb):
    bsz, t, _ = z3.shape
    nt = t // tb
    dec, qd, kd, cd = consts
    d = B_QK_DIM
    width = B_HEADS * d
    base = 3
    full3 = lambda b, i: (0, 0, 0)
    return pl.pallas_call(
        _retention_kernel,
        grid=(bsz, nt),
        in_specs=[pl.BlockSpec((1, tb, width), lambda b, i: (b, i, base)),
                  pl.BlockSpec((1, tb, width), lambda b, i: (b, i, base + 1)),
                  pl.BlockSpec((1, tb, width), lambda b, i: (b, i, base + 2)),
                  pl.BlockSpec((1, tb, width), lambda b, i: (b, i, base + 3)),
                  pl.BlockSpec((tb, d), lambda b, i: (i, 0)),
                  pl.BlockSpec((tb, d), lambda b, i: (i, 0)),
                  pl.BlockSpec((B_HEADS, c, c), full3),
                  pl.BlockSpec((B_HEADS, c, d), full3),
                  pl.BlockSpec((B_HEADS, c, d), full3),
                  pl.BlockSpec((B_HEADS, 1, d), full3),
                  pl.BlockSpec((1, B_HEADS, d, d), lambda b, i: (b, 0, 0, 0))],
        out_specs=[pl.BlockSpec((tb, width), lambda b, i: (b * nt + i, 0)),
                   pl.BlockSpec((1, B_HEADS, d, d), lambda b, i: (b, 0, 0, 0))],
        out_shape=[jax.ShapeDtypeStruct((bsz * t, width), BF16),
                   jax.ShapeDtypeStruct((bsz, B_HEADS, d, d), F32)],
        scratch_shapes=[pltpu.VMEM((B_HEADS, d, d), F32)],
        compiler_params=_params(2),
    )(z3, z3, z3, z3, cos, sin, dec, qd, kd, cd, s0)


def _retention_consts(c):
    log_g = jnp.log1p(-jnp.exp2(-5.0 - jnp.arange(B_HEADS, dtype=F32)))
    idx = jnp.arange(c, dtype=F32)
    dist = idx[:, None] - idx[None, :]
    dec = jnp.where(dist >= 0, jnp.exp(jnp.maximum(dist, 0.0)[None] * log_g[:, None, None]), 0.0)
    qd = jnp.exp((idx[None, :] + 1.0) * log_g[:, None])
    kd = jnp.exp((c - 1.0 - idx)[None, :] * log_g[:, None])
    cd = jnp.exp(c * log_g)
    bc = lambda a: jnp.broadcast_to(a[..., None], a.shape + (B_QK_DIM,))
    return dec, bc(qd), bc(kd), bc(cd[:, None])


def _rotary_tables(pos):
    half = B_QK_DIM // 2
    inv = ROPE_BASE ** (-jnp.arange(half, dtype=F32) / half)
    ang = pos.astype(F32)[:, None] * inv[None, :]
    cos, sin = jnp.cos(ang), jnp.sin(ang)
    return jnp.concatenate([cos, cos], axis=-1), jnp.concatenate([-sin, sin], axis=-1)


def _gla_levels(c):
    return [c >> (l + 1) for l in range(int(math.log2(c)))]


def _gla_consts(c):
    levels = _gla_levels(c)
    rows = np.arange(c)
    mats = []
    for s in levels:
        ref = (rows // (2 * s)) * 2 * s + s - 1
        a = np.zeros((c, c), np.float32)
        for i in range(c):
            if i & s:
                a[i, ref[i] + 1:i + 1] = 1.0
            else:
                a[i, i + 1:ref[i] + 1] = 1.0
        mats.append(a)
    mats.append(np.tril(np.ones((c, c), np.float32)))
    mats.append(np.triu(np.ones((c, c), np.float32), 1))
    lv = np.full((c, c), -1, np.int32)
    for i in range(c):
        lv[i, i] = len(levels)
        for j in range(i):
            lv[i, j] = levels.index(1 << int(math.floor(math.log2(i ^ j))))
    return jnp.asarray(np.concatenate(mats, axis=0), BF16), jnp.asarray(lv)


def _gla_kernel(q_ref, k_ref, v_ref, r_ref, g_ref, s0_ref, a_ref, lv_ref, nw_ref,
                o_ref, s_out_ref, st_sc, *, c, n_chunks):
    t = pl.program_id(1)
    dk, dv = C_QK_DIM, C_V_DIM

    @pl.when(t == 0)
    def _():
        for h in range(C_HEADS):
            st_sc[h] = s0_ref[0, h].T

    levels = _gla_levels(c)
    n_lv = len(levels)
    half_blocks = (c // 2) % LANES == 0
    lv = lv_ref[...]
    row = lax.broadcasted_iota(jnp.int32, (c, dk), 0)

    def gate_sums(ci):
        return _dot(a_ref[...], g_ref[0, ci * c:(ci + 1) * c, :].astype(BF16))

    xs_next = gate_sums(0)
    for ci, h in [(ci, h) for ci in range(n_chunks) for h in range(C_HEADS)]:
        rs = slice(ci * c, (ci + 1) * c)
        ks = slice(h * dk, (h + 1) * dk)
        vs = slice(h * dv, (h + 1) * dv)
        if h == 0:
            xs = xs_next
        if h == 1 and ci + 1 < n_chunks:
            xs_next = gate_sums(ci + 1)
        x = xs[:, ks]
        q = q_ref[0, rs, ks].astype(F32) * (dk ** -0.5)
        k = k_ref[0, rs, ks].astype(F32)
        v = v_ref[0, rs, vs]
        if half_blocks:
            hc = c // 2
            lvh = lv[:hc, :hc]
            diag = [jnp.zeros((hc, hc), F32), jnp.zeros((hc, hc), F32)]
            qb, kb = q.astype(BF16), k.astype(BF16)
            pending = (n_lv, [_dot_nt(qb[:hc], kb[:hc]), _dot_nt(qb[hc:], kb[hc:])])
        else:
            att = jnp.zeros((c, c), F32)
            pending = (n_lv, _dot_nt(q.astype(BF16), k.astype(BF16)))
        for l, s in enumerate(levels):
            e = jnp.exp(x[l * c:(l + 1) * c])
            up = (row & s) != 0
            mix = jnp.where(up, q, k) * e
            qt = jnp.where(up, mix, 0.0).astype(BF16)
            kt = jnp.where(up, 0.0, mix).astype(BF16)
            if not half_blocks:
                prod = _dot_nt(qt, kt)
                att = jnp.where(lv == pending[0], pending[1], att)
            elif l == 0:
                lower_left = _dot_nt(qt[hc:], kt[:hc])
                continue
            else:
                prod = [_dot_nt(qt[:hc], kt[:hc]), _dot_nt(qt[hc:], kt[hc:])]
                diag = [jnp.where(lvh == pending[0], pending[1][u], diag[u]) for u in range(2)]
            pending = (l, prod)
        if half_blocks:
            diag = [jnp.where(lvh == pending[0], pending[1][u], diag[u]) for u in range(2)]
            att = jnp.concatenate(
                [jnp.concatenate([diag[0], jnp.zeros((hc, hc), F32)], axis=1),
                 jnp.concatenate([lower_left, diag[1]], axis=1)], axis=0)
        else:
            att = jnp.where(lv == pending[0], pending[1], att)
        b = x[n_lv * c:(n_lv + 1) * c]
        rem = x[(n_lv + 1) * c:(n_lv + 2) * c]
        st = st_sc[h]
        o = _dot(att.astype(BF16), v) + _dot_nt((q * jnp.exp(b)).astype(BF16), st.astype(BF16))
        kd = (k * jnp.exp(rem)).astype(BF16)
        st_sc[h] = st * jnp.exp(b[c - 1:c, :]) + _dot_tn(v, kd)
        r = r_ref[0, rs, vs].astype(F32)
        o_ref[rs, vs] = (_head_rms(o) * nw_ref[...] * _silu(r)).astype(BF16)

    @pl.when(t == pl.num_programs(1) - 1)
    def _():
        for h in range(C_HEADS):
            s_out_ref[0, h] = st_sc[h].T


def _gla(z3, log_a3, s0, a_stack, lv, norm_w, c, tb):
    bsz, t, _ = z3.shape
    nt = t // tb
    dk, dv = C_QK_DIM, C_V_DIM
    qk_w, v_w = C_HEADS * dk, C_HEADS * dv
    const = lambda b, i: (0, 0)
    return pl.pallas_call(
        functools.partial(_gla_kernel, c=c, n_chunks=tb // c),
        grid=(bsz, nt),
        in_specs=[pl.BlockSpec((1, tb, qk_w), lambda b, i: (b, i, 0)),
                  pl.BlockSpec((1, tb, qk_w), lambda b, i: (b, i, 1)),
                  pl.BlockSpec((1, tb, v_w), lambda b, i: (b, i, 1)),
                  pl.BlockSpec((1, tb, v_w), lambda b, i: (b, i, 2)),
                  pl.BlockSpec((1, tb, qk_w), lambda b, i: (b, i, 0)),
                  pl.BlockSpec((1, C_HEADS, dk, dv), lambda b, i: (b, 0, 0, 0)),
                  pl.BlockSpec(a_stack.shape, const),
                  pl.BlockSpec((c, c), const),
                  pl.BlockSpec((1, dv), const)],
        out_specs=[pl.BlockSpec((tb, v_w), lambda b, i: (b * nt + i, 0)),
                   pl.BlockSpec((1, C_HEADS, dk, dv), lambda b, i: (b, 0, 0, 0))],
        out_shape=[jax.ShapeDtypeStruct((bsz * t, v_w), BF16),
                   jax.ShapeDtypeStruct((bsz, C_HEADS, dk, dv), F32)],
        scratch_shapes=[pltpu.VMEM((C_HEADS, dv, dk), F32)],
        compiler_params=_params(2),
    )(z3, z3, z3, z3, log_a3, s0, a_stack, lv, norm_w.reshape(1, dv))


def _cross_kernel(*refs, n_mix, n_sub):
    x_ref = refs[0]
    a_refs = refs[1:1 + n_mix]
    w_refs = refs[1 + n_mix:1 + 2 * n_mix]
    g_ref, wq_ref, mk_ref, mv_ref, wo_ref, o_ref = refs[1 + 2 * n_mix:]
    def head(ref, hd):
        if len(ref.shape) == 5:
            return ref[0, 0, :, hd, :].astype(BF16)
        return ref[0, 0, :, hd * M_HEAD_DIM:(hd + 1) * M_HEAD_DIM].astype(BF16)

    tr = x_ref.shape[0] // n_sub
    xs = []
    for sub in range(n_sub):
        rows = slice(sub * tr, (sub + 1) * tr)
        x = x_ref[rows, :]
        for a_ref, w_ref in zip(a_refs, w_refs):
            x = x + _dot(a_ref[rows, :], w_ref[...])
        xs.append(x)
    for sub in range(n_sub):
        x = xs[sub]
        h = _rms(x, g_ref[...]).astype(BF16)
        q = (_dot(h, wq_ref[0]) * (M_HEAD_DIM ** -0.5)).astype(BF16)
        outs = []
        for hd in range(M_HEADS):
            cs = slice(hd * M_HEAD_DIM, (hd + 1) * M_HEAD_DIM)
            s = _dot_nt(q[:, cs], head(mk_ref, hd))
            p = jnp.exp(s - jnp.max(s, axis=-1, keepdims=True))
            l = jnp.sum(p, axis=-1, keepdims=True)
            outs.append((_dot(p.astype(BF16), head(mv_ref, hd)) / l).astype(BF16))
        o_ref[sub * tr:(sub + 1) * tr, :] = x + _dot(jnp.concatenate(outs, axis=1), wo_ref[0])


def _cross(x2d, mix_list, w_mix_list, gamma, wq, mk, mv, wo, layer, t, tm):
    n = x2d.shape[0]
    nt = t // tm
    width = M_HEADS * M_HEAD_DIM
    rows = lambda b, i: (b * nt + i, 0)
    const = lambda b, i: (0, 0)
    of_layer = lambda b, i: (layer, 0, 0)
    mem_spec = pl.BlockSpec((1, 1) + mk.shape[2:], lambda b, i: (layer, b) + (0,) * (mk.ndim - 2))
    in_specs = [pl.BlockSpec((tm, D_MODEL), rows)]
    in_specs += [pl.BlockSpec((tm, a.shape[1]), rows) for a in mix_list]
    in_specs += [pl.BlockSpec(w.shape, const) for w in w_mix_list]
    in_specs += [pl.BlockSpec((1, D_MODEL), const),
                 pl.BlockSpec((1, D_MODEL, width), of_layer),
                 mem_spec, mem_spec,
                 pl.BlockSpec((1, width, D_MODEL), of_layer)]
    return pl.pallas_call(
        functools.partial(_cross_kernel, n_mix=len(mix_list), n_sub=max(1, tm // CROSS_SUB_ROWS)),
        grid=(n // t, nt),
        in_specs=in_specs,
        out_specs=pl.BlockSpec((tm, D_MODEL), rows),
        out_shape=jax.ShapeDtypeStruct((n, D_MODEL), F32),
        compiler_params=_params(2),
    )(x2d, *mix_list, *w_mix_list, gamma.reshape(1, D_MODEL), wq, mk, mv, wo)


def _ffn_kernel(x_ref, g_ref, wg_ref, wu_ref, cw_ref, cb_ref, prev_ref, wd_ref, *rest,
                tm, n_sub, final_norm):
    if final_norm:
        gf_ref, o_ref, tail_ref, carry = rest
    else:
        o_ref, tail_ref, carry = rest
    t = pl.program_id(1)
    lo = SUBLANES - 2

    @pl.when(t == 0)
    def _():
        carry[lo:SUBLANES, :] = prev_ref[0]

    row = lax.broadcasted_iota(jnp.int32, (SUBLANES, FF_CHUNK), 0)
    n_chunks = D_FF // FF_CHUNK
    chunk = lambda c: slice(c * FF_CHUNK, (c + 1) * FF_CHUNK)
    tr = tm // n_sub
    h_next = _rms(x_ref[0:tr, :], g_ref[...]).astype(BF16)
    for sub in range(n_sub):
        rows = slice(sub * tr, (sub + 1) * tr)
        h = h_next
        acc = x_ref[rows, :]
        nxt = (_dot(h, wg_ref[0, :, chunk(0)]), _dot(h, wu_ref[0, :, chunk(0)]))
        pending = []
        for c in range(n_chunks):
            cs = chunk(c)
            gate, up = nxt
            if c + 1 < n_chunks:
                nxt = (_dot(h, wg_ref[0, :, chunk(c + 1)]), _dot(h, wu_ref[0, :, chunk(c + 1)]))
            if c == n_chunks // 2 and sub + 1 < n_sub:
                h_next = _rms(x_ref[(sub + 1) * tr:(sub + 2) * tr, :], g_ref[...]).astype(BF16)
            p1 = carry[SUBLANES - 1:SUBLANES, cs]
            p2 = carry[lo:lo + 1, cs]
            r1 = pltpu.roll(gate, 1, 0)
            r2 = pltpu.roll(gate, 2, 0)
            h1 = jnp.where(row == 0, p1, r1[:SUBLANES])
            h2 = jnp.where(row == 0, p2, jnp.where(row == 1, p1, r2[:SUBLANES]))
            g1 = jnp.concatenate([h1, r1[SUBLANES:]], axis=0)
            g2 = jnp.concatenate([h2, r2[SUBLANES:]], axis=0)
            carry[:, cs] = gate[tr - SUBLANES:tr]
            conv = cb_ref[:, cs] + cw_ref[0:1, cs] * g2
            conv = conv + cw_ref[1:2, cs] * g1
            conv = conv + cw_ref[2:3, cs] * gate
            pending.append((_gelu_tanh(conv) * up).astype(BF16))
            if len(pending) == DOWN_GROUP or c == n_chunks - 1:
                lo_col = (c + 1 - len(pending)) * FF_CHUNK
                acc = acc + _dot(jnp.concatenate(pending, axis=1),
                                 wd_ref[0, lo_col:(c + 1) * FF_CHUNK, :])
                pending = []
        if final_norm:
            o_ref[rows, :] = _rms(acc, gf_ref[...])
        else:
            o_ref[rows, :] = acc

    @pl.when(t == pl.num_programs(1) - 1)
    def _():
        tail_ref[0] = carry[lo:SUBLANES, :]


def _ffn(x2d, gamma, wg, wu, conv_w, conv_b, prev, wd, layer, t, tm, final_gamma=None):
    n = x2d.shape[0]
    nt = t // tm
    bsz = n // t
    const = lambda b, i: (0, 0)
    of_layer = lambda b, i: (layer, 0, 0)
    resident = dict(pipeline_mode=pl.Buffered(1))
    in_specs = [pl.BlockSpec((tm, D_MODEL), lambda b, i: (b * nt + i, 0)),
                pl.BlockSpec((1, D_MODEL), const),
                pl.BlockSpec((1, D_MODEL, D_FF), of_layer, **resident),
                pl.BlockSpec((1, D_MODEL, D_FF), of_layer, **resident),
                pl.BlockSpec((3, D_FF), const),
                pl.BlockSpec((1, D_FF), const),
                pl.BlockSpec((1, 2, D_FF), lambda b, i: (b, 0, 0)),
                pl.BlockSpec((1, D_FF, D_MODEL), of_layer, **resident)]
    args = [x2d, gamma.reshape(1, D_MODEL), wg, wu, conv_w, conv_b.reshape(1, D_FF), prev, wd]
    if final_gamma is not None:
        in_specs.append(pl.BlockSpec((1, D_MODEL), const))
        args.append(final_gamma.reshape(1, D_MODEL))
    return pl.pallas_call(
        functools.partial(_ffn_kernel, tm=tm, n_sub=max(1, tm // FFN_SUB_ROWS),
                          final_norm=final_gamma is not None),
        grid=(bsz, nt),
        in_specs=in_specs,
        out_specs=[pl.BlockSpec((tm, D_MODEL), lambda b, i: (b * nt + i, 0)),
                   pl.BlockSpec((1, 2, D_FF), lambda b, i: (b, 0, 0))],
        out_shape=[jax.ShapeDtypeStruct((n, D_MODEL), F32),
                   jax.ShapeDtypeStruct((bsz, 2, D_FF), F32)],
        scratch_shapes=[pltpu.VMEM((SUBLANES, D_FF), F32)],
        compiler_params=_params(2),
    )(*args)


def _rel_bucket(rel):
    nb = REL_BUCKETS // 2
    max_exact = nb // 2
    n = jnp.abs(rel)
    nf = jnp.maximum(n, 1).astype(F32)
    large = max_exact + (jnp.log(nf / max_exact) / math.log(REL_MAX_DIST / max_exact)
                         * (nb - max_exact)).astype(jnp.int32)
    large = jnp.minimum(large, nb - 1)
    return jnp.where(rel > 0, nb, 0) + jnp.where(n < max_exact, n, large)


def _toeplitz_kernel(w_ref, o_ref, *, rows, cols):
    w = w_ref[0]
    x = jnp.broadcast_to(w, (rows, w.shape[1]))
    o_ref[0] = pltpu.roll(x, 0, 1, stride=1, stride_axis=0)[:, :cols]


def _toeplitz(w, rows, cols):
    groups, period = w.shape
    return pl.pallas_call(
        functools.partial(_toeplitz_kernel, rows=rows, cols=cols),
        grid=(groups,),
        in_specs=[pl.BlockSpec((1, 1, period), lambda g: (g, 0, 0))],
        out_specs=pl.BlockSpec((1, rows, cols), lambda g: (g, 0, 0)),
        out_shape=jax.ShapeDtypeStruct((groups, rows, cols), F32),
        compiler_params=_params(1),
    )(w.reshape(groups, 1, period))


def _bias_tiles(rel_bias, tq, nd):
    period = 2 * tq
    n = np.arange(period)
    rel = np.stack([np.where(n < tq, (d - (nd - 1)) * tq - n, (d - (nd - 1)) * tq + period - n)
                    for d in range(nd)]).astype(np.int32)
    w = jnp.transpose(rel_bias[_rel_bucket(jnp.asarray(rel))], (2, 0, 1)).astype(F32)
    return _toeplitz(w.reshape(A_HEADS * nd, period), tq, tq).reshape(A_HEADS, nd, tq, tq)


def _bias_rows(rel_bias, t, past):
    n_keys = past + t
    period = -(-(n_keys + t) // LANES) * LANES
    m = np.arange(period)
    rel = np.where(m < n_keys, m - past, m - period - past).astype(np.int32)
    w = jnp.transpose(rel_bias[_rel_bucket(jnp.asarray(rel))], (1, 0)).astype(F32)
    bias = _toeplitz(w, t, n_keys)
    return bias[:, :, :past], bias[:, :, past:]


def _forward(x, past_k, past_v, ret_state, gla_state, conv_prev, mem_k, mem_v, p, w):
    bsz, t, _ = x.shape
    n = bsz * t
    past = 0 if past_k is None else past_k.shape[2]
    tm = min(2 * PROJ_SUB_ROWS, n)
    tseq = min(2 * CROSS_SUB_ROWS, t)
    tffn = min(2 * FFN_SUB_ROWS, t)
    tq = min(ATTN_TILE, t)
    c_ret = min(RECURRENT_CHUNK, t)
    c_gla = min(RECURRENT_CHUNK, t)
    x2 = x.reshape(n, D_MODEL)
    pos = past + jnp.arange(t, dtype=jnp.int32)
    new_conv = []

    z, ak, av = _in_even(x2, p["ln_mix"][0], w["w_in_even"], tm)
    z3 = z.reshape(bsz, t, EVEN_IN)
    lam_init = 0.8 - 0.6 * math.exp(-0.3 * 0)
    lam = (jnp.exp(jnp.sum(p["diff_lq1"][0].astype(F32) * p["diff_lk1"][0].astype(F32)))
           - jnp.exp(jnp.sum(p["diff_lq2"][0].astype(F32) * p["diff_lk2"][0].astype(F32)))
           + lam_init).reshape(1, 1).astype(F32)
    if past == 0:
        o_a = _diff_attn(lam, z3, _bias_tiles(p["rel_bias"], tq, t // tq), p["diff_subln"][0],
                         tq, A_HEADS, 1.0 - lam_init)
    else:
        bias_past, bias_new = _bias_rows(p["rel_bias"], t, past)
        o_a = _diff_attn_cached(lam, z3, past_k[0], past_v[0], bias_past, bias_new,
                                p["diff_subln"][0], 1.0 - lam_init)
    cos, sin = _rotary_tables(pos)
    o_b, s_ret = _retention(z3, cos, sin, _retention_consts(c_ret), ret_state[0], c_ret,
                            min(2 * c_ret, t))
    x2 = _cross(x2, [o_a, o_b], [w["w_out_even_a"], w["w_out_even_b"]], p["ln_cross"][0],
                w["w_cq"], mem_k, mem_v, w["w_co"], 0, t, tseq)
    x2, tail = _ffn(x2, p["ln_ffn"][0], w["w_ffn_gate"], w["w_ffn_up"], p["ffn_conv_w"][0],
                    p["ffn_conv_b"][0], conv_prev[0], w["w_ffn_down"], 0, t, tffn)
    new_conv.append(tail)

    zc, log_a = _in_odd(x2, p["ln_mix"][1], w["w_in_odd"], w["w_gate_lr"], p["b_gate"][0], tm)
    a_stack, lv = _gla_consts(c_gla)
    o_c, s_gla = _gla(zc.reshape(bsz, t, ODD_Z), log_a.reshape(bsz, t, C_HEADS * C_QK_DIM),
                      gla_state[0], a_stack, lv, p["gla_norm"][0], c_gla, min(2 * c_gla, t))
    x2 = _cross(x2, [o_c], [w["w_out_odd"]], p["ln_cross"][1],
                w["w_cq"], mem_k, mem_v, w["w_co"], 1, t, tseq)
    y, tail = _ffn(x2, p["ln_ffn"][1], w["w_ffn_gate"], w["w_ffn_up"], p["ffn_conv_w"][1],
                   p["ffn_conv_b"][1], conv_prev[1], w["w_ffn_down"], 1, t, tffn,
                   final_gamma=p["ln_final"])
    new_conv.append(tail)

    return (y.reshape(bsz, t, D_MODEL),
            ak.reshape(1, bsz, t, A_HEADS, A_V_DIM), av.reshape(1, bsz, t, A_HEADS, A_V_DIM),
            s_ret[None], s_gla[None], jnp.stack(new_conv))


def kernel(x_prompt, x_sample, cache_diff_k, cache_diff_v, state_retention, state_gla, cache_ffn_conv, cache_mem_k, cache_mem_v, mem_prompt, ln_mix, ln_cross, ln_ffn, ln_mem, ln_final, w_in_even, w_out_even, diff_lq1, diff_lk1, diff_lq2, diff_lk2, diff_subln, rel_bias, w_in_odd, w_gate_lr, b_gate, gla_norm, w_out_odd, w_cq, w_ck, w_cv, w_co, w_ffn_gate, w_ffn_up, ffn_conv_w, ffn_conv_b, w_ffn_down):
    p = dict(ln_mix=ln_mix, ln_cross=ln_cross, ln_ffn=ln_ffn, ln_final=ln_final,
             diff_lq1=diff_lq1, diff_lk1=diff_lk1, diff_lq2=diff_lq2, diff_lk2=diff_lk2,
             diff_subln=diff_subln, rel_bias=rel_bias, b_gate=b_gate, gla_norm=gla_norm,
             ffn_conv_w=ffn_conv_w, ffn_conv_b=ffn_conv_b)
    a_v = A_HEADS * A_V_DIM
    w = dict(
        w_in_even=w_in_even[0].astype(BF16),
        w_out_even_a=w_out_even[0, :a_v].astype(BF16),
        w_out_even_b=w_out_even[0, a_v:].astype(BF16),
        w_in_odd=jnp.pad(w_in_odd[0], ((0, 0), (0, ODD_IN_PAD - w_in_odd.shape[2]))).astype(BF16),
        w_gate_lr=jnp.pad(w_gate_lr[0], ((0, LANES - C_GATE_RANK), (0, 0))).astype(BF16),
        w_out_odd=w_out_odd[0].astype(BF16),
        w_cq=w_cq.astype(BF16), w_co=w_co.astype(BF16),
        w_ffn_gate=w_ffn_gate.astype(BF16), w_ffn_up=w_ffn_up.astype(BF16),
        w_ffn_down=w_ffn_down.astype(BF16))

    bp, mem_len, _ = mem_prompt.shape
    m_width = M_HEADS * M_HEAD_DIM
    depth = ln_mem.shape[0]
    mem_k_p, mem_v_p, mk_b, mv_b = _mem_kv(mem_prompt, ln_mem, w_ck.astype(BF16),
                                           w_cv.astype(BF16), seqs=2)

    dt = x_prompt.dtype
    zero_ret = jnp.zeros((1, bp, B_HEADS, B_QK_DIM, B_QK_DIM), dt)
    zero_gla = jnp.zeros((1, bp, C_HEADS, C_QK_DIM, C_V_DIM), dt)
    zero_conv = jnp.zeros((depth, bp, 2, D_FF), dt)
    y_p, dk_p, dv_p, ret_p, gla_p, conv_p = _forward(
        x_prompt, None, None, zero_ret, zero_gla, zero_conv,
        mk_b.reshape(depth, bp, mem_len, m_width), mv_b.reshape(depth, bp, mem_len, m_width), p, w)
    y_s, dk_s, dv_s, ret_s, gla_s, conv_s = _forward(
        x_sample, cache_diff_k, cache_diff_v, state_retention, state_gla, cache_ffn_conv,
        cache_mem_k, cache_mem_v, p, w)
    return (y_p, y_s, dk_p, dv_p, ret_p, gla_p, conv_p, mem_k_p, mem_v_p,
            dk_s, dv_s, ret_s, gla_s, conv_s)
```
